```python
import math
import jax, jax.numpy as jnp
from jax import lax
import numpy as np

D_MODEL = 2048
BATCH = 2
SEQ = 4096
DEPTH = 4
DEC_BATCH = 8
DEC_SEQ = 8
PAST_LEN = 16384
PAGE_SIZE = 128

N_A_LAYERS = DEPTH // 2
N_B_LAYERS = DEPTH - N_A_LAYERS
CHUNK = 128
D_GATE = 2 * D_MODEL
N_GROUPS_A = 16
GROUP_DIM_A = D_GATE // N_GROUPS_A
HEAD_DIM = 128
N_HEADS = D_MODEL // (2 * HEAD_DIM)
N_QK = 2 * N_HEADS
V_DIM = 2 * HEAD_DIM
D_FF = 4 * D_MODEL
ROPE_THETA = 10000.0
EPS = 1e-5
Q_BLOCK = 128
NEG_INF = -1e30

kernel_name = "yoco_gmlp_diffattn_step"


def rmsnorm(x, g):
    xf = x.astype(jnp.float32)
    y = xf * lax.rsqrt(jnp.mean(xf * xf, axis=-1, keepdims=True) + EPS)
    return (y * g.astype(jnp.float32)).astype(x.dtype)


def rope(x, pos):
    half = HEAD_DIM // 2
    inv = ROPE_THETA ** (-jnp.arange(half, dtype=jnp.float32) / half)
    ang = pos.astype(jnp.float32)[:, None] * inv[None, :]
    cos = jnp.cos(ang)[:, None, :]
    sin = jnp.sin(ang)[:, None, :]
    xf = x.astype(jnp.float32)
    x1, x2 = xf[..., :half], xf[..., half:]
    return jnp.concatenate([x1 * cos - x2 * sin, x2 * cos + x1 * sin], axis=-1).astype(x.dtype)


def chunk_gmlp(x, w_in, w_s, b_s, w_out):
    B, T, _ = x.shape
    L = T if T <= CHUNK else CHUNK
    n_chunks = T // L
    z = jax.nn.gelu(x @ w_in, approximate=False)
    u, v = z[..., :D_GATE], z[..., D_GATE:]
    mask = jnp.tril(jnp.ones((L, L), dtype=bool))
    w = jnp.where(mask[None], w_s[:, :L, :L], 0.0)
    vc = v.reshape(B, n_chunks, L, N_GROUPS_A, GROUP_DIM_A)
    s = jnp.einsum('gts,bnsgc->bntgc', w, vc) + b_s[:, :L].T[None, None, :, :, None]
    s = s.reshape(B, T, D_GATE)
    return (u * s) @ w_out, v


def sqrelu_mlp(x, w_up, w_down):
    return jnp.square(jax.nn.relu(x @ w_up)) @ w_down


def diff_lambda(lq1, lk1, lq2, lk2, lam_init):
    f = jnp.float32
    return (jnp.exp(jnp.sum(lq1.astype(f) * lk1.astype(f)))
            - jnp.exp(jnp.sum(lq2.astype(f) * lk2.astype(f))) + lam_init)


def diff_combine(s, lam):
    p = jax.nn.softmax(s, axis=-1)
    B, _, Q, K = s.shape
    p = p.reshape(B, N_HEADS, 2, Q, K)
    return p[:, :, 0] - lam * p[:, :, 1]


def diff_attn_prompt(q, k, v, lam):
    B, T = q.shape[:2]
    qb = min(T, Q_BLOCK)
    nb = T // qb
    q_blocks = q.reshape(B, nb, qb, N_QK, HEAD_DIM).swapaxes(0, 1)
    k_pos = jnp.arange(T, dtype=jnp.int32)

    def block(args):
        q_blk, i = args
        q_pos = i * qb + jnp.arange(qb, dtype=jnp.int32)
        s = jnp.einsum('bqhd,bkhd->bhqk', q_blk, k, preferred_element_type=jnp.float32)
        s = jnp.where(k_pos[None, :] <= q_pos[:, None], s, NEG_INF)
        a = diff_combine(s, lam)
        return jnp.einsum('bhqk,bkhe->bqhe', a, v)

    o = lax.map(block, (q_blocks, jnp.arange(nb, dtype=jnp.int32)))
    return o.swapaxes(0, 1).reshape(B, T, N_HEADS, V_DIM)


def diff_attn_sample(q, k_new, v_new, k_pages, v_pages, lam):
    Bd, Tq = q.shape[:2]
    n_pages = k_pages.shape[1]
    past = n_pages * PAGE_SIZE
    s_past = jnp.einsum('bqhd,bpshd->bhqps', q, k_pages,
                        preferred_element_type=jnp.float32).reshape(Bd, N_QK, Tq, past)
    s_new = jnp.einsum('bqhd,bkhd->bhqk', q, k_new, preferred_element_type=jnp.float32)
    s_new = jnp.where(jnp.tril(jnp.ones((Tq, Tq), dtype=bool)), s_new, NEG_INF)
    a = diff_combine(jnp.concatenate([s_past, s_new], axis=-1), lam)
    a_past = a[..., :past].reshape(Bd, N_HEADS, Tq, n_pages, PAGE_SIZE)
    return (jnp.einsum('bhqps,bpshe->bqhe', a_past, v_pages)
            + jnp.einsum('bhqk,bkhe->bqhe', a[..., past:], v_new))


def diff_head_out(o, subln, lam_init, w_o):
    o = rmsnorm(o, subln) * (1.0 - lam_init)
    B, T = o.shape[:2]
    return o.reshape(B, T, N_HEADS * V_DIM).astype(w_o.dtype) @ w_o


def trunk(x, pos, k_pages, v_pages, p):
    h = x
    v_rows = []
    k_sh = None
    v_sh = None
    B, T, _ = x.shape
    for l in range(DEPTH):
        if l < N_A_LAYERS:
            y, v_chunk = chunk_gmlp(rmsnorm(h, p['norm_a'][l]), p['w_in_a'][l], p['w_s_a'][l],
                                    p['b_s_a'][l], p['w_out_a'][l])
            h = h + y
            v_rows.append(v_chunk)
        else:
            if l == N_A_LAYERS:
                kv_in = rmsnorm(h, p['norm_kv'])
                k_sh = rope((kv_in @ p['w_k']).reshape(B, T, N_QK, HEAD_DIM), pos)
                v_sh = (kv_in @ p['w_v']).reshape(B, T, N_HEADS, V_DIM)
            j = l - N_A_LAYERS
            lam_init = 0.8 - 0.6 * math.exp(-0.3 * l)
            lam = diff_lambda(p['lambda_q1'][j], p['lambda_k1'][j], p['lambda_q2'][j],
                              p['lambda_k2'][j], lam_init)
            q = rope((rmsnorm(h, p['norm_b'][j]) @ p['w_q'][j]).reshape(B, T, N_QK, HEAD_DIM), pos)
            q = q * (HEAD_DIM ** -0.5)
            if k_pages is None:
                o = diff_attn_prompt(q, k_sh, v_sh, lam)
            else:
                o = diff_attn_sample(q, k_sh, v_sh, k_pages, v_pages, lam)
            h = h + diff_head_out(o, p['subln_b'][j], lam_init, p['w_o_b'][j])
        h = h + sqrelu_mlp(rmsnorm(h, p['norm_ffn'][l]), p['w_up'][l], p['w_down'][l])
    return rmsnorm(h, p['norm_f']), k_sh, v_sh, jnp.stack(v_rows)


def setup_inputs(seed: int = 0) -> dict:
    key = jax.random.key(seed)
    ks = jax.random.split(key, 32)
    f = jnp.float32
    n_pages = PAST_LEN // PAGE_SIZE
    n_pool = (5 * DEC_BATCH * n_pages) // 4

    def nrm(k, shape, scale):
        return jax.random.normal(k, shape, f) * scale

    def gain(k, shape):
        return 1.0 + 0.02 * jax.random.normal(k, shape, f)

    perm = jax.random.permutation(ks[4], n_pool)[: DEC_BATCH * n_pages]
    return {
        "x_prompt": nrm(ks[0], (BATCH, SEQ, D_MODEL), 1.0),
        "x_sample": nrm(ks[1], (DEC_BATCH, DEC_SEQ, D_MODEL), 1.0),
        "cache_k": nrm(ks[2], (n_pool, PAGE_SIZE, N_QK, HEAD_DIM), 1.0),
        "cache_v": nrm(ks[3], (n_pool, PAGE_SIZE, N_HEADS, V_DIM), 1.0),
        "page_table": perm.reshape(DEC_BATCH, n_pages).astype(jnp.int32),
        "norm_a": gain(ks[5], (N_A_LAYERS, D_MODEL)),
        "w_in_a": nrm(ks[6], (N_A_LAYERS, D_MODEL, 2 * D_GATE), D_MODEL ** -0.5),
        "w_s_a": nrm(ks[7], (N_A_LAYERS, N_GROUPS_A, CHUNK, CHUNK), CHUNK ** -0.5),
        "b_s_a": 1.0 + 0.1 * jax.random.normal(ks[8], (N_A_LAYERS, N_GROUPS_A, CHUNK), f),
        "w_out_a": nrm(ks[9], (N_A_LAYERS, D_GATE, D_MODEL), D_GATE ** -0.5),
        "norm_kv": gain(ks[10], (D_MODEL,)),
        "w_k": nrm(ks[11], (D_MODEL, N_QK * HEAD_DIM), D_MODEL ** -0.5),
        "w_v": nrm(ks[12], (D_MODEL, N_HEADS * V_DIM), D_MODEL ** -0.5),
        "norm_b": gain(ks[13], (N_B_LAYERS, D_MODEL)),
        "w_q": nrm(ks[14], (N_B_LAYERS, D_MODEL, N_QK * HEAD_DIM), D_MODEL ** -0.5),
        "lambda_q1": nrm(ks[15], (N_B_LAYERS, HEAD_DIM), 0.1),
        "lambda_k1": nrm(ks[16], (N_B_LAYERS, HEAD_DIM), 0.1),
        "lambda_q2": nrm(ks[17], (N_B_LAYERS, HEAD_DIM), 0.1),
        "lambda_k2": nrm(ks[18], (N_B_LAYERS, HEAD_DIM), 0.1),
        "subln_b": gain(ks[19], (N_B_LAYERS, V_DIM)),
        "w_o_b": nrm(ks[20], (N_B_LAYERS, N_HEADS * V_DIM, D_MODEL), (N_HEADS * V_DIM) ** -0.5),
        "norm_ffn": gain(ks[21], (DEPTH, D_MODEL)),
        "w_up": nrm(ks[22], (DEPTH, D_MODEL, D_FF), D_MODEL ** -0.5),
        "w_down": nrm(ks[23], (DEPTH, D_FF, D_MODEL), D_FF ** -0.5),
        "norm_f": gain(ks[24], (D_MODEL,)),
    }


def reference(x_prompt, x_sample, cache_k, cache_v, page_table, norm_a, w_in_a, w_s_a, b_s_a,
              w_out_a, norm_kv, w_k, w_v, norm_b, w_q, lambda_q1, lambda_k1, lambda_q2, lambda_k2,
              subln_b, w_o_b, norm_ffn, w_up, w_down, norm_f):
    params = dict(norm_a=norm_a, w_in_a=w_in_a, w_s_a=w_s_a, b_s_a=b_s_a, w_out_a=w_out_a,
                  norm_kv=norm_kv, w_k=w_k, w_v=w_v, norm_b=norm_b, w_q=w_q,
                  lambda_q1=lambda_q1, lambda_k1=lambda_k1, lambda_q2=lambda_q2,
                  lambda_k2=lambda_k2, subln_b=subln_b, w_o_b=w_o_b, norm_ffn=norm_ffn,
                  w_up=w_up, w_down=w_down, norm_f=norm_f)
    pos_prompt = jnp.arange(x_prompt.shape[1], dtype=jnp.int32)
    y_prompt, k_prompt, v_prompt, _ = trunk(x_prompt, pos_prompt, None, None, params)
    k_pages = cache_k[page_table]
    v_pages = cache_v[page_table]
    pos_sample = PAST_LEN + jnp.arange(x_sample.shape[1], dtype=jnp.int32)
    y_sample, k_sample, v_sample, chunk_v_sample = trunk(x_sample, pos_sample, k_pages, v_pages, params)
    return (y_prompt, y_sample, k_prompt, v_prompt, k_sample, v_sample, chunk_v_sample)
```

```python
import functools
import math

import jax
import jax.numpy as jnp
from jax import lax
from jax.experimental import pallas as pl
from jax.experimental.pallas import tpu as pltpu

D_MODEL = 2048
DEPTH = 4
PAST_LEN = 16384
PAGE_SIZE = 128
N_A_LAYERS = DEPTH // 2
CHUNK = 128
D_GATE = 2 * D_MODEL
N_GROUPS_A = 16
GROUP_DIM_A = D_GATE // N_GROUPS_A
HEAD_DIM = 128
N_HEADS = D_MODEL // (2 * HEAD_DIM)
N_QK = 2 * N_HEADS
V_DIM = 2 * HEAD_DIM
D_FF = 4 * D_MODEL
ROPE_THETA = 10000.0
EPS = 1e-5
NEG_INF = -1e30

VMEM_LIMIT = 56 * 1024 * 1024

F32 = jnp.float32
BF16 = jnp.bfloat16


def _params(n_axes):
    return pltpu.CompilerParams(dimension_semantics=("arbitrary",) * n_axes,
                                vmem_limit_bytes=VMEM_LIMIT)


def _rms_rows(x, g):
    return x * lax.rsqrt(jnp.mean(x * x, axis=-1, keepdims=True) + EPS) * g


def _lambda(lam_ref, lam_init):
    a = jnp.sum(lam_ref[0:1, :] * lam_ref[1:2, :], axis=-1, keepdims=True)
    b = jnp.sum(lam_ref[2:3, :] * lam_ref[3:4, :], axis=-1, keepdims=True)
    return jnp.exp(a) - jnp.exp(b) + lam_init


def _norm_matmul_kernel(*refs, epilogue, scale):
    if epilogue == "rope":
        x_ref, g_ref, w_ref, cos_ref, sin_ref, o_ref, xn_ref = refs
    else:
        x_ref, g_ref, w_ref, o_ref, xn_ref = refs

    @pl.when(pl.program_id(1) == 0)
    def _():
        xn_ref[...] = _rms_rows(x_ref[...], g_ref[...]).astype(BF16)

    y = jnp.dot(xn_ref[...], w_ref[...], preferred_element_type=F32)
    if epilogue == "gelu":
        o_ref[...] = (0.5 * y * (1.0 + lax.erf(y * (2.0 ** -0.5)))).astype(o_ref.dtype)
    elif epilogue == "rope":
        cos = cos_ref[...]
        sin = sin_ref[...]
        for h in range(y.shape[1] // HEAD_DIM):
            sl = slice(h * HEAD_DIM, (h + 1) * HEAD_DIM)
            yh = y[:, sl]
            oh = yh * cos + pltpu.roll(yh, HEAD_DIM // 2, 1) * sin
            if scale != 1.0:
                oh = oh * scale
            o_ref[:, sl] = oh.astype(o_ref.dtype)
    else:
        o_ref[...] = y.astype(o_ref.dtype)


def _norm_matmul(x, g, w, l, *, tm, tn, out_dtype, epilogue=None, rope=None, scale=1.0):
    M, K = x.shape
    N = w.shape[2]
    in_specs = [
        pl.BlockSpec((tm, K), lambda i, j: (i, 0)),
        pl.BlockSpec((None, 1, K), lambda i, j: (l, 0, 0)),
        pl.BlockSpec((None, K, tn), lambda i, j: (l, 0, j)),
    ]
    args = [x, g, w]
    if epilogue == "rope":
        cos, sin = rope
        nb = cos.shape[0] // tm
        in_specs += [pl.BlockSpec((tm, HEAD_DIM), lambda i, j: (i % nb, 0))] * 2
        args += [cos, sin]
    return pl.pallas_call(
        functools.partial(_norm_matmul_kernel, epilogue=epilogue, scale=scale),
        grid=(M // tm, N // tn),
        in_specs=in_specs,
        out_specs=pl.BlockSpec((tm, tn), lambda i, j: (i, j)),
        out_shape=jax.ShapeDtypeStruct((M, N), out_dtype),
        scratch_shapes=[pltpu.VMEM((tm, K), BF16)],
        compiler_params=_params(2),
        name="norm_matmul_" + (epilogue or "plain"),
    )(*args)


def _gmlp_out_kernel(u_ref, v_ref, ws_ref, bs_ref, h_ref, w_ref, o_ref, gated_ref, wt_ref, *, rows, chunk):
    tm = u_ref.shape[0]

    @pl.when(pl.program_id(1) == 0)
    def _():
        r = lax.broadcasted_iota(jnp.int32, (rows, rows), 0)
        c = lax.broadcasted_iota(jnp.int32, (rows, rows), 1)
        mask = (r // chunk == c // chunk) & (r >= c)
        for g in range(N_GROUPS_A):
            wt_ref[g] = jnp.where(mask, ws_ref[g], 0.0).astype(BF16)

        def mix(ci, carry):
            r0 = pl.multiple_of(ci * rows, rows)
            for g in range(N_GROUPS_A):
                sl = slice(g * GROUP_DIM_A, (g + 1) * GROUP_DIM_A)
                vg = v_ref[pl.ds(r0, rows), sl].astype(BF16)
                s = jnp.dot(wt_ref[g], vg, preferred_element_type=F32) + bs_ref[:, g:g + 1]
                ug = u_ref[pl.ds(r0, rows), sl].astype(F32)
                gated_ref[pl.ds(r0, rows), sl] = (ug * s).astype(BF16)
            return carry

        lax.fori_loop(0, tm // rows, mix, 0)

    o_ref[...] = h_ref[...] + jnp.dot(gated_ref[...], w_ref[...], preferred_element_type=F32)


def _gmlp_out(z, ws, bs_t, h, w_out, l, *, tm, tn, rows, chunk):
    M = z.shape[0]
    N = w_out.shape[2]
    return pl.pallas_call(
        functools.partial(_gmlp_out_kernel, rows=rows, chunk=chunk),
        grid=(M // tm, N // tn),
        in_specs=[
            pl.BlockSpec((tm, D_GATE), lambda i, j: (i, 0)),
            pl.BlockSpec((tm, D_GATE), lambda i, j: (i, 1)),
            pl.BlockSpec((N_GROUPS_A, rows, rows), lambda i, j: (0, 0, 0)),
            pl.BlockSpec((rows, N_GROUPS_A), lambda i, j: (0, 0)),
            pl.BlockSpec((tm, tn), lambda i, j: (i, j)),
            pl.BlockSpec((None, D_GATE, tn), lambda i, j: (l, 0, j)),
        ],
        out_specs=pl.BlockSpec((tm, tn), lambda i, j: (i, j)),
        out_shape=jax.ShapeDtypeStruct((M, N), F32),
        scratch_shapes=[pltpu.VMEM((tm, D_GATE), BF16),
                        pltpu.VMEM((N_GROUPS_A, rows, rows), BF16)],
        compiler_params=_params(2),
        name="gmlp_out",
    )(z, z, ws, bs_t, h, w_out)


def _ffn_kernel(*refs, final_norm):
    if final_norm:
        x_ref, g_ref, wu_ref, wd_ref, gf_ref, o_ref, xn_ref = refs
    else:
        x_ref, g_ref, wu_ref, wd_ref, o_ref, xn_ref = refs
    f = pl.program_id(1)

    @pl.when(f == 0)
    def _():
        x = x_ref[...]
        xn_ref[...] = _rms_rows(x, g_ref[...]).astype(BF16)
        o_ref[...] = x

    a = jnp.dot(xn_ref[...], wu_ref[...], preferred_element_type=F32)
    a = jnp.square(jnp.maximum(a, 0.0)).astype(BF16)
    o_ref[...] += jnp.dot(a, wd_ref[...], preferred_element_type=F32)

    if final_norm:
        @pl.when(f == pl.num_programs(1) - 1)
        def _():
            o_ref[...] = _rms_rows(o_ref[...], gf_ref[...])


def _ffn(x, g, w_up, w_down, l, g_final, *, tm, tf):
    M, K = x.shape
    F = w_up.shape[2]
    final_norm = g_final is not None
    in_specs = [
        pl.BlockSpec((tm, K), lambda i, f: (i, 0)),
        pl.BlockSpec((None, 1, K), lambda i, f: (l, 0, 0)),
        pl.BlockSpec((None, K, tf), lambda i, f: (l, 0, f)),
        pl.BlockSpec((None, tf, K), lambda i, f: (l, f, 0)),
    ]
    args = [x, g, w_up, w_down]
    if final_norm:
        in_specs.append(pl.BlockSpec((1, K), lambda i, f: (0, 0)))
        args.append(g_final)
    return pl.pallas_call(
        functools.partial(_ffn_kernel, final_norm=final_norm),
        grid=(M // tm, F // tf),
        in_specs=in_specs,
        out_specs=pl.BlockSpec((tm, K), lambda i, f: (i, 0)),
        out_shape=jax.ShapeDtypeStruct((M, K), F32),
        scratch_shapes=[pltpu.VMEM((tm, K), BF16)],
        compiler_params=_params(2),
        name="ffn",
    )(*args)


def _matmul_res_kernel(x_ref, w_ref, h_ref, o_ref):
    o_ref[...] = h_ref[...] + jnp.dot(x_ref[...], w_ref[...], preferred_element_type=F32)


def _matmul_res(x, w, l, h, *, tm, tn):
    M, K = x.shape
    N = w.shape[2]
    return pl.pallas_call(
        _matmul_res_kernel,
        grid=(M // tm, N // tn),
        in_specs=[
            pl.BlockSpec((tm, K), lambda i, j: (i, 0)),
            pl.BlockSpec((None, K, tn), lambda i, j: (l, 0, j)),
            pl.BlockSpec((tm, tn), lambda i, j: (i, j)),
        ],
        out_specs=pl.BlockSpec((tm, tn), lambda i, j: (i, j)),
        out_shape=jax.ShapeDtypeStruct((M, N), F32),
        compiler_params=_params(2),
        name="matmul_res",
    )(x, w, h)


def _head_out(o1, o2, lam, subln, lam_init):
    d = o1 - lam * o2
    return _rms_rows(d, subln) * (1.0 - lam_init)


def _attn_prompt_kernel(qi_tab, kj_tab, q_ref, k_ref, v_ref, lam_ref, subln_ref, o_ref,
                        m_ref, l_ref, acc_ref, *, lam_init):
    step = pl.program_id(2)
    qi = qi_tab[step]
    kj = kj_tab[step]
    tq = q_ref.shape[1]
    tk = k_ref.shape[1]

    @pl.when(kj == 0)
    def _():
        m_ref[...] = jnp.full(m_ref.shape, NEG_INF, F32)
        l_ref[...] = jnp.zeros(l_ref.shape, F32)
        acc_ref[...] = jnp.zeros(acc_ref.shape, F32)

    def update(masked):
        v = v_ref[0].astype(BF16)
        for sub in range(2):
            sl = slice(sub * HEAD_DIM, (sub + 1) * HEAD_DIM)
            q = q_ref[0, :, sl]
            k = k_ref[0, :, sl].astype(BF16)
            s = lax.dot_general(q, k, (((1,), (1,)), ((), ())), preferred_element_type=F32)
            if masked:
                r = lax.broadcasted_iota(jnp.int32, (tq, tk), 0)
                c = lax.broadcasted_iota(jnp.int32, (tq, tk), 1)
                s = jnp.where(c <= r, s, NEG_INF)
            m_prev = m_ref[sub]
            m_new = jnp.maximum(m_prev, jnp.max(s, axis=-1, keepdims=True))
            alpha = jnp.exp(m_prev - m_new)
            p = jnp.exp(s - m_new)
            l_ref[sub] = alpha * l_ref[sub] + jnp.sum(p, axis=-1, keepdims=True)
            m_ref[sub] = m_new
            acc_ref[sub] = alpha * acc_ref[sub] + jnp.dot(p.astype(BF16), v, preferred_element_type=F32)

    @pl.when(kj < qi)
    def _():
        update(False)

    @pl.when(kj == qi)
    def _():
        update(True)
        lam = _lambda(lam_ref, lam_init)
        o1 = acc_ref[0] * (1.0 / l_ref[0])
        o2 = acc_ref[1] * (1.0 / l_ref[1])
        o_ref[0] = _head_out(o1, o2, lam, subln_ref[...], lam_init).astype(o_ref.dtype)


def _attn_prompt(q, k, v, lam_vecs, subln, l, lam_init, *, tq):
    B, T, _ = q.shape
    nq = T // tq
    pairs = [(i, j) for i in range(nq) for j in range(i + 1)]
    qi_tab = jnp.asarray([p[0] for p in pairs], jnp.int32)
    kj_tab = jnp.asarray([p[1] for p in pairs], jnp.int32)
    grid_spec = pltpu.PrefetchScalarGridSpec(
        num_scalar_prefetch=2,
        grid=(B, N_HEADS, len(pairs)),
        in_specs=[
            pl.BlockSpec((1, tq, V_DIM), lambda b, h, s, qt, kt: (b, qt[s], h)),
            pl.BlockSpec((1, tq, V_DIM), lambda b, h, s, qt, kt: (b, kt[s], h)),
            pl.BlockSpec((1, tq, V_DIM), lambda b, h, s, qt, kt: (b, kt[s], h)),
            pl.BlockSpec((None, 4, HEAD_DIM), lambda b, h, s, qt, kt: (l, 0, 0)),
            pl.BlockSpec((None, 1, V_DIM), lambda b, h, s, qt, kt: (l, 0, 0)),
        ],
        out_specs=pl.BlockSpec((1, tq, V_DIM), lambda b, h, s, qt, kt: (b, qt[s], h)),
        scratch_shapes=[pltpu.VMEM((2, tq, 1), F32), pltpu.VMEM((2, tq, 1), F32),
                        pltpu.VMEM((2, tq, V_DIM), F32)],
    )
    return pl.pallas_call(
        functools.partial(_attn_prompt_kernel, lam_init=lam_init),
        grid_spec=grid_spec,
        out_shape=jax.ShapeDtypeStruct((B, T, N_HEADS * V_DIM), BF16),
        compiler_params=_params(3),
        name="attn_prompt",
    )(qi_tab, kj_tab, q, k, v, lam_vecs, subln)


def _attn_sample_kernel(pt_ref, q_ref, *refs, pages_per_step, lam_init):
    k_refs = refs[:pages_per_step]
    v_refs = refs[pages_per_step:2 * pages_per_step]
    kn_ref, vn_ref, lam_ref, subln_ref, o_ref, qbd_ref, m_ref, l_ref, acc_ref = refs[2 * pages_per_step:]
    p_idx = pl.program_id(1)
    tq = q_ref.shape[1]
    rows = N_QK * tq
    d_all = N_QK * HEAD_DIM

    @pl.when(p_idx == 0)
    def _():
        m_ref[...] = jnp.full(m_ref.shape, NEG_INF, F32)
        l_ref[...] = jnp.zeros(l_ref.shape, F32)
        acc_ref[...] = jnp.zeros(acc_ref.shape, F32)
        qt = jnp.concatenate([q_ref[0]] * N_QK, axis=0)
        r = lax.broadcasted_iota(jnp.int32, (rows, d_all), 0)
        c = lax.broadcasted_iota(jnp.int32, (rows, d_all), 1)
        qbd_ref[...] = jnp.where(r // tq == c // HEAD_DIM, qt, 0.0).astype(BF16)

    def update(k_list, v_list, mask):
        qbd = qbd_ref[...]
        s_list = []
        for kp in k_list:
            s = lax.dot_general(qbd, kp, (((1,), (1,)), ((), ())), preferred_element_type=F32)
            if mask is not None:
                s = jnp.where(mask, s, NEG_INF)
            s_list.append(s)
        m_prev = m_ref[...]
        m_new = m_prev
        for s in s_list:
            m_new = jnp.maximum(m_new, jnp.max(s, axis=-1, keepdims=True))
        alpha = jnp.exp(m_prev - m_new)
        l_new = alpha * l_ref[...]
        p_list = []
        for s in s_list:
            p = jnp.exp(s - m_new)
            l_new = l_new + jnp.sum(p, axis=-1, keepdims=True)
            p_list.append(p.astype(BF16))
        m_ref[...] = m_new
        l_ref[...] = l_new
        for hv in range(N_HEADS):
            rs = slice(hv * 2 * tq, (hv + 1) * 2 * tq)
            pv = None
            for p, v_heads in zip(p_list, v_list):
                t = jnp.dot(p[rs, :], v_heads[hv], preferred_element_type=F32)
                pv = t if pv is None else pv + t
            acc_ref[rs, :] = alpha[rs, :] * acc_ref[rs, :] + pv

    update([jnp.concatenate([k[:, h, :].astype(BF16) for h in range(N_QK)], axis=1) for k in k_refs],
           [[v[:, hv, :].astype(BF16) for hv in range(N_HEADS)] for v in v_refs], None)

    @pl.when(p_idx == pl.num_programs(1) - 1)
    def _():
        pad = jnp.zeros((PAGE_SIZE - tq, d_all), F32)
        kn = jnp.concatenate([kn_ref[0], pad], axis=0).astype(BF16)
        vn = jnp.concatenate([vn_ref[0], pad], axis=0).astype(BF16)
        r = lax.broadcasted_iota(jnp.int32, (rows, PAGE_SIZE), 0)
        c = lax.broadcasted_iota(jnp.int32, (rows, PAGE_SIZE), 1)
        update([kn], [[vn[:, hv * V_DIM:(hv + 1) * V_DIM] for hv in range(N_HEADS)]], c <= r % tq)
        lam = _lambda(lam_ref, lam_init)
        o = acc_ref[...] * (1.0 / l_ref[...])
        for hv in range(N_HEADS):
            o1 = o[hv * 2 * tq:hv * 2 * tq + tq, :]
            o2 = o[hv * 2 * tq + tq:(hv + 1) * 2 * tq, :]
            o_ref[0, :, hv * V_DIM:(hv + 1) * V_DIM] = _head_out(
                o1, o2, lam, subln_ref[...], lam_init).astype(o_ref.dtype)


def _attn_sample(q, cache_k, cache_v, page_table, k_new, v_new, lam_vecs, subln, l, lam_init, *,
                 pages_per_step):
    Bd, Tq, D = q.shape
    n_pages = page_table.shape[1]
    pps = pages_per_step

    def page_map(i):
        return lambda b, p, pt: (pt[b * n_pages + p * pps + i], 0, 0, 0)

    k_specs = [pl.BlockSpec((None, PAGE_SIZE, N_QK, HEAD_DIM), page_map(i)) for i in range(pps)]
    v_specs = [pl.BlockSpec((None, PAGE_SIZE, N_HEADS, V_DIM), page_map(i)) for i in range(pps)]
    row_spec = pl.BlockSpec((1, Tq, D), lambda b, p, pt: (b, 0, 0))
    grid_spec = pltpu.PrefetchScalarGridSpec(
        num_scalar_prefetch=1,
        grid=(Bd, n_pages // pps),
        in_specs=[row_spec] + k_specs + v_specs + [
            row_spec, row_spec,
            pl.BlockSpec((None, 4, HEAD_DIM), lambda b, p, pt: (l, 0, 0)),
            pl.BlockSpec((None, 1, V_DIM), lambda b, p, pt: (l, 0, 0)),
        ],
        out_specs=row_spec,
        scratch_shapes=[pltpu.VMEM((N_QK * Tq, D), BF16),
                        pltpu.VMEM((N_QK * Tq, 1), F32), pltpu.VMEM((N_QK * Tq, 1), F32),
                        pltpu.VMEM((N_QK * Tq, V_DIM), F32)],
    )
    return pl.pallas_call(
        functools.partial(_attn_sample_kernel, pages_per_step=pps, lam_init=lam_init),
        grid_spec=grid_spec,
        out_shape=jax.ShapeDtypeStruct((Bd, Tq, D), BF16),
        compiler_params=_params(2),
        name="attn_sample",
    )(page_table.reshape(-1), q, *([cache_k] * pps), *([cache_v] * pps), k_new, v_new, lam_vecs, subln)


def _rope_tables(pos):
    half = HEAD_DIM // 2
    inv = ROPE_THETA ** (-jnp.arange(half, dtype=F32) / half)
    ang = pos.astype(F32)[:, None] * inv[None, :]
    cos = jnp.cos(ang)
    sin = jnp.sin(ang)
    return jnp.concatenate([cos, cos], axis=-1), jnp.concatenate([-sin, sin], axis=-1)


def _trunk(x, pos, paged, p, cfg):
    B, T, _ = x.shape
    M = B * T
    h = x.reshape(M, D_MODEL)
    tm, rows, chunk = cfg["tm"], cfg["rows"], cfg["chunk"]
    cos, sin = _rope_tables(pos)
    if T < tm:
        cos = jnp.tile(cos, (tm // T, 1))
        sin = jnp.tile(sin, (tm // T, 1))
    v_rows = []
    k_sh = v_sh = None
    for l in range(DEPTH):
        if l < N_A_LAYERS:
            z = _norm_matmul(h, p["norm_a"], p["w_in_a"], l, tm=tm, tn=cfg["tn_in"],
                             out_dtype=cfg["z_dtype"], epilogue="gelu")
            ws = p["w_s_a"][l][:, :chunk, :chunk]
            bs_t = p["b_s_a"][l][:, :chunk].T
            if rows > chunk:
                ws = jnp.tile(ws, (1, rows // chunk, rows // chunk))
                bs_t = jnp.tile(bs_t, (rows // chunk, 1))
            h = _gmlp_out(z, ws, bs_t, h, p["w_out_a"], l, tm=cfg["tm_gmlp"], tn=cfg["tn"],
                          rows=rows, chunk=chunk)
            v_rows.append(z[:, D_GATE:])
        else:
            if l == N_A_LAYERS:
                k_sh = _norm_matmul(h, p["norm_kv"], p["w_k"], 0, tm=tm, tn=cfg["tn"], out_dtype=F32,
                                    epilogue="rope", rope=(cos, sin))
                v_sh = _norm_matmul(h, p["norm_kv"], p["w_v"], 0, tm=tm, tn=cfg["tn"], out_dtype=F32)
            j = l - N_A_LAYERS
            lam_init = 0.8 - 0.6 * math.exp(-0.3 * l)
            q = _norm_matmul(h, p["norm_b"], p["w_q"], j, tm=tm, tn=cfg["tn"],
                             out_dtype=cfg["q_dtype"], epilogue="rope", rope=(cos, sin),
                             scale=HEAD_DIM ** -0.5)
            if paged is None:
                o = _attn_prompt(q.reshape(B, T, -1), k_sh.reshape(B, T, -1), v_sh.reshape(B, T, -1),
                                 p["lam_vecs"], p["subln_b"], j, lam_init, tq=cfg["tq"])
            else:
                cache_k, cache_v, page_table = paged
                o = _attn_sample(q.reshape(B, T, -1), cache_k, cache_v, page_table,
                                 k_sh.reshape(B, T, -1), v_sh.reshape(B, T, -1),
                                 p["lam_vecs"], p["subln_b"], j, lam_init, pages_per_step=cfg["pps"])
            h = _matmul_res(o.reshape(M, -1), p["w_o_b"], j, h, tm=tm, tn=cfg["tn"])
        h = _ffn(h, p["norm_ffn"], p["w_up"], p["w_down"], l,
                 p["norm_f"] if l == DEPTH - 1 else None, tm=cfg["tm_ffn"], tf=cfg["tf"])
    return h, k_sh, v_sh, v_rows


def kernel(x_prompt, x_sample, cache_k, cache_v, page_table, norm_a, w_in_a, w_s_a, b_s_a, w_out_a, norm_kv, w_k, w_v, norm_b, w_q, lambda_q1, lambda_k1, lambda_q2, lambda_k2, subln_b, w_o_b, norm_ffn, w_up, w_down, norm_f):
    def gains(g):
        return g.reshape(g.shape[0], 1, g.shape[1])

    p = dict(norm_a=gains(norm_a), w_in_a=w_in_a.astype(BF16), w_s_a=w_s_a, b_s_a=b_s_a,
             w_out_a=w_out_a.astype(BF16), norm_kv=norm_kv.reshape(1, 1, -1),
             w_k=w_k.astype(BF16)[None], w_v=w_v.astype(BF16)[None], norm_b=gains(norm_b),
             w_q=w_q.astype(BF16),
             lam_vecs=jnp.stack([lambda_q1, lambda_k1, lambda_q2, lambda_k2], axis=1),
             subln_b=gains(subln_b), w_o_b=w_o_b.astype(BF16), norm_ffn=gains(norm_ffn),
             w_up=w_up.astype(BF16), w_down=w_down.astype(BF16), norm_f=norm_f.reshape(1, -1))

    B, T, _ = x_prompt.shape
    Bd, Td, _ = x_sample.shape

    cfg_p = dict(tm=1024, tn=512, tn_in=512, tm_gmlp=512, tm_ffn=512, tf=512, tq=512,
                 rows=CHUNK, chunk=CHUNK, z_dtype=BF16, q_dtype=BF16)
    y_p, k_p, v_p, _ = _trunk(x_prompt, jnp.arange(T, dtype=jnp.int32), None, p, cfg_p)

    Ms = Bd * Td
    cfg_s = dict(tm=Ms, tn=1024, tn_in=1024, tm_gmlp=Ms, tm_ffn=Ms, tf=1024, pps=4,
                 rows=Ms, chunk=Td, z_dtype=F32, q_dtype=F32)
    y_s, k_s, v_s, vr_s = _trunk(x_sample, PAST_LEN + jnp.arange(Td, dtype=jnp.int32),
                                 (cache_k, cache_v, page_table), p, cfg_s)

    return (y_p.reshape(B, T, D_MODEL), y_s.reshape(Bd, Td, D_MODEL),
            k_p.reshape(B, T, N_QK, HEAD_DIM), v_p.reshape(B, T, N_HEADS, V_DIM),
            k_s.reshape(Bd, Td, N_QK, HEAD_DIM), v_s.reshape(Bd, Td, N_HEADS, V_DIM),
            jnp.stack(vr_s).reshape(N_A_LAYERS, Bd, Td, D_GATE))
```

```python
import functools
import math

import jax
import jax.numpy as jnp
from jax import lax
from jax.experimental import pallas as pl
from jax.experimental.pallas import tpu as pltpu

D_MODEL = 2048
DEPTH = 4
PAST_LEN = 16384
PAGE_SIZE = 128
N_A_LAYERS = DEPTH // 2
CHUNK = 128
D_GATE = 2 * D_MODEL
N_GROUPS_A = 16
GROUP_DIM_A = D_GATE // N_GROUPS_A
HEAD_DIM = 128
N_HEADS = D_MODEL // (2 * HEAD_DIM)
N_QK = 2 * N_HEADS
V_DIM = 2 * HEAD_DIM
D_FF = 4 * D_MODEL
ROPE_THETA = 10000.0
EPS = 1e-5
NEG_INF = -1e30

LANES = 128
VMEM_LIMIT = 56 * 1024 * 1024
LOG2E = math.log2(math.e)

F32 = jnp.float32
BF16 = jnp.bfloat16


def _params(n_axes):
    return pltpu.CompilerParams(dimension_semantics=("arbitrary",) * n_axes,
                                vmem_limit_bytes=VMEM_LIMIT)


def _rms_rows(x, g):
    return x * lax.rsqrt(jnp.mean(x * x, axis=-1, keepdims=True) + EPS) * g


def _lambda(lam_ref, lam_init):
    a = jnp.sum(lam_ref[0:1, :] * lam_ref[1:2, :], axis=-1, keepdims=True)
    b = jnp.sum(lam_ref[2:3, :] * lam_ref[3:4, :], axis=-1, keepdims=True)
    return jnp.exp(a) - jnp.exp(b) + lam_init


def _norm_matmul_kernel(*refs, epilogue, scale):
    if epilogue == "rope":
        x_ref, g_ref, w_ref, cos_ref, sin_ref, o_ref, xn_ref = refs
    else:
        x_ref, g_ref, w_ref, o_ref, xn_ref = refs

    @pl.when(pl.program_id(1) == 0)
    def _():
        xn_ref[...] = _rms_rows(x_ref[...], g_ref[...]).astype(BF16)

    y = jnp.dot(xn_ref[...], w_ref[...], preferred_element_type=F32)
    if epilogue == "gelu":
        o_ref[...] = (0.5 * y * (1.0 + lax.erf(y * (2.0 ** -0.5)))).astype(o_ref.dtype)
    elif epilogue == "rope":
        cos = cos_ref[...]
        sin = sin_ref[...]
        for h in range(y.shape[1] // HEAD_DIM):
            sl = slice(h * HEAD_DIM, (h + 1) * HEAD_DIM)
            yh = y[:, sl]
            oh = yh * cos + pltpu.roll(yh, HEAD_DIM // 2, 1) * sin
            if scale != 1.0:
                oh = oh * scale
            o_ref[:, sl] = oh.astype(o_ref.dtype)
    else:
        o_ref[...] = y.astype(o_ref.dtype)


def _norm_matmul(x, g, w, l, *, tm, tn, out_dtype, epilogue=None, rope=None, scale=1.0):
    M, K = x.shape
    N = w.shape[2]
    in_specs = [
        pl.BlockSpec((tm, K), lambda i, j: (i, 0)),
        pl.BlockSpec((None, 1, K), lambda i, j: (l, 0, 0)),
        pl.BlockSpec((None, K, tn), lambda i, j: (l, 0, j)),
    ]
    args = [x, g, w]
    if epilogue == "rope":
        cos, sin = rope
        nb = cos.shape[0] // tm
        in_specs += [pl.BlockSpec((tm, HEAD_DIM), lambda i, j: (i % nb, 0))] * 2
        args += [cos, sin]
    return pl.pallas_call(
        functools.partial(_norm_matmul_kernel, epilogue=epilogue, scale=scale),
        grid=(M // tm, N // tn),
        in_specs=in_specs,
        out_specs=pl.BlockSpec((tm, tn), lambda i, j: (i, j)),
        out_shape=jax.ShapeDtypeStruct((M, N), out_dtype),
        scratch_shapes=[pltpu.VMEM((tm, K), BF16)],
        compiler_params=_params(2),
        name="norm_matmul_" + (epilogue or "plain"),
    )(*args)


def _gmlp_out_kernel(u_ref, v_ref, ws_ref, bs_ref, h_ref, w_ref, o_ref, gated_ref, wt_ref, *, rows, chunk):
    tm = u_ref.shape[0]

    @pl.when(pl.program_id(1) == 0)
    def _():
        r = lax.broadcasted_iota(jnp.int32, (rows, rows), 0)
        c = lax.broadcasted_iota(jnp.int32, (rows, rows), 1)
        mask = (r // chunk == c // chunk) & (r >= c)
        for g in range(N_GROUPS_A):
            wt_ref[g] = jnp.where(mask, ws_ref[g], 0.0).astype(BF16)

        def mix(ci, carry):
            r0 = pl.multiple_of(ci * rows, rows)
            for g in range(N_GROUPS_A):
                sl = slice(g * GROUP_DIM_A, (g + 1) * GROUP_DIM_A)
                vg = v_ref[pl.ds(r0, rows), sl].astype(BF16)
                s = jnp.dot(wt_ref[g], vg, preferred_element_type=F32) + bs_ref[:, g:g + 1]
                ug = u_ref[pl.ds(r0, rows), sl].astype(F32)
                gated_ref[pl.ds(r0, rows), sl] = (ug * s).astype(BF16)
            return carry

        lax.fori_loop(0, tm // rows, mix, 0)

    o_ref[...] = h_ref[...] + jnp.dot(gated_ref[...], w_ref[...], preferred_element_type=F32)


def _gmlp_out(z, ws, bs_t, h, w_out, l, *, tm, tn, rows, chunk):
    M = z.shape[0]
    N = w_out.shape[2]
    return pl.pallas_call(
        functools.partial(_gmlp_out_kernel, rows=rows, chunk=chunk),
        grid=(M // tm, N // tn),
        in_specs=[
            pl.BlockSpec((tm, D_GATE), lambda i, j: (i, 0)),
            pl.BlockSpec((tm, D_GATE), lambda i, j: (i, 1)),
            pl.BlockSpec((N_GROUPS_A, rows, rows), lambda i, j: (0, 0, 0)),
            pl.BlockSpec((rows, N_GROUPS_A), lambda i, j: (0, 0)),
            pl.BlockSpec((tm, tn), lambda i, j: (i, j)),
            pl.BlockSpec((None, D_GATE, tn), lambda i, j: (l, 0, j)),
        ],
        out_specs=pl.BlockSpec((tm, tn), lambda i, j: (i, j)),
        out_shape=jax.ShapeDtypeStruct((M, N), F32),
        scratch_shapes=[pltpu.VMEM((tm, D_GATE), BF16),
                        pltpu.VMEM((N_GROUPS_A, rows, rows), BF16)],
        compiler_params=_params(2),
        name="gmlp_out",
    )(z, z, ws, bs_t, h, w_out)


def _ffn_kernel(*refs, final_norm):
    if final_norm:
        x_ref, g_ref, wu_ref, wd_ref, gf_ref, o_ref, xn_ref = refs
    else:
        x_ref, g_ref, wu_ref, wd_ref, o_ref, xn_ref = refs
    f = pl.program_id(1)

    @pl.when(f == 0)
    def _():
        x = x_ref[...]
        xn_ref[...] = _rms_rows(x, g_ref[...]).astype(BF16)
        o_ref[...] = x

    a = jnp.dot(xn_ref[...], wu_ref[...], preferred_element_type=F32)
    a = jnp.square(jnp.maximum(a, 0.0)).astype(BF16)
    o_ref[...] += jnp.dot(a, wd_ref[...], preferred_element_type=F32)

    if final_norm:
        @pl.when(f == pl.num_programs(1) - 1)
        def _():
            o_ref[...] = _rms_rows(o_ref[...], gf_ref[...])


def _ffn(x, g, w_up, w_down, l, g_final, *, tm, tf):
    M, K = x.shape
    F = w_up.shape[2]
    final_norm = g_final is not None
    in_specs = [
        pl.BlockSpec((tm, K), lambda i, f: (i, 0)),
        pl.BlockSpec((None, 1, K), lambda i, f: (l, 0, 0)),
        pl.BlockSpec((None, K, tf), lambda i, f: (l, 0, f)),
        pl.BlockSpec((None, tf, K), lambda i, f: (l, f, 0)),
    ]
    args = [x, g, w_up, w_down]
    if final_norm:
        in_specs.append(pl.BlockSpec((1, K), lambda i, f: (0, 0)))
        args.append(g_final)
    return pl.pallas_call(
        functools.partial(_ffn_kernel, final_norm=final_norm),
        grid=(M // tm, F // tf),
        in_specs=in_specs,
        out_specs=pl.BlockSpec((tm, K), lambda i, f: (i, 0)),
        out_shape=jax.ShapeDtypeStruct((M, K), F32),
        scratch_shapes=[pltpu.VMEM((tm, K), BF16)],
        compiler_params=_params(2),
        name="ffn",
    )(*args)


def _matmul_res_kernel(x_ref, w_ref, h_ref, o_ref):
    o_ref[...] = h_ref[...] + jnp.dot(x_ref[...], w_ref[...], preferred_element_type=F32)


def _matmul_res(x, w, l, h, *, tm, tn):
    M, K = x.shape
    N = w.shape[2]
    return pl.pallas_call(
        _matmul_res_kernel,
        grid=(M // tm, N // tn),
        in_specs=[
            pl.BlockSpec((tm, K), lambda i, j: (i, 0)),
            pl.BlockSpec((None, K, tn), lambda i, j: (l, 0, j)),
            pl.BlockSpec((tm, tn), lambda i, j: (i, j)),
        ],
        out_specs=pl.BlockSpec((tm, tn), lambda i, j: (i, j)),
        out_shape=jax.ShapeDtypeStruct((M, N), F32),
        compiler_params=_params(2),
        name="matmul_res",
    )(x, w, h)


def _softmax_step(s_list, m_ref, l_ref):
    cols = [[s[:, c * LANES:(c + 1) * LANES] for c in range(s.shape[1] // LANES)] for s in s_list]
    flat = [c for cs in cols for c in cs]
    m_blk = flat[0]
    for c in flat[1:]:
        m_blk = jnp.maximum(m_blk, c)
    m_prev = m_ref[...]
    m_new = jnp.maximum(m_prev, jnp.max(m_blk, axis=-1, keepdims=True))
    alpha = jnp.exp2(m_prev - m_new)
    probs = [[jnp.exp2(c - m_new) for c in cs] for cs in cols]
    flat = [p for ps in probs for p in ps]
    l_blk = flat[0]
    for p in flat[1:]:
        l_blk = l_blk + p
    l_ref[...] = alpha * l_ref[...] + jnp.sum(l_blk, axis=-1, keepdims=True)
    m_ref[...] = m_new
    return [jnp.concatenate(ps, axis=1).astype(BF16) if len(ps) > 1 else ps[0].astype(BF16)
            for ps in probs], alpha


def _head_out(o1, o2, lam, subln, lam_init):
    d = o1 - lam * o2
    return _rms_rows(d, subln) * (1.0 - lam_init)


def _attn_prompt_kernel(qi_tab, kj_tab, q_ref, k_ref, v_ref, lam_ref, subln_ref, o_ref,
                        m_ref, l_ref, acc_ref, *, lam_init):
    step = pl.program_id(2)
    qi = qi_tab[step]
    kj = kj_tab[step]
    tq = q_ref.shape[1]
    tk = k_ref.shape[1]

    @pl.when(kj == 0)
    def _():
        m_ref[...] = jnp.full(m_ref.shape, NEG_INF, F32)
        l_ref[...] = jnp.zeros(l_ref.shape, F32)
        acc_ref[...] = jnp.zeros(acc_ref.shape, F32)

    n_sub = m_ref.shape[0]

    def update(masked):
        for sub in range(n_sub):
            sl = slice(sub * HEAD_DIM, (sub + 1) * HEAD_DIM)
            vs = slice((sub // 2) * V_DIM, (sub // 2 + 1) * V_DIM)
            q = q_ref[0, :, sl]
            k = k_ref[0, :, sl].astype(BF16)
            s = lax.dot_general(q, k, (((1,), (1,)), ((), ())), preferred_element_type=F32)
            if masked:
                r = lax.broadcasted_iota(jnp.int32, (tq, tk), 0)
                c = lax.broadcasted_iota(jnp.int32, (tq, tk), 1)
                s = jnp.where(c <= r, s, NEG_INF)
            p, alpha = _softmax_step([s], m_ref.at[sub], l_ref.at[sub])
            acc_ref[sub] = (jnp.concatenate([alpha, alpha], axis=1) * acc_ref[sub]
                            + jnp.dot(p[0], v_ref[0, :, vs].astype(BF16), preferred_element_type=F32))

    @pl.when(kj < qi)
    def _():
        update(False)

    @pl.when(kj == qi)
    def _():
        update(True)
        lam = _lambda(lam_ref, lam_init)
        for hv in range(n_sub // 2):
            inv1 = 1.0 / l_ref[2 * hv]
            inv2 = 1.0 / l_ref[2 * hv + 1]
            o1 = acc_ref[2 * hv] * jnp.concatenate([inv1, inv1], axis=1)
            o2 = acc_ref[2 * hv + 1] * jnp.concatenate([inv2, inv2], axis=1)
            o_ref[0, :, hv * V_DIM:(hv + 1) * V_DIM] = _head_out(
                o1, o2, lam, subln_ref[...], lam_init).astype(o_ref.dtype)


def _attn_prompt(q, k, v, lam_vecs, subln, l, lam_init, *, tq, heads_per_step):
    B, T, _ = q.shape
    nq = T // tq
    pairs = [(i, j) for i in range(nq) for j in range(i + 1)]
    qi_tab = jnp.asarray([p[0] for p in pairs], jnp.int32)
    kj_tab = jnp.asarray([p[1] for p in pairs], jnp.int32)
    hw = heads_per_step * V_DIM
    n_sub = 2 * heads_per_step
    grid_spec = pltpu.PrefetchScalarGridSpec(
        num_scalar_prefetch=2,
        grid=(B, N_HEADS // heads_per_step, len(pairs)),
        in_specs=[
            pl.BlockSpec((1, tq, hw), lambda b, h, s, qt, kt: (b, qt[s], h)),
            pl.BlockSpec((1, tq, hw), lambda b, h, s, qt, kt: (b, kt[s], h)),
            pl.BlockSpec((1, tq, hw), lambda b, h, s, qt, kt: (b, kt[s], h)),
            pl.BlockSpec((None, 4, HEAD_DIM), lambda b, h, s, qt, kt: (l, 0, 0)),
            pl.BlockSpec((None, 1, V_DIM), lambda b, h, s, qt, kt: (l, 0, 0)),
        ],
        out_specs=pl.BlockSpec((1, tq, hw), lambda b, h, s, qt, kt: (b, qt[s], h)),
        scratch_shapes=[pltpu.VMEM((n_sub, tq, LANES), F32), pltpu.VMEM((n_sub, tq, LANES), F32),
                        pltpu.VMEM((n_sub, tq, V_DIM), F32)],
    )
    return pl.pallas_call(
        functools.partial(_attn_prompt_kernel, lam_init=lam_init),
        grid_spec=grid_spec,
        out_shape=jax.ShapeDtypeStruct((B, T, N_HEADS * V_DIM), BF16),
        compiler_params=_params(3),
        name="attn_prompt",
    )(qi_tab, kj_tab, q, k, v, lam_vecs, subln)


def _attn_sample_kernel(pt_ref, q_ref, *refs, pages_per_step, lam_init):
    k_refs = refs[:pages_per_step]
    v_refs = refs[pages_per_step:3 * pages_per_step]
    kn_ref, vn_ref, lam_ref, subln_ref, o_ref, qbd_ref, m_ref, l_ref, acc_ref = refs[3 * pages_per_step:]
    p_idx = pl.program_id(1)
    tq = q_ref.shape[1]
    rows = N_QK * tq
    d_all = N_QK * HEAD_DIM

    @pl.when(p_idx == 0)
    def _():
        m_ref[...] = jnp.full(m_ref.shape, NEG_INF, F32)
        l_ref[...] = jnp.zeros(l_ref.shape, F32)
        acc_ref[...] = jnp.zeros(acc_ref.shape, F32)
        qt = jnp.concatenate([q_ref[0]] * N_QK, axis=0)
        r = lax.broadcasted_iota(jnp.int32, (rows, d_all), 0)
        c = lax.broadcasted_iota(jnp.int32, (rows, d_all), 1)
        qbd_ref[...] = jnp.where(r // tq == c // HEAD_DIM, qt, 0.0).astype(BF16)

    def update(k_list, v_list, mask):
        qbd = qbd_ref[...]
        s_list = []
        for kp in k_list:
            s = lax.dot_general(qbd, kp, (((1,), (1,)), ((), ())), preferred_element_type=F32)
            if mask is not None:
                s = jnp.where(mask, s, NEG_INF)
            s_list.append(s)
        p_list, alpha = _softmax_step(s_list, m_ref, l_ref)
        for hv in range(N_HEADS):
            rs = slice(hv * 2 * tq, (hv + 1) * 2 * tq)
            pv = None
            for p, v_heads in zip(p_list, v_list):
                t = jnp.dot(p[rs, :], v_heads[hv], preferred_element_type=F32)
                pv = t if pv is None else pv + t
            a = alpha[rs, :]
            acc_ref[rs, :] = jnp.concatenate([a, a], axis=1) * acc_ref[rs, :] + pv

    def head_rows(ref, h, n_heads):
        return ref[pl.ds(h, PAGE_SIZE, stride=n_heads), :].astype(BF16)

    update([jnp.concatenate([head_rows(k, h, N_QK) for h in range(N_QK)], axis=1) for k in k_refs],
           [[jnp.concatenate([head_rows(v_refs[2 * i], hv, N_HEADS),
                              head_rows(v_refs[2 * i + 1], hv, N_HEADS)], axis=1)
             for hv in range(N_HEADS)] for i in range(pages_per_step)], None)

    @pl.when(p_idx == pl.num_programs(1) - 1)
    def _():
        pad = jnp.zeros((PAGE_SIZE - tq, d_all), F32)
        kn = jnp.concatenate([kn_ref[0], pad], axis=0).astype(BF16)
        vn = jnp.concatenate([vn_ref[0], pad], axis=0).astype(BF16)
        r = lax.broadcasted_iota(jnp.int32, (rows, PAGE_SIZE), 0)
        c = lax.broadcasted_iota(jnp.int32, (rows, PAGE_SIZE), 1)
        update([kn], [[vn[:, hv * V_DIM:(hv + 1) * V_DIM] for hv in range(N_HEADS)]], c <= r % tq)
        lam = _lambda(lam_ref, lam_init)
        inv = 1.0 / l_ref[...]
        o = acc_ref[...] * jnp.concatenate([inv, inv], axis=1)
        for hv in range(N_HEADS):
            o1 = o[hv * 2 * tq:hv * 2 * tq + tq, :]
            o2 = o[hv * 2 * tq + tq:(hv + 1) * 2 * tq, :]
            o_ref[0, :, hv * V_DIM:(hv + 1) * V_DIM] = _head_out(
                o1, o2, lam, subln_ref[...], lam_init).astype(o_ref.dtype)


def _attn_sample(q, cache_k, cache_v, page_table, k_new, v_new, lam_vecs, subln, l, lam_init, *,
                 pages_per_step):
    Bd, Tq, D = q.shape
    n_pages = page_table.shape[1]
    pps = pages_per_step

    def page_map(i, half):
        return lambda b, p, pt: (pt[b * n_pages + p * pps + i], 0, half)

    k_specs = [pl.BlockSpec((None, PAGE_SIZE * N_QK, HEAD_DIM), page_map(i, 0)) for i in range(pps)]
    v_specs = [pl.BlockSpec((None, PAGE_SIZE * N_HEADS, LANES), page_map(i, half))
               for i in range(pps) for half in range(V_DIM // LANES)]
    row_spec = pl.BlockSpec((1, Tq, D), lambda b, p, pt: (b, 0, 0))
    grid_spec = pltpu.PrefetchScalarGridSpec(
        num_scalar_prefetch=1,
        grid=(Bd, n_pages // pps),
        in_specs=[row_spec] + k_specs + v_specs + [
            row_spec, row_spec,
            pl.BlockSpec((None, 4, HEAD_DIM), lambda b, p, pt: (l, 0, 0)),
            pl.BlockSpec((None, 1, V_DIM), lambda b, p, pt: (l, 0, 0)),
        ],
        out_specs=row_spec,
        scratch_shapes=[pltpu.VMEM((N_QK * Tq, D), BF16),
                        pltpu.VMEM((N_QK * Tq, LANES), F32), pltpu.VMEM((N_QK * Tq, LANES), F32),
                        pltpu.VMEM((N_QK * Tq, V_DIM), F32)],
    )
    return pl.pallas_call(
        functools.partial(_attn_sample_kernel, pages_per_step=pps, lam_init=lam_init),
        grid_spec=grid_spec,
        out_shape=jax.ShapeDtypeStruct((Bd, Tq, D), BF16),
        compiler_params=_params(2),
        name="attn_sample",
    )(page_table.reshape(-1), q, *([cache_k] * pps), *([cache_v] * (2 * pps)), k_new, v_new, lam_vecs, subln)


def _rope_tables(pos):
    half = HEAD_DIM // 2
    inv = ROPE_THETA ** (-jnp.arange(half, dtype=F32) / half)
    ang = pos.astype(F32)[:, None] * inv[None, :]
    cos = jnp.cos(ang)
    sin = jnp.sin(ang)
    return jnp.concatenate([cos, cos], axis=-1), jnp.concatenate([-sin, sin], axis=-1)


def _trunk(x, pos, paged, p, cfg):
    B, T, _ = x.shape
    M = B * T
    h = x.reshape(M, D_MODEL)
    tm, rows, chunk = cfg["tm"], cfg["rows"], cfg["chunk"]
    cos, sin = _rope_tables(pos)
    if T < tm:
        cos = jnp.tile(cos, (tm // T, 1))
        sin = jnp.tile(sin, (tm // T, 1))
    v_rows = []
    k_sh = v_sh = None
    for l in range(DEPTH):
        if l < N_A_LAYERS:
            z = _norm_matmul(h, p["norm_a"], p["w_in_a"], l, tm=tm, tn=cfg["tn_in"],
                             out_dtype=cfg["z_dtype"], epilogue="gelu")
            ws = p["w_s_a"][l][:, :chunk, :chunk]
            bs_t = p["b_s_a"][l][:, :chunk].T
            if rows > chunk:
                ws = jnp.tile(ws, (1, rows // chunk, rows // chunk))
                bs_t = jnp.tile(bs_t, (rows // chunk, 1))
            h = _gmlp_out(z, ws, bs_t, h, p["w_out_a"], l, tm=cfg["tm_gmlp"], tn=cfg["tn"],
                          rows=rows, chunk=chunk)
            v_rows.append(z[:, D_GATE:])
        else:
            if l == N_A_LAYERS:
                k_sh = _norm_matmul(h, p["norm_kv"], p["w_k"], 0, tm=tm, tn=cfg["tn"], out_dtype=F32,
                                    epilogue="rope", rope=(cos, sin))
                v_sh = _norm_matmul(h, p["norm_kv"], p["w_v"], 0, tm=tm, tn=cfg["tn"], out_dtype=F32)
            j = l - N_A_LAYERS
            lam_init = 0.8 - 0.6 * math.exp(-0.3 * l)
            q = _norm_matmul(h, p["norm_b"], p["w_q"], j, tm=tm, tn=cfg["tn"],
                             out_dtype=cfg["q_dtype"], epilogue="rope", rope=(cos, sin),
                             scale=HEAD_DIM ** -0.5 * LOG2E)
            if paged is None:
                o = _attn_prompt(q.reshape(B, T, -1), k_sh.reshape(B, T, -1), v_sh.reshape(B, T, -1),
                                 p["lam_vecs"], p["subln_b"], j, lam_init, tq=cfg["tq"],
                                 heads_per_step=cfg["hps"])
            else:
                cache_k, cache_v, page_table = paged
                o = _attn_sample(q.reshape(B, T, -1), cache_k, cache_v, page_table,
                                 k_sh.reshape(B, T, -1), v_sh.reshape(B, T, -1),
                                 p["lam_vecs"], p["subln_b"], j, lam_init, pages_per_step=cfg["pps"])
            h = _matmul_res(o.reshape(M, -1), p["w_o_b"], j, h, tm=tm, tn=cfg["tn"])
        h = _ffn(h, p["norm_ffn"], p["w_up"], p["w_down"], l,
                 p["norm_f"] if l == DEPTH - 1 else None, tm=cfg["tm_ffn"], tf=cfg["tf"])
    return h, k_sh, v_sh, v_rows


def kernel(x_prompt, x_sample, cache_k, cache_v, page_table, norm_a, w_in_a, w_s_a, b_s_a, w_out_a, norm_kv, w_k, w_v, norm_b, w_q, lambda_q1, lambda_k1, lambda_q2, lambda_k2, subln_b, w_o_b, norm_ffn, w_up, w_down, norm_f):
    def gains(g):
        return g.reshape(g.shape[0], 1, g.shape[1])

    p = dict(norm_a=gains(norm_a), w_in_a=w_in_a.astype(BF16), w_s_a=w_s_a, b_s_a=b_s_a,
             w_out_a=w_out_a.astype(BF16), norm_kv=norm_kv.reshape(1, 1, -1),
             w_k=w_k.astype(BF16)[None], w_v=w_v.astype(BF16)[None], norm_b=gains(norm_b),
             w_q=w_q.astype(BF16),
             lam_vecs=jnp.stack([lambda_q1, lambda_k1, lambda_q2, lambda_k2], axis=1),
             subln_b=gains(subln_b), w_o_b=w_o_b.astype(BF16), norm_ffn=gains(norm_ffn),
             w_up=w_up.astype(BF16), w_down=w_down.astype(BF16), norm_f=norm_f.reshape(1, -1))

    B, T, _ = x_prompt.shape
    Bd, Td, _ = x_sample.shape

    cfg_p = dict(tm=1024, tn=512, tn_in=1024, tm_gmlp=512, tm_ffn=512, tf=1024, tq=512, hps=2,
                 rows=CHUNK, chunk=CHUNK, z_dtype=BF16, q_dtype=BF16)
    y_p, k_p, v_p, _ = _trunk(x_prompt, jnp.arange(T, dtype=jnp.int32), None, p, cfg_p)

    Ms = Bd * Td
    cfg_s = dict(tm=Ms, tn=1024, tn_in=1024, tm_gmlp=Ms, tm_ffn=Ms, tf=1024, pps=4,
                 rows=Ms, chunk=Td, z_dtype=F32, q_dtype=F32)
    n_pool = cache_k.shape[0]
    paged = (cache_k.reshape(n_pool, PAGE_SIZE * N_QK, HEAD_DIM),
             cache_v.reshape(n_pool, PAGE_SIZE * N_HEADS, V_DIM), page_table)
    y_s, k_s, v_s, vr_s = _trunk(x_sample, PAST_LEN + jnp.arange(Td, dtype=jnp.int32), paged, p, cfg_s)

    return (y_p.reshape(B, T, D_MODEL), y_s.reshape(Bd, Td, D_MODEL),
            k_p.reshape(B, T, N_QK, HEAD_DIM), v_p.reshape(B, T, N_HEADS, V_DIM),
            k_s.reshape(Bd, Td, N_QK, HEAD_DIM), v_s.reshape(Bd, Td, N_HEADS, V_DIM),
            jnp.stack(vr_s).reshape(N_A_LAYERS, Bd, Td, D_GATE))
```

```python
import functools
import math

import jax
import jax.numpy as jnp
from jax import lax
from jax.experimental import pallas as pl
from jax.experimental.pallas import tpu as pltpu

D_MODEL = 2048
DEPTH = 4
PAST_LEN = 16384
PAGE_SIZE = 128
N_A_LAYERS = DEPTH // 2
CHUNK = 128
D_GATE = 2 * D_MODEL
N_GROUPS_A = 16
GROUP_DIM_A = D_GATE // N_GROUPS_A
HEAD_DIM = 128
N_HEADS = D_MODEL // (2 * HEAD_DIM)
N_QK = 2 * N_HEADS
V_DIM = 2 * HEAD_DIM
D_FF = 4 * D_MODEL
ROPE_THETA = 10000.0
EPS = 1e-5
NEG_INF = -1e30

LANES = 128
VMEM_LIMIT = 56 * 1024 * 1024
LOG2E = math.log2(math.e)

F32 = jnp.float32
BF16 = jnp.bfloat16


def _params(n_axes):
    return pltpu.CompilerParams(dimension_semantics=("arbitrary",) * n_axes,
                                vmem_limit_bytes=VMEM_LIMIT)


def _rms_rows(x, g):
    return x * lax.rsqrt(jnp.mean(x * x, axis=-1, keepdims=True) + EPS) * g


def _lambda(lam_ref, lam_init):
    a = jnp.sum(lam_ref[0:1, :] * lam_ref[1:2, :], axis=-1, keepdims=True)
    b = jnp.sum(lam_ref[2:3, :] * lam_ref[3:4, :], axis=-1, keepdims=True)
    return jnp.exp(a) - jnp.exp(b) + lam_init


def _mxu_weight(w_ref, wb_ref):
    w = w_ref[...].astype(BF16)
    if wb_ref is not None:
        wb_ref[...] = w
    return w


def _norm_matmul_kernel(*refs, epilogue, scale, emit):
    x_ref, g_ref, w_ref = refs[:3]
    refs = refs[3:]
    if epilogue == "rope":
        cos_ref, sin_ref = refs[:2]
        refs = refs[2:]
    o_ref = refs[0]
    wb_ref = refs[1] if emit else None
    xn_ref = refs[-1]

    @pl.when(pl.program_id(1) == 0)
    def _():
        xn_ref[...] = _rms_rows(x_ref[...], g_ref[...]).astype(BF16)

    y = jnp.dot(xn_ref[...], _mxu_weight(w_ref, wb_ref), preferred_element_type=F32)
    if epilogue == "gelu":
        o_ref[...] = (0.5 * y * (1.0 + lax.erf(y * (2.0 ** -0.5)))).astype(o_ref.dtype)
    elif epilogue == "rope":
        cos = cos_ref[...]
        sin = sin_ref[...]
        for h in range(y.shape[1] // HEAD_DIM):
            sl = slice(h * HEAD_DIM, (h + 1) * HEAD_DIM)
            yh = y[:, sl]
            oh = yh * cos + pltpu.roll(yh, HEAD_DIM // 2, 1) * sin
            if scale != 1.0:
                oh = oh * scale
            o_ref[:, sl] = oh.astype(o_ref.dtype)
    else:
        o_ref[...] = y.astype(o_ref.dtype)


def _weight_copy_out(emit, M, tm, K, N, bk, bn, index_map):
    if not emit:
        return [], []
    assert M == tm
    return [pl.BlockSpec((bk, bn), index_map)], [jax.ShapeDtypeStruct((K, N), BF16)]


def _norm_matmul(x, gl, wl, *, tm, tn, out_dtype, epilogue=None, rope=None, scale=1.0, emit=False):
    g, lg = gl
    w, l = wl
    M, K = x.shape
    N = w.shape[2]
    wb_specs, wb_shapes = _weight_copy_out(emit, M, tm, K, N, K, tn, lambda i, j: (0, j))
    in_specs = [
        pl.BlockSpec((tm, K), lambda i, j: (i, 0)),
        pl.BlockSpec((None, 1, K), lambda i, j: (lg, 0, 0)),
        pl.BlockSpec((None, K, tn), lambda i, j: (l, 0, j)),
    ]
    args = [x, g, w]
    if epilogue == "rope":
        cos, sin = rope
        nb = cos.shape[0] // tm
        in_specs += [pl.BlockSpec((tm, HEAD_DIM), lambda i, j: (i % nb, 0))] * 2
        args += [cos, sin]
    outs = pl.pallas_call(
        functools.partial(_norm_matmul_kernel, epilogue=epilogue, scale=scale, emit=emit),
        grid=(M // tm, N // tn),
        in_specs=in_specs,
        out_specs=[pl.BlockSpec((tm, tn), lambda i, j: (i, j))] + wb_specs,
        out_shape=[jax.ShapeDtypeStruct((M, N), out_dtype)] + wb_shapes,
        scratch_shapes=[pltpu.VMEM((tm, K), BF16)],
        compiler_params=_params(2),
        name="norm_matmul_" + (epilogue or "plain"),
    )(*args)
    return (outs[0], outs[1][None]) if emit else outs[0]


def _gmlp_out_kernel(u_ref, v_ref, ws_ref, bs_ref, h_ref, w_ref, o_ref, *refs, rows, chunk, emit):
    wb_ref = refs[0] if emit else None
    gated_ref, wt_ref = refs[-2:]
    tm = u_ref.shape[0]

    @pl.when(pl.program_id(1) == 0)
    def _():
        r = lax.broadcasted_iota(jnp.int32, (rows, rows), 0)
        c = lax.broadcasted_iota(jnp.int32, (rows, rows), 1)
        mask = (r // chunk == c // chunk) & (r >= c)
        for g in range(N_GROUPS_A):
            wt_ref[g] = jnp.where(mask, ws_ref[g], 0.0).astype(BF16)

        def mix(ci, carry):
            r0 = pl.multiple_of(ci * rows, rows)
            for g in range(N_GROUPS_A):
                sl = slice(g * GROUP_DIM_A, (g + 1) * GROUP_DIM_A)
                vg = v_ref[pl.ds(r0, rows), sl].astype(BF16)
                s = jnp.dot(wt_ref[g], vg, preferred_element_type=F32) + bs_ref[:, g:g + 1]
                ug = u_ref[pl.ds(r0, rows), sl].astype(F32)
                gated_ref[pl.ds(r0, rows), sl] = (ug * s).astype(BF16)
            return carry

        lax.fori_loop(0, tm // rows, mix, 0)

    o_ref[...] = h_ref[...] + jnp.dot(gated_ref[...], _mxu_weight(w_ref, wb_ref),
                                      preferred_element_type=F32)


def _gmlp_out(z, ws, bs_t, h, wl, *, tm, tn, rows, chunk, emit=False):
    w_out, l = wl
    M = z.shape[0]
    N = w_out.shape[2]
    wb_specs, wb_shapes = _weight_copy_out(emit, M, tm, D_GATE, N, D_GATE, tn, lambda i, j: (0, j))
    outs = pl.pallas_call(
        functools.partial(_gmlp_out_kernel, rows=rows, chunk=chunk, emit=emit),
        grid=(M // tm, N // tn),
        in_specs=[
            pl.BlockSpec((tm, D_GATE), lambda i, j: (i, 0)),
            pl.BlockSpec((tm, D_GATE), lambda i, j: (i, 1)),
            pl.BlockSpec((N_GROUPS_A, rows, rows), lambda i, j: (0, 0, 0)),
            pl.BlockSpec((rows, N_GROUPS_A), lambda i, j: (0, 0)),
            pl.BlockSpec((tm, tn), lambda i, j: (i, j)),
            pl.BlockSpec((None, D_GATE, tn), lambda i, j: (l, 0, j)),
        ],
        out_specs=[pl.BlockSpec((tm, tn), lambda i, j: (i, j))] + wb_specs,
        out_shape=[jax.ShapeDtypeStruct((M, N), F32)] + wb_shapes,
        scratch_shapes=[pltpu.VMEM((tm, D_GATE), BF16),
                        pltpu.VMEM((N_GROUPS_A, rows, rows), BF16)],
        compiler_params=_params(2),
        name="gmlp_out",
    )(z, z, ws, bs_t, h, w_out)
    return (outs[0], outs[1][None]) if emit else outs[0]


def _ffn_kernel(*refs, final_norm, emit):
    x_ref, g_ref, wu_ref, wd_ref = refs[:4]
    refs = refs[4:]
    if final_norm:
        gf_ref = refs[0]
        refs = refs[1:]
    o_ref = refs[0]
    wub_ref, wdb_ref = refs[1:3] if emit else (None, None)
    xn_ref = refs[-1]
    f = pl.program_id(1)

    @pl.when(f == 0)
    def _():
        x = x_ref[...]
        xn_ref[...] = _rms_rows(x, g_ref[...]).astype(BF16)
        o_ref[...] = x

    a = jnp.dot(xn_ref[...], _mxu_weight(wu_ref, wub_ref), preferred_element_type=F32)
    a = jnp.square(jnp.maximum(a, 0.0)).astype(BF16)
    o_ref[...] += jnp.dot(a, _mxu_weight(wd_ref, wdb_ref), preferred_element_type=F32)

    if final_norm:
        @pl.when(f == pl.num_programs(1) - 1)
        def _():
            o_ref[...] = _rms_rows(o_ref[...], gf_ref[...])


def _ffn(x, gl, wul, wdl, g_final, *, tm, tf, emit=False):
    g, l = gl
    w_up, lu = wul
    w_down, ld = wdl
    M, K = x.shape
    F = w_up.shape[2]
    final_norm = g_final is not None
    in_specs = [
        pl.BlockSpec((tm, K), lambda i, f: (i, 0)),
        pl.BlockSpec((None, 1, K), lambda i, f: (l, 0, 0)),
        pl.BlockSpec((None, K, tf), lambda i, f: (lu, 0, f)),
        pl.BlockSpec((None, tf, K), lambda i, f: (ld, f, 0)),
    ]
    ub_specs, ub_shapes = _weight_copy_out(emit, M, tm, K, F, K, tf, lambda i, f: (0, f))
    db_specs, db_shapes = _weight_copy_out(emit, M, tm, F, K, tf, K, lambda i, f: (f, 0))
    args = [x, g, w_up, w_down]
    if final_norm:
        in_specs.append(pl.BlockSpec((1, K), lambda i, f: (0, 0)))
        args.append(g_final)
    outs = pl.pallas_call(
        functools.partial(_ffn_kernel, final_norm=final_norm, emit=emit),
        grid=(M // tm, F // tf),
        in_specs=in_specs,
        out_specs=[pl.BlockSpec((tm, K), lambda i, f: (i, 0))] + ub_specs + db_specs,
        out_shape=[jax.ShapeDtypeStruct((M, K), F32)] + ub_shapes + db_shapes,
        scratch_shapes=[pltpu.VMEM((tm, K), BF16)],
        compiler_params=_params(2),
        name="ffn",
    )(*args)
    return (outs[0], outs[1][None], outs[2][None]) if emit else outs[0]


def _matmul_res_kernel(x_ref, w_ref, h_ref, o_ref, wb_ref=None):
    o_ref[...] = h_ref[...] + jnp.dot(x_ref[...], _mxu_weight(w_ref, wb_ref), preferred_element_type=F32)


def _matmul_res(x, wl, h, *, tm, tn, emit=False):
    w, l = wl
    M, K = x.shape
    N = w.shape[2]
    wb_specs, wb_shapes = _weight_copy_out(emit, M, tm, K, N, K, tn, lambda i, j: (0, j))
    outs = pl.pallas_call(
        _matmul_res_kernel,
        grid=(M // tm, N // tn),
        in_specs=[
            pl.BlockSpec((tm, K), lambda i, j: (i, 0)),
            pl.BlockSpec((None, K, tn), lambda i, j: (l, 0, j)),
            pl.BlockSpec((tm, tn), lambda i, j: (i, j)),
        ],
        out_specs=[pl.BlockSpec((tm, tn), lambda i, j: (i, j))] + wb_specs,
        out_shape=[jax.ShapeDtypeStruct((M, N), F32)] + wb_shapes,
        compiler_params=_params(2),
        name="matmul_res",
    )(x, w, h)
    return (outs[0], outs[1][None]) if emit else outs[0]


def _softmax_step(s_list, m_ref, l_ref):
    cols = [[s[:, c * LANES:(c + 1) * LANES] for c in range(s.shape[1] // LANES)] for s in s_list]
    flat = [c for cs in cols for c in cs]
    m_blk = flat[0]
    for c in flat[1:]:
        m_blk = jnp.maximum(m_blk, c)
    m_prev = m_ref[...]
    m_new = jnp.maximum(m_prev, jnp.max(m_blk, axis=-1, keepdims=True))
    alpha = jnp.exp2(m_prev - m_new)
    probs = [[jnp.exp2(c - m_new) for c in cs] for cs in cols]
    flat = [p for ps in probs for p in ps]
    l_blk = flat[0]
    for p in flat[1:]:
        l_blk = l_blk + p
    l_ref[...] = alpha * l_ref[...] + jnp.sum(l_blk, axis=-1, keepdims=True)
    m_ref[...] = m_new
    return [jnp.concatenate(ps, axis=1).astype(BF16) if len(ps) > 1 else ps[0].astype(BF16)
            for ps in probs], alpha


def _head_out(o1, o2, lam, subln, lam_init):
    d = o1 - lam * o2
    return _rms_rows(d, subln) * (1.0 - lam_init)


def _attn_prompt_kernel(qi_tab, kj_tab, q_ref, k_ref, v_ref, lam_ref, subln_ref, o_ref,
                        m_ref, l_ref, acc_ref, *, lam_init):
    step = pl.program_id(2)
    qi = qi_tab[step]
    kj = kj_tab[step]
    tq = q_ref.shape[1]
    tk = k_ref.shape[1]

    @pl.when(kj == 0)
    def _():
        m_ref[...] = jnp.full(m_ref.shape, NEG_INF, F32)
        l_ref[...] = jnp.zeros(l_ref.shape, F32)
        acc_ref[...] = jnp.zeros(acc_ref.shape, F32)

    n_sub = m_ref.shape[0]

    def update(masked):
        for sub in range(n_sub):
            sl = slice(sub * HEAD_DIM, (sub + 1) * HEAD_DIM)
            vs = slice((sub // 2) * V_DIM, (sub // 2 + 1) * V_DIM)
            q = q_ref[0, :, sl]
            k = k_ref[0, :, sl].astype(BF16)
            s = lax.dot_general(q, k, (((1,), (1,)), ((), ())), preferred_element_type=F32)
            if masked:
                r = lax.broadcasted_iota(jnp.int32, (tq, tk), 0)
                c = lax.broadcasted_iota(jnp.int32, (tq, tk), 1)
                s = jnp.where(c <= r, s, NEG_INF)
            p, alpha = _softmax_step([s], m_ref.at[sub], l_ref.at[sub])
            acc_ref[sub] = (jnp.concatenate([alpha, alpha], axis=1) * acc_ref[sub]
                            + jnp.dot(p[0], v_ref[0, :, vs].astype(BF16), preferred_element_type=F32))

    @pl.when(kj < qi)
    def _():
        update(False)

    @pl.when(kj == qi)
    def _():
        update(True)
        lam = _lambda(lam_ref, lam_init)
        for hv in range(n_sub // 2):
            inv1 = 1.0 / l_ref[2 * hv]
            inv2 = 1.0 / l_ref[2 * hv + 1]
            o1 = acc_ref[2 * hv] * jnp.concatenate([inv1, inv1], axis=1)
            o2 = acc_ref[2 * hv + 1] * jnp.concatenate([inv2, inv2], axis=1)
            o_ref[0, :, hv * V_DIM:(hv + 1) * V_DIM] = _head_out(
                o1, o2, lam, subln_ref[...], lam_init).astype(o_ref.dtype)


def _attn_prompt(q, k, v, lam_vecs, subln, l, lam_init, *, tq, heads_per_step):
    B, T, _ = q.shape
    nq = T // tq
    pairs = [(i, j) for i in range(nq) for j in range(i + 1)]
    qi_tab = jnp.asarray([p[0] for p in pairs], jnp.int32)
    kj_tab = jnp.asarray([p[1] for p in pairs], jnp.int32)
    hw = heads_per_step * V_DIM
    n_sub = 2 * heads_per_step
    grid_spec = pltpu.PrefetchScalarGridSpec(
        num_scalar_prefetch=2,
        grid=(B, N_HEADS // heads_per_step, len(pairs)),
        in_specs=[
            pl.BlockSpec((1, tq, hw), lambda b, h, s, qt, kt: (b, qt[s], h)),
            pl.BlockSpec((1, tq, hw), lambda b, h, s, qt, kt: (b, kt[s], h)),
            pl.BlockSpec((1, tq, hw), lambda b, h, s, qt, kt: (b, kt[s], h)),
            pl.BlockSpec((None, 4, HEAD_DIM), lambda b, h, s, qt, kt: (l, 0, 0)),
            pl.BlockSpec((None, 1, V_DIM), lambda b, h, s, qt, kt: (l, 0, 0)),
        ],
        out_specs=pl.BlockSpec((1, tq, hw), lambda b, h, s, qt, kt: (b, qt[s], h)),
        scratch_shapes=[pltpu.VMEM((n_sub, tq, LANES), F32), pltpu.VMEM((n_sub, tq, LANES), F32),
                        pltpu.VMEM((n_sub, tq, V_DIM), F32)],
    )
    return pl.pallas_call(
        functools.partial(_attn_prompt_kernel, lam_init=lam_init),
        grid_spec=grid_spec,
        out_shape=jax.ShapeDtypeStruct((B, T, N_HEADS * V_DIM), BF16),
        compiler_params=_params(3),
        name="attn_prompt",
    )(qi_tab, kj_tab, q, k, v, lam_vecs, subln)


def _attn_sample_kernel(pt_ref, q_ref, *refs, pages_per_step, lam_init):
    k_refs = refs[:pages_per_step]
    v_refs = refs[pages_per_step:3 * pages_per_step]
    kn_ref, vn_ref, lam_ref, subln_ref, o_ref, qbd_ref, m_ref, l_ref, acc_ref = refs[3 * pages_per_step:]
    p_idx = pl.program_id(1)
    tq = q_ref.shape[1]
    rows = N_QK * tq
    d_all = N_QK * HEAD_DIM

    @pl.when(p_idx == 0)
    def _():
        m_ref[...] = jnp.full(m_ref.shape, NEG_INF, F32)
        l_ref[...] = jnp.zeros(l_ref.shape, F32)
        acc_ref[...] = jnp.zeros(acc_ref.shape, F32)
        qt = jnp.concatenate([q_ref[0]] * N_QK, axis=0)
        r = lax.broadcasted_iota(jnp.int32, (rows, d_all), 0)
        c = lax.broadcasted_iota(jnp.int32, (rows, d_all), 1)
        qbd_ref[...] = jnp.where(r // tq == c // HEAD_DIM, qt, 0.0).astype(BF16)

    def update(k_list, v_list, mask):
        qbd = qbd_ref[...]
        s_list = []
        for kp in k_list:
            s = lax.dot_general(qbd, kp, (((1,), (1,)), ((), ())), preferred_element_type=F32)
            if mask is not None:
                s = jnp.where(mask, s, NEG_INF)
            s_list.append(s)
        p_list, alpha = _softmax_step(s_list, m_ref, l_ref)
        for hv in range(N_HEADS):
            rs = slice(hv * 2 * tq, (hv + 1) * 2 * tq)
            pv = None
            for p, v_heads in zip(p_list, v_list):
                t = jnp.dot(p[rs, :], v_heads[hv], preferred_element_type=F32)
                pv = t if pv is None else pv + t
            a = alpha[rs, :]
            acc_ref[rs, :] = jnp.concatenate([a, a], axis=1) * acc_ref[rs, :] + pv

    def head_rows(ref, h, n_heads):
        return ref[pl.ds(h, PAGE_SIZE, stride=n_heads), :].astype(BF16)

    update([jnp.concatenate([head_rows(k, h, N_QK) for h in range(N_QK)], axis=1) for k in k_refs],
           [[jnp.concatenate([head_rows(v_refs[2 * i], hv, N_HEADS),
                              head_rows(v_refs[2 * i + 1], hv, N_HEADS)], axis=1)
             for hv in range(N_HEADS)] for i in range(pages_per_step)], None)

    @pl.when(p_idx == pl.num_programs(1) - 1)
    def _():
        pad = jnp.zeros((PAGE_SIZE - tq, d_all), F32)
        kn = jnp.concatenate([kn_ref[0], pad], axis=0).astype(BF16)
        vn = jnp.concatenate([vn_ref[0], pad], axis=0).astype(BF16)
        r = lax.broadcasted_iota(jnp.int32, (rows, PAGE_SIZE), 0)
        c = lax.broadcasted_iota(jnp.int32, (rows, PAGE_SIZE), 1)
        update([kn], [[vn[:, hv * V_DIM:(hv + 1) * V_DIM] for hv in range(N_HEADS)]], c <= r % tq)
        lam = _lambda(lam_ref, lam_init)
        inv = 1.0 / l_ref[...]
        o = acc_ref[...] * jnp.concatenate([inv, inv], axis=1)
        for hv in range(N_HEADS):
            o1 = o[hv * 2 * tq:hv * 2 * tq + tq, :]
            o2 = o[hv * 2 * tq + tq:(hv + 1) * 2 * tq, :]
            o_ref[0, :, hv * V_DIM:(hv + 1) * V_DIM] = _head_out(
                o1, o2, lam, subln_ref[...], lam_init).astype(o_ref.dtype)


def _attn_sample(q, cache_k, cache_v, page_table, k_new, v_new, lam_vecs, subln, l, lam_init, *,
                 pages_per_step):
    Bd, Tq, D = q.shape
    n_pages = page_table.shape[1]
    pps = pages_per_step

    def page_map(i, half):
        return lambda b, p, pt: (pt[b * n_pages + p * pps + i], 0, half)

    k_specs = [pl.BlockSpec((None, PAGE_SIZE * N_QK, HEAD_DIM), page_map(i, 0)) for i in range(pps)]
    v_specs = [pl.BlockSpec((None, PAGE_SIZE * N_HEADS, LANES), page_map(i, half))
               for i in range(pps) for half in range(V_DIM // LANES)]
    row_spec = pl.BlockSpec((1, Tq, D), lambda b, p, pt: (b, 0, 0))
    grid_spec = pltpu.PrefetchScalarGridSpec(
        num_scalar_prefetch=1,
        grid=(Bd, n_pages // pps),
        in_specs=[row_spec] + k_specs + v_specs + [
            row_spec, row_spec,
            pl.BlockSpec((None, 4, HEAD_DIM), lambda b, p, pt: (l, 0, 0)),
            pl.BlockSpec((None, 1, V_DIM), lambda b, p, pt: (l, 0, 0)),
        ],
        out_specs=row_spec,
        scratch_shapes=[pltpu.VMEM((N_QK * Tq, D), BF16),
                        pltpu.VMEM((N_QK * Tq, LANES), F32), pltpu.VMEM((N_QK * Tq, LANES), F32),
                        pltpu.VMEM((N_QK * Tq, V_DIM), F32)],
    )
    return pl.pallas_call(
        functools.partial(_attn_sample_kernel, pages_per_step=pps, lam_init=lam_init),
        grid_spec=grid_spec,
        out_shape=jax.ShapeDtypeStruct((Bd, Tq, D), BF16),
        compiler_params=_params(2),
        name="attn_sample",
    )(page_table.reshape(-1), q, *([cache_k] * pps), *([cache_v] * (2 * pps)), k_new, v_new, lam_vecs, subln)


def _rope_tables(pos):
    half = HEAD_DIM // 2
    inv = ROPE_THETA ** (-jnp.arange(half, dtype=F32) / half)
    ang = pos.astype(F32)[:, None] * inv[None, :]
    cos = jnp.cos(ang)
    sin = jnp.sin(ang)
    return jnp.concatenate([cos, cos], axis=-1), jnp.concatenate([-sin, sin], axis=-1)


def _trunk(x, pos, paged, p, W, cfg, emit):
    B, T, _ = x.shape
    M = B * T
    h = x.reshape(M, D_MODEL)
    tm, rows, chunk = cfg["tm"], cfg["rows"], cfg["chunk"]
    cos, sin = _rope_tables(pos)
    if T < tm:
        cos = jnp.tile(cos, (tm // T, 1))
        sin = jnp.tile(sin, (tm // T, 1))
    v_rows = []
    k_sh = v_sh = None
    Wb = {name: [] for name in W}

    def keep(name, outs):
        if not emit:
            return outs
        for n, wb in zip(name.split(","), outs[1:]):
            Wb[n].append((wb, 0))
        return outs[0]

    for l in range(DEPTH):
        if l < N_A_LAYERS:
            z = keep("w_in_a", _norm_matmul(h, p["norm_a"][l], W["w_in_a"][l], tm=cfg["tm_in"], tn=cfg["tn_in"],
                                            out_dtype=cfg["z_dtype"], epilogue="gelu", emit=emit))
            ws = p["w_s_a"][l][:, :chunk, :chunk]
            bs_t = p["b_s_a"][l][:, :chunk].T
            if rows > chunk:
                ws = jnp.tile(ws, (1, rows // chunk, rows // chunk))
                bs_t = jnp.tile(bs_t, (rows // chunk, 1))
            h = keep("w_out_a", _gmlp_out(z, ws, bs_t, h, W["w_out_a"][l], tm=cfg["tm_gmlp"], tn=cfg["tn_gmlp"],
                                          rows=rows, chunk=chunk, emit=emit))
            v_rows.append(z[:, D_GATE:])
        else:
            if l == N_A_LAYERS:
                k_sh = keep("w_k", _norm_matmul(h, p["norm_kv"][0], W["w_k"][0], tm=tm, tn=cfg["tn"], out_dtype=F32,
                                                epilogue="rope", rope=(cos, sin), emit=emit))
                v_sh = keep("w_v", _norm_matmul(h, p["norm_kv"][0], W["w_v"][0], tm=tm, tn=cfg["tn"], out_dtype=F32,
                                                emit=emit))
            j = l - N_A_LAYERS
            lam_init = 0.8 - 0.6 * math.exp(-0.3 * l)
            q = keep("w_q", _norm_matmul(h, p["norm_b"][j], W["w_q"][j], tm=tm, tn=cfg["tn"],
                                         out_dtype=cfg["q_dtype"], epilogue="rope", rope=(cos, sin),
                                         scale=HEAD_DIM ** -0.5 * LOG2E, emit=emit))
            if paged is None:
                o = _attn_prompt(q.reshape(B, T, -1), k_sh.reshape(B, T, -1), v_sh.reshape(B, T, -1),
                                 p["lam_vecs"], p["subln_b"], j, lam_init, tq=cfg["tq"],
                                 heads_per_step=cfg["hps"])
            else:
                cache_k, cache_v, page_table = paged
                o = _attn_sample(q.reshape(B, T, -1), cache_k, cache_v, page_table,
                                 k_sh.reshape(B, T, -1), v_sh.reshape(B, T, -1),
                                 p["lam_vecs"], p["subln_b"], j, lam_init, pages_per_step=cfg["pps"])
            h = keep("w_o_b", _matmul_res(o.reshape(M, -1), W["w_o_b"][j], h, tm=tm, tn=cfg["tn"], emit=emit))
        h = keep("w_up,w_down", _ffn(h, p["norm_ffn"][l], W["w_up"][l], W["w_down"][l],
                                     p["norm_f"] if l == DEPTH - 1 else None,
                                     tm=cfg["tm_ffn"], tf=cfg["tf"], emit=emit))
    return h, k_sh, v_sh, v_rows, Wb


def kernel(x_prompt, x_sample, cache_k, cache_v, page_table, norm_a, w_in_a, w_s_a, b_s_a, w_out_a, norm_kv, w_k, w_v, norm_b, w_q, lambda_q1, lambda_k1, lambda_q2, lambda_k2, subln_b, w_o_b, norm_ffn, w_up, w_down, norm_f):
    def layers(a):
        return [(a, l) for l in range(a.shape[0])]

    def gains(g):
        return layers(g.reshape(g.shape[0], 1, g.shape[1]))

    p = dict(norm_a=gains(norm_a), w_s_a=w_s_a, b_s_a=b_s_a, norm_kv=gains(norm_kv[None]),
             norm_b=gains(norm_b),
             lam_vecs=jnp.stack([lambda_q1, lambda_k1, lambda_q2, lambda_k2], axis=1),
             subln_b=subln_b.reshape(subln_b.shape[0], 1, -1), norm_ffn=gains(norm_ffn),
             norm_f=norm_f.reshape(1, -1))
    W = dict(w_in_a=layers(w_in_a), w_out_a=layers(w_out_a), w_k=layers(w_k[None]), w_v=layers(w_v[None]),
             w_q=layers(w_q), w_o_b=layers(w_o_b), w_up=layers(w_up), w_down=layers(w_down))

    B, T, _ = x_prompt.shape
    Bd, Td, _ = x_sample.shape

    Ms = Bd * Td
    cfg_s = dict(tm=Ms, tn=1024, tm_in=Ms, tn_in=1024, tm_gmlp=Ms, tn_gmlp=512, tm_ffn=Ms, tf=512, pps=4,
                 rows=Ms, chunk=Td, z_dtype=F32, q_dtype=F32)
    n_pool = cache_k.shape[0]
    paged = (cache_k.reshape(n_pool, PAGE_SIZE * N_QK, HEAD_DIM),
             cache_v.reshape(n_pool, PAGE_SIZE * N_HEADS, V_DIM), page_table)
    y_s, k_s, v_s, vr_s, Wb = _trunk(x_sample, PAST_LEN + jnp.arange(Td, dtype=jnp.int32), paged, p, W,
                                     cfg_s, True)

    cfg_p = dict(tm=512, tn=2048, tm_in=1024, tn_in=1024, tm_gmlp=512, tn_gmlp=1024, tm_ffn=512, tf=1024,
                 tq=512, hps=2, rows=CHUNK, chunk=CHUNK, z_dtype=BF16, q_dtype=BF16)
    y_p, k_p, v_p, _, _ = _trunk(x_prompt, jnp.arange(T, dtype=jnp.int32), None, p, Wb, cfg_p, False)

    return (y_p.reshape(B, T, D_MODEL), y_s.reshape(Bd, Td, D_MODEL),
            k_p.reshape(B, T, N_QK, HEAD_DIM), v_p.reshape(B, T, N_HEADS, V_DIM),
            k_s.reshape(Bd, Td, N_QK, HEAD_DIM), v_s.reshape(Bd, Td, N_HEADS, V_DIM),
            jnp.stack(vr_s).reshape(N_A_LAYERS, Bd, Td, D_GATE))
```

```python
import functools
import math

import jax
import jax.numpy as jnp
from jax import lax
from jax.experimental import pallas as pl
from jax.experimental.pallas import tpu as pltpu

D_MODEL = 2048
DEPTH = 4
PAST_LEN = 16384
PAGE_SIZE = 128
N_A_LAYERS = DEPTH // 2
CHUNK = 128
D_GATE = 2 * D_MODEL
N_GROUPS_A = 16
GROUP_DIM_A = D_GATE // N_GROUPS_A
HEAD_DIM = 128
N_HEADS = D_MODEL // (2 * HEAD_DIM)
N_QK = 2 * N_HEADS
V_DIM = 2 * HEAD_DIM
D_FF = 4 * D_MODEL
ROPE_THETA = 10000.0
EPS = 1e-5
NEG_INF = -1e30

LANES = 128
VMEM_LIMIT = 56 * 1024 * 1024
LOG2E = math.log2(math.e)

F32 = jnp.float32
BF16 = jnp.bfloat16


def _params(n_axes):
    return pltpu.CompilerParams(dimension_semantics=("arbitrary",) * n_axes,
                                vmem_limit_bytes=VMEM_LIMIT)


def _rms_rows(x, g):
    return x * lax.rsqrt(jnp.mean(x * x, axis=-1, keepdims=True) + EPS) * g


def _lambda(lam_ref, lam_init):
    a = jnp.sum(lam_ref[0:1, :] * lam_ref[1:2, :], axis=-1, keepdims=True)
    b = jnp.sum(lam_ref[2:3, :] * lam_ref[3:4, :], axis=-1, keepdims=True)
    return jnp.exp(a) - jnp.exp(b) + lam_init


def _mxu_weight(w_ref, wb_ref):
    w = w_ref[...].astype(BF16)
    if wb_ref is not None:
        wb_ref[...] = w
    return w


def _norm_matmul_kernel(*refs, epilogue, scale, emit, dup):
    x_ref, g_ref, w_ref = refs[:3]
    refs = refs[3:]
    if epilogue == "rope":
        cos_ref, sin_ref = refs[:2]
        refs = refs[2:]
    out_refs = refs[:2] if dup else refs[:1]
    wb_ref = refs[len(out_refs)] if emit else None
    xn_ref = refs[-1]

    def put(sl, val):
        for o_ref in out_refs:
            o_ref[:, sl] = val.astype(o_ref.dtype)

    @pl.when(pl.program_id(1) == 0)
    def _():
        xn_ref[...] = _rms_rows(x_ref[...], g_ref[...]).astype(BF16)

    y = jnp.dot(xn_ref[...], _mxu_weight(w_ref, wb_ref), preferred_element_type=F32)
    if epilogue == "gelu":
        put(slice(None), 0.5 * y * (1.0 + lax.erf(y * (2.0 ** -0.5))))
    elif epilogue == "rope":
        cos = cos_ref[...]
        sin = sin_ref[...]
        for h in range(y.shape[1] // HEAD_DIM):
            sl = slice(h * HEAD_DIM, (h + 1) * HEAD_DIM)
            yh = y[:, sl]
            oh = yh * cos + pltpu.roll(yh, HEAD_DIM // 2, 1) * sin
            if scale != 1.0:
                oh = oh * scale
            put(sl, oh)
    else:
        put(slice(None), y)


def _weight_copy_out(emit, M, tm, K, N, bk, bn, index_map):
    if not emit:
        return [], []
    assert M == tm
    return [pl.BlockSpec((bk, bn), index_map)], [jax.ShapeDtypeStruct((K, N), BF16)]


def _norm_matmul(x, gl, wl, *, tm, tn, out_dtype, epilogue=None, rope=None, scale=1.0, emit=False,
                 dup=False):
    assert not (emit and dup)
    g, lg = gl
    w, l = wl
    M, K = x.shape
    N = w.shape[2]
    wb_specs, wb_shapes = _weight_copy_out(emit, M, tm, K, N, K, tn, lambda i, j: (0, j))
    in_specs = [
        pl.BlockSpec((tm, K), lambda i, j: (i, 0)),
        pl.BlockSpec((None, 1, K), lambda i, j: (lg, 0, 0)),
        pl.BlockSpec((None, K, tn), lambda i, j: (l, 0, j)),
    ]
    args = [x, g, w]
    if epilogue == "rope":
        cos, sin = rope
        nb = cos.shape[0] // tm
        in_specs += [pl.BlockSpec((tm, HEAD_DIM), lambda i, j: (i % nb, 0))] * 2
        args += [cos, sin]
    outs = pl.pallas_call(
        functools.partial(_norm_matmul_kernel, epilogue=epilogue, scale=scale, emit=emit, dup=dup),
        grid=(M // tm, N // tn),
        in_specs=in_specs,
        out_specs=[pl.BlockSpec((tm, tn), lambda i, j: (i, j))] * (2 if dup else 1) + wb_specs,
        out_shape=[jax.ShapeDtypeStruct((M, N), out_dtype)]
        + ([jax.ShapeDtypeStruct((M, N), BF16)] if dup else []) + wb_shapes,
        scratch_shapes=[pltpu.VMEM((tm, K), BF16)],
        compiler_params=_params(2),
        name="norm_matmul_" + (epilogue or "plain"),
    )(*args)
    if emit:
        return outs[0], outs[1][None]
    return (outs[0], outs[1]) if dup else outs[0]


def _gmlp_out_kernel(u_ref, v_ref, ws_ref, bs_ref, h_ref, w_ref, o_ref, *refs, rows, chunk, emit):
    wb_ref = refs[0] if emit else None
    gated_ref, wt_ref = refs[-2:]
    tm = u_ref.shape[0]

    @pl.when(pl.program_id(1) == 0)
    def _():
        r = lax.broadcasted_iota(jnp.int32, (rows, rows), 0)
        c = lax.broadcasted_iota(jnp.int32, (rows, rows), 1)
        mask = (r // chunk == c // chunk) & (r >= c)
        for g in range(N_GROUPS_A):
            wt_ref[g] = jnp.where(mask, ws_ref[g], 0.0).astype(BF16)

        def mix(ci, carry):
            r0 = pl.multiple_of(ci * rows, rows)
            for g in range(N_GROUPS_A):
                sl = slice(g * GROUP_DIM_A, (g + 1) * GROUP_DIM_A)
                vg = v_ref[pl.ds(r0, rows), sl].astype(BF16)
                s = jnp.dot(wt_ref[g], vg, preferred_element_type=F32) + bs_ref[:, g:g + 1]
                ug = u_ref[pl.ds(r0, rows), sl].astype(F32)
                gated_ref[pl.ds(r0, rows), sl] = (ug * s).astype(BF16)
            return carry

        lax.fori_loop(0, tm // rows, mix, 0)

    o_ref[...] = h_ref[...] + jnp.dot(gated_ref[...], _mxu_weight(w_ref, wb_ref),
                                      preferred_element_type=F32)


def _gmlp_out(z, ws, bs_t, h, wl, *, tm, tn, rows, chunk, emit=False):
    w_out, l = wl
    M = z.shape[0]
    N = w_out.shape[2]
    wb_specs, wb_shapes = _weight_copy_out(emit, M, tm, D_GATE, N, D_GATE, tn, lambda i, j: (0, j))
    outs = pl.pallas_call(
        functools.partial(_gmlp_out_kernel, rows=rows, chunk=chunk, emit=emit),
        grid=(M // tm, N // tn),
        in_specs=[
            pl.BlockSpec((tm, D_GATE), lambda i, j: (i, 0)),
            pl.BlockSpec((tm, D_GATE), lambda i, j: (i, 1)),
            pl.BlockSpec((N_GROUPS_A, rows, rows), lambda i, j: (0, 0, 0)),
            pl.BlockSpec((rows, N_GROUPS_A), lambda i, j: (0, 0)),
            pl.BlockSpec((tm, tn), lambda i, j: (i, j)),
            pl.BlockSpec((None, D_GATE, tn), lambda i, j: (l, 0, j)),
        ],
        out_specs=[pl.BlockSpec((tm, tn), lambda i, j: (i, j))] + wb_specs,
        out_shape=[jax.ShapeDtypeStruct((M, N), F32)] + wb_shapes,
        scratch_shapes=[pltpu.VMEM((tm, D_GATE), BF16),
                        pltpu.VMEM((N_GROUPS_A, rows, rows), BF16)],
        compiler_params=_params(2),
        name="gmlp_out",
    )(z, z, ws, bs_t, h, w_out)
    return (outs[0], outs[1][None]) if emit else outs[0]


def _ffn_kernel(*refs, final_norm, emit):
    x_ref, g_ref, wu_ref, wd_ref = refs[:4]
    refs = refs[4:]
    if final_norm:
        gf_ref = refs[0]
        refs = refs[1:]
    o_ref = refs[0]
    wub_ref, wdb_ref = refs[1:3] if emit else (None, None)
    xn_ref = refs[-1]
    f = pl.program_id(1)

    @pl.when(f == 0)
    def _():
        x = x_ref[...]
        xn_ref[...] = _rms_rows(x, g_ref[...]).astype(BF16)
        o_ref[...] = x

    a = jnp.dot(xn_ref[...], _mxu_weight(wu_ref, wub_ref), preferred_element_type=F32)
    a = jnp.square(jnp.maximum(a, 0.0)).astype(BF16)
    o_ref[...] += jnp.dot(a, _mxu_weight(wd_ref, wdb_ref), preferred_element_type=F32)

    if final_norm:
        @pl.when(f == pl.num_programs(1) - 1)
        def _():
            o_ref[...] = _rms_rows(o_ref[...], gf_ref[...])


def _ffn(x, gl, wul, wdl, g_final, *, tm, tf, emit=False):
    g, l = gl
    w_up, lu = wul
    w_down, ld = wdl
    M, K = x.shape
    F = w_up.shape[2]
    final_norm = g_final is not None
    in_specs = [
        pl.BlockSpec((tm, K), lambda i, f: (i, 0)),
        pl.BlockSpec((None, 1, K), lambda i, f: (l, 0, 0)),
        pl.BlockSpec((None, K, tf), lambda i, f: (lu, 0, f)),
        pl.BlockSpec((None, tf, K), lambda i, f: (ld, f, 0)),
    ]
    ub_specs, ub_shapes = _weight_copy_out(emit, M, tm, K, F, K, tf, lambda i, f: (0, f))
    db_specs, db_shapes = _weight_copy_out(emit, M, tm, F, K, tf, K, lambda i, f: (f, 0))
    args = [x, g, w_up, w_down]
    if final_norm:
        in_specs.append(pl.BlockSpec((1, K), lambda i, f: (0, 0)))
        args.append(g_final)
    outs = pl.pallas_call(
        functools.partial(_ffn_kernel, final_norm=final_norm, emit=emit),
        grid=(M // tm, F // tf),
        in_specs=in_specs,
        out_specs=[pl.BlockSpec((tm, K), lambda i, f: (i, 0))] + ub_specs + db_specs,
        out_shape=[jax.ShapeDtypeStruct((M, K), F32)] + ub_shapes + db_shapes,
        scratch_shapes=[pltpu.VMEM((tm, K), BF16)],
        compiler_params=_params(2),
        name="ffn",
    )(*args)
    return (outs[0], outs[1][None], outs[2][None]) if emit else outs[0]


def _matmul_res_kernel(x_ref, w_ref, h_ref, o_ref, wb_ref=None):
    o_ref[...] = h_ref[...] + jnp.dot(x_ref[...], _mxu_weight(w_ref, wb_ref), preferred_element_type=F32)


def _matmul_res(x, wl, h, *, tm, tn, emit=False):
    w, l = wl
    M, K = x.shape
    N = w.shape[2]
    wb_specs, wb_shapes = _weight_copy_out(emit, M, tm, K, N, K, tn, lambda i, j: (0, j))
    outs = pl.pallas_call(
        _matmul_res_kernel,
        grid=(M // tm, N // tn),
        in_specs=[
            pl.BlockSpec((tm, K), lambda i, j: (i, 0)),
            pl.BlockSpec((None, K, tn), lambda i, j: (l, 0, j)),
            pl.BlockSpec((tm, tn), lambda i, j: (i, j)),
        ],
        out_specs=[pl.BlockSpec((tm, tn), lambda i, j: (i, j))] + wb_specs,
        out_shape=[jax.ShapeDtypeStruct((M, N), F32)] + wb_shapes,
        compiler_params=_params(2),
        name="matmul_res",
    )(x, w, h)
    return (outs[0], outs[1][None]) if emit else outs[0]


def _lane_chunks(s):
    return [s[:, c * LANES:(c + 1) * LANES] for c in range(s.shape[1] // LANES)]


def _chunk_max(chunks):
    m = chunks[0]
    for c in chunks[1:]:
        m = jnp.maximum(m, c)
    return m


def _softmax_step(s_list, m_ref, l_ref, m_blk=None):
    cols = [_lane_chunks(s) for s in s_list]
    if m_blk is None:
        m_blk = _chunk_max([c for cs in cols for c in cs])
    m_prev = m_ref[...]
    m_new = jnp.maximum(m_prev, jnp.max(m_blk, axis=-1, keepdims=True))
    alpha = jnp.exp2(m_prev - m_new)
    probs = [[jnp.exp2(c - m_new) for c in cs] for cs in cols]
    flat = [p for ps in probs for p in ps]
    l_blk = flat[0]
    for p in flat[1:]:
        l_blk = l_blk + p
    l_ref[...] = alpha * l_ref[...] + jnp.sum(l_blk, axis=-1, keepdims=True)
    m_ref[...] = m_new
    return [jnp.concatenate(ps, axis=1).astype(BF16) if len(ps) > 1 else ps[0].astype(BF16)
            for ps in probs], alpha


def _head_out(o1, o2, lam, subln, lam_init):
    d = o1 - lam * o2
    return _rms_rows(d, subln) * (1.0 - lam_init)


def _attn_prompt_kernel(qi_tab, kj_tab, q0_ref, k0_ref, qn_ref, kn_ref, v_ref, lam_ref, subln_ref, o_ref,
                        m_ref, l_ref, acc_ref, sa_ref, sb_ref, ma_ref, mb_ref, p_ref, a_ref, *, lam_init):
    step = pl.program_id(2)
    n_steps = pl.num_programs(2)
    qi = qi_tab[step]
    kj = kj_tab[step]
    nxt = jnp.where(step + 1 == n_steps, 0, step + 1)
    next_masked = qi_tab[nxt] == kj_tab[nxt]
    tq = qn_ref.shape[1]
    tk = kn_ref.shape[1]
    n_sub = m_ref.shape[0]

    def scores(q_ref, k_ref, bufs, masked):
        s_ref, mx_ref = bufs
        for sub in range(n_sub):
            sl = slice(sub * HEAD_DIM, (sub + 1) * HEAD_DIM)
            q = q_ref[0, :, sl]
            k = k_ref[0, :, sl].astype(BF16)
            s = lax.dot_general(q, k, (((1,), (1,)), ((), ())), preferred_element_type=F32)
            if masked:
                r = lax.broadcasted_iota(jnp.int32, (tq, tk), 0)
                c = lax.broadcasted_iota(jnp.int32, (tq, tk), 1)
                s = jnp.where(c <= r, s, NEG_INF)
            s_ref[sub] = s
            mx_ref[sub] = _chunk_max(_lane_chunks(s))

    def accumulate(bufs):
        s_ref, mx_ref = bufs
        for sub in range(n_sub):
            p, alpha = _softmax_step([s_ref[sub]], m_ref.at[sub], l_ref.at[sub], mx_ref[sub])
            p_ref[sub] = p[0]
            a_ref[sub] = alpha
        for sub in range(n_sub):
            vs = slice((sub // 2) * V_DIM, (sub // 2 + 1) * V_DIM)
            alpha = a_ref[sub]
            acc_ref[sub] = (jnp.concatenate([alpha, alpha], axis=1) * acc_ref[sub]
                            + jnp.dot(p_ref[sub], v_ref[0, :, vs].astype(BF16), preferred_element_type=F32))

    @pl.when((pl.program_id(0) == 0) & (pl.program_id(1) == 0) & (step == 0))
    def _():
        scores(q0_ref, k0_ref, (sa_ref, ma_ref), True)

    @pl.when(kj == 0)
    def _():
        m_ref[...] = jnp.full(m_ref.shape, NEG_INF, F32)
        l_ref[...] = jnp.zeros(l_ref.shape, F32)
        acc_ref[...] = jnp.zeros(acc_ref.shape, F32)

    for parity, (cur_ref, nxt_ref) in enumerate((((sa_ref, ma_ref), (sb_ref, mb_ref)),
                                                 ((sb_ref, mb_ref), (sa_ref, ma_ref)))):
        for masked in (False, True):
            @pl.when((step % 2 == parity) & (next_masked == masked))
            def _(cur_ref=cur_ref, nxt_ref=nxt_ref, masked=masked):
                scores(qn_ref, kn_ref, nxt_ref, masked)
                accumulate(cur_ref)

    @pl.when(kj == qi)
    def _():
        lam = _lambda(lam_ref, lam_init)
        for hv in range(n_sub // 2):
            inv1 = 1.0 / l_ref[2 * hv]
            inv2 = 1.0 / l_ref[2 * hv + 1]
            o1 = acc_ref[2 * hv] * jnp.concatenate([inv1, inv1], axis=1)
            o2 = acc_ref[2 * hv + 1] * jnp.concatenate([inv2, inv2], axis=1)
            o_ref[0, :, hv * V_DIM:(hv + 1) * V_DIM] = _head_out(
                o1, o2, lam, subln_ref[...], lam_init).astype(o_ref.dtype)


def _attn_prompt(q, k, v, lam_vecs, subln, l, lam_init, *, tq, heads_per_step):
    B, T, _ = q.shape
    nq = T // tq
    pairs = [(i, j) for i in range(nq) for j in range(i + 1)]
    qi_tab = jnp.asarray([p[0] for p in pairs], jnp.int32)
    kj_tab = jnp.asarray([p[1] for p in pairs], jnp.int32)
    hw = heads_per_step * V_DIM
    n_sub = 2 * heads_per_step
    n_hg = N_HEADS // heads_per_step
    n_steps = len(pairs)
    assert n_steps % 2 == 0

    def next_block(tab):
        def index_map(b, h, s, qt, kt):
            wrap_s = (s + 1 == n_steps).astype(jnp.int32)
            s_n = (s + 1) * (1 - wrap_s)
            wrap_h = ((h + wrap_s) == n_hg).astype(jnp.int32)
            h_n = (h + wrap_s) * (1 - wrap_h)
            b_n = jnp.minimum(b + wrap_h, B - 1)
            return (b_n, (qt if tab == "q" else kt)[s_n], h_n)
        return index_map

    grid_spec = pltpu.PrefetchScalarGridSpec(
        num_scalar_prefetch=2,
        grid=(B, n_hg, n_steps),
        in_specs=[
            pl.BlockSpec((1, tq, hw), lambda b, h, s, qt, kt: (0, 0, 0)),
            pl.BlockSpec((1, tq, hw), lambda b, h, s, qt, kt: (0, 0, 0)),
            pl.BlockSpec((1, tq, hw), next_block("q")),
            pl.BlockSpec((1, tq, hw), next_block("k")),
            pl.BlockSpec((1, tq, hw), lambda b, h, s, qt, kt: (b, kt[s], h)),
            pl.BlockSpec((None, 4, HEAD_DIM), lambda b, h, s, qt, kt: (l, 0, 0)),
            pl.BlockSpec((None, 1, V_DIM), lambda b, h, s, qt, kt: (l, 0, 0)),
        ],
        out_specs=pl.BlockSpec((1, tq, hw), lambda b, h, s, qt, kt: (b, qt[s], h)),
        scratch_shapes=[pltpu.VMEM((n_sub, tq, LANES), F32), pltpu.VMEM((n_sub, tq, LANES), F32),
                        pltpu.VMEM((n_sub, tq, V_DIM), F32),
                        pltpu.VMEM((n_sub, tq, tq), F32), pltpu.VMEM((n_sub, tq, tq), F32),
                        pltpu.VMEM((n_sub, tq, LANES), F32), pltpu.VMEM((n_sub, tq, LANES), F32),
                        pltpu.VMEM((n_sub, tq, tq), BF16), pltpu.VMEM((n_sub, tq, LANES), F32)],
    )
    return pl.pallas_call(
        functools.partial(_attn_prompt_kernel, lam_init=lam_init),
        grid_spec=grid_spec,
        out_shape=jax.ShapeDtypeStruct((B, T, N_HEADS * V_DIM), BF16),
        compiler_params=_params(3),
        name="attn_prompt",
    )(qi_tab, kj_tab, q, k, q, k, v, lam_vecs, subln)


def _attn_sample_kernel(pt_ref, q_ref, *refs, pages_per_step, lam_init):
    k_refs = refs[:pages_per_step]
    v_refs = refs[pages_per_step:3 * pages_per_step]
    kn_ref, vn_ref, lam_ref, subln_ref, o_ref, qbd_ref, m_ref, l_ref, acc_ref = refs[3 * pages_per_step:]
    p_idx = pl.program_id(1)
    tq = q_ref.shape[1]
    rows = N_QK * tq
    d_all = N_QK * HEAD_DIM

    @pl.when(p_idx == 0)
    def _():
        m_ref[...] = jnp.full(m_ref.shape, NEG_INF, F32)
        l_ref[...] = jnp.zeros(l_ref.shape, F32)
        acc_ref[...] = jnp.zeros(acc_ref.shape, F32)
        qt = jnp.concatenate([q_ref[0]] * N_QK, axis=0)
        r = lax.broadcasted_iota(jnp.int32, (rows, d_all), 0)
        c = lax.broadcasted_iota(jnp.int32, (rows, d_all), 1)
        qbd_ref[...] = jnp.where(r // tq == c // HEAD_DIM, qt, 0.0).astype(BF16)

    def update(k_list, v_list, mask):
        qbd = qbd_ref[...]
        s_list = []
        for kp in k_list:
            s = lax.dot_general(qbd, kp, (((1,), (1,)), ((), ())), preferred_element_type=F32)
            if mask is not None:
                s = jnp.where(mask, s, NEG_INF)
            s_list.append(s)
        p_list, alpha = _softmax_step(s_list, m_ref, l_ref)
        for hv in range(N_HEADS):
            rs = slice(hv * 2 * tq, (hv + 1) * 2 * tq)
            pv = None
            for p, v_heads in zip(p_list, v_list):
                t = jnp.dot(p[rs, :], v_heads[hv], preferred_element_type=F32)
                pv = t if pv is None else pv + t
            a = alpha[rs, :]
            acc_ref[rs, :] = jnp.concatenate([a, a], axis=1) * acc_ref[rs, :] + pv

    def head_rows(ref, h, n_heads):
        return ref[pl.ds(h, PAGE_SIZE, stride=n_heads), :].astype(BF16)

    update([jnp.concatenate([head_rows(k, h, N_QK) for h in range(N_QK)], axis=1) for k in k_refs],
           [[jnp.concatenate([head_rows(v_refs[2 * i], hv, N_HEADS),
                              head_rows(v_refs[2 * i + 1], hv, N_HEADS)], axis=1)
             for hv in range(N_HEADS)] for i in range(pages_per_step)], None)

    @pl.when(p_idx == pl.num_programs(1) - 1)
    def _():
        pad = jnp.zeros((PAGE_SIZE - tq, d_all), F32)
        kn = jnp.concatenate([kn_ref[0], pad], axis=0).astype(BF16)
        vn = jnp.concatenate([vn_ref[0], pad], axis=0).astype(BF16)
        r = lax.broadcasted_iota(jnp.int32, (rows, PAGE_SIZE), 0)
        c = lax.broadcasted_iota(jnp.int32, (rows, PAGE_SIZE), 1)
        update([kn], [[vn[:, hv * V_DIM:(hv + 1) * V_DIM] for hv in range(N_HEADS)]], c <= r % tq)
        lam = _lambda(lam_ref, lam_init)
        inv = 1.0 / l_ref[...]
        o = acc_ref[...] * jnp.concatenate([inv, inv], axis=1)
        for hv in range(N_HEADS):
            o1 = o[hv * 2 * tq:hv * 2 * tq + tq, :]
            o2 = o[hv * 2 * tq + tq:(hv + 1) * 2 * tq, :]
            o_ref[0, :, hv * V_DIM:(hv + 1) * V_DIM] = _head_out(
                o1, o2, lam, subln_ref[...], lam_init).astype(o_ref.dtype)


def _attn_sample(q, cache_k, cache_v, page_table, k_new, v_new, lam_vecs, subln, l, lam_init, *,
                 pages_per_step):
    Bd, Tq, D = q.shape
    n_pages = page_table.shape[1]
    pps = pages_per_step

    def page_map(i, half):
        return lambda b, p, pt: (pt[b * n_pages + p * pps + i], 0, half)

    k_specs = [pl.BlockSpec((None, PAGE_SIZE * N_QK, HEAD_DIM), page_map(i, 0)) for i in range(pps)]
    v_specs = [pl.BlockSpec((None, PAGE_SIZE * N_HEADS, LANES), page_map(i, half))
               for i in range(pps) for half in range(V_DIM // LANES)]
    row_spec = pl.BlockSpec((1, Tq, D), lambda b, p, pt: (b, 0, 0))
    grid_spec = pltpu.PrefetchScalarGridSpec(
        num_scalar_prefetch=1,
        grid=(Bd, n_pages // pps),
        in_specs=[row_spec] + k_specs + v_specs + [
            row_spec, row_spec,
            pl.BlockSpec((None, 4, HEAD_DIM), lambda b, p, pt: (l, 0, 0)),
            pl.BlockSpec((None, 1, V_DIM), lambda b, p, pt: (l, 0, 0)),
        ],
        out_specs=row_spec,
        scratch_shapes=[pltpu.VMEM((N_QK * Tq, D), BF16),
                        pltpu.VMEM((N_QK * Tq, LANES), F32), pltpu.VMEM((N_QK * Tq, LANES), F32),
                        pltpu.VMEM((N_QK * Tq, V_DIM), F32)],
    )
    return pl.pallas_call(
        functools.partial(_attn_sample_kernel, pages_per_step=pps, lam_init=lam_init),
        grid_spec=grid_spec,
        out_shape=jax.ShapeDtypeStruct((Bd, Tq, D), BF16),
        compiler_params=_params(2),
        name="attn_sample",
    )(page_table.reshape(-1), q, *([cache_k] * pps), *([cache_v] * (2 * pps)), k_new, v_new, lam_vecs, subln)


def _rope_tables(pos):
    half = HEAD_DIM // 2
    inv = ROPE_THETA ** (-jnp.arange(half, dtype=F32) / half)
    ang = pos.astype(F32)[:, None] * inv[None, :]
    cos = jnp.cos(ang)
    sin = jnp.sin(ang)
    return jnp.concatenate([cos, cos], axis=-1), jnp.concatenate([-sin, sin], axis=-1)


def _trunk(x, pos, paged, p, W, cfg, emit):
    B, T, _ = x.shape
    M = B * T
    h = x.reshape(M, D_MODEL)
    tm, rows, chunk = cfg["tm"], cfg["rows"], cfg["chunk"]
    cos, sin = _rope_tables(pos)
    if T < tm:
        cos = jnp.tile(cos, (tm // T, 1))
        sin = jnp.tile(sin, (tm // T, 1))
    v_rows = []
    k_sh = v_sh = None
    Wb = {name: [] for name in W}

    def keep(name, outs):
        if not emit:
            return outs
        for n, wb in zip(name.split(","), outs[1:]):
            Wb[n].append((wb, 0))
        return outs[0]

    for l in range(DEPTH):
        if l < N_A_LAYERS:
            z = keep("w_in_a", _norm_matmul(h, p["norm_a"][l], W["w_in_a"][l], tm=cfg["tm_in"], tn=cfg["tn_in"],
                                            out_dtype=cfg["z_dtype"], epilogue="gelu", emit=emit))
            ws = p["w_s_a"][l][:, :chunk, :chunk]
            bs_t = p["b_s_a"][l][:, :chunk].T
            if rows > chunk:
                ws = jnp.tile(ws, (1, rows // chunk, rows // chunk))
                bs_t = jnp.tile(bs_t, (rows // chunk, 1))
            h = keep("w_out_a", _gmlp_out(z, ws, bs_t, h, W["w_out_a"][l], tm=cfg["tm_gmlp"], tn=cfg["tn_gmlp"],
                                          rows=rows, chunk=chunk, emit=emit))
            v_rows.append(z[:, D_GATE:])
        else:
            if l == N_A_LAYERS:
                dup = paged is None
                k_sh = keep("w_k", _norm_matmul(h, p["norm_kv"][0], W["w_k"][0], tm=tm, tn=cfg["tn"], out_dtype=F32,
                                                epilogue="rope", rope=(cos, sin), emit=emit, dup=dup))
                v_sh = keep("w_v", _norm_matmul(h, p["norm_kv"][0], W["w_v"][0], tm=tm, tn=cfg["tn"], out_dtype=F32,
                                                emit=emit, dup=dup))
                if dup:
                    (k_sh, k_mxu), (v_sh, v_mxu) = k_sh, v_sh
            j = l - N_A_LAYERS
            lam_init = 0.8 - 0.6 * math.exp(-0.3 * l)
            q = keep("w_q", _norm_matmul(h, p["norm_b"][j], W["w_q"][j], tm=tm, tn=cfg["tn"],
                                         out_dtype=cfg["q_dtype"], epilogue="rope", rope=(cos, sin),
                                         scale=HEAD_DIM ** -0.5 * LOG2E, emit=emit))
            if paged is None:
                o = _attn_prompt(q.reshape(B, T, -1), k_mxu.reshape(B, T, -1), v_mxu.reshape(B, T, -1),
                                 p["lam_vecs"], p["subln_b"], j, lam_init, tq=cfg["tq"],
                                 heads_per_step=cfg["hps"])
            else:
                cache_k, cache_v, page_table = paged
                o = _attn_sample(q.reshape(B, T, -1), cache_k, cache_v, page_table,
                                 k_sh.reshape(B, T, -1), v_sh.reshape(B, T, -1),
                                 p["lam_vecs"], p["subln_b"], j, lam_init, pages_per_step=cfg["pps"])
            h = keep("w_o_b", _matmul_res(o.reshape(M, -1), W["w_o_b"][j], h, tm=tm, tn=cfg["tn"], emit=emit))
        h = keep("w_up,w_down", _ffn(h, p["norm_ffn"][l], W["w_up"][l], W["w_down"][l],
                                     p["norm_f"] if l == DEPTH - 1 else None,
                                     tm=cfg["tm_ffn"], tf=cfg["tf"], emit=emit))
    return h, k_sh, v_sh, v_rows, Wb


def kernel(x_prompt, x_sample, cache_k, cache_v, page_table, norm_a, w_in_a, w_s_a, b_s_a, w_out_a, norm_kv, w_k, w_v, norm_b, w_q, lambda_q1, lambda_k1, lambda_q2, lambda_k2, subln_b, w_o_b, norm_ffn, w_up, w_down, norm_f):
    def layers(a):
        return [(a, l) for l in range(a.shape[0])]

    def gains(g):
        return layers(g.reshape(g.shape[0], 1, g.shape[1]))

    p = dict(norm_a=gains(norm_a), w_s_a=w_s_a, b_s_a=b_s_a, norm_kv=gains(norm_kv[None]),
             norm_b=gains(norm_b),
             lam_vecs=jnp.stack([lambda_q1, lambda_k1, lambda_q2, lambda_k2], axis=1),
             subln_b=subln_b.reshape(subln_b.shape[0], 1, -1), norm_ffn=gains(norm_ffn),
             norm_f=norm_f.reshape(1, -1))
    W = dict(w_in_a=layers(w_in_a), w_out_a=layers(w_out_a), w_k=layers(w_k[None]), w_v=layers(w_v[None]),
             w_q=layers(w_q), w_o_b=layers(w_o_b), w_up=layers(w_up), w_down=layers(w_down))

    B, T, _ = x_prompt.shape
    Bd, Td, _ = x_sample.shape

    Ms = Bd * Td
    cfg_s = dict(tm=Ms, tn=1024, tm_in=Ms, tn_in=1024, tm_gmlp=Ms, tn_gmlp=512, tm_ffn=Ms, tf=512, pps=8,
                 rows=Ms, chunk=Td, z_dtype=F32, q_dtype=F32)
    n_pool = cache_k.shape[0]
    paged = (cache_k.reshape(n_pool, PAGE_SIZE * N_QK, HEAD_DIM),
             cache_v.reshape(n_pool, PAGE_SIZE * N_HEADS, V_DIM), page_table)
    y_s, k_s, v_s, vr_s, Wb = _trunk(x_sample, PAST_LEN + jnp.arange(Td, dtype=jnp.int32), paged, p, W,
                                     cfg_s, True)

    cfg_p = dict(tm=512, tn=2048, tm_in=1024, tn_in=1024, tm_gmlp=512, tn_gmlp=1024, tm_ffn=512, tf=1024,
                 tq=512, hps=2, rows=CHUNK, chunk=CHUNK, z_dtype=BF16, q_dtype=BF16)
    y_p, k_p, v_p, _, _ = _trunk(x_prompt, jnp.arange(T, dtype=jnp.int32), None, p, Wb, cfg_p, False)

    return (y_p.reshape(B, T, D_MODEL), y_s.reshape(Bd, Td, D_MODEL),
            k_p.reshape(B, T, N_QK, HEAD_DIM), v_p.reshape(B, T, N_HEADS, V_DIM),
            k_s.reshape(Bd, Td, N_QK, HEAD_DIM), v_s.reshape(Bd, Td, N_HEADS, V_DIM),
            jnp.stack(vr_s).reshape(N_A_LAYERS, Bd, Td, D_GATE))
```

```python
import functools
import math

import jax
import jax.numpy as jnp
from jax import lax
from jax.experimental import pallas as pl
from jax.experimental.pallas import tpu as pltpu

D_MODEL = 2048
DEPTH = 4
PAST_LEN = 16384
PAGE_SIZE = 128
N_A_LAYERS = DEPTH // 2
CHUNK = 128
D_GATE = 2 * D_MODEL
N_GROUPS_A = 16
GROUP_DIM_A = D_GATE // N_GROUPS_A
HEAD_DIM = 128
N_HEADS = D_MODEL // (2 * HEAD_DIM)
N_QK = 2 * N_HEADS
V_DIM = 2 * HEAD_DIM
D_FF = 4 * D_MODEL
ROPE_THETA = 10000.0
EPS = 1e-5
NEG_INF = -1e30

LANES = 128
VMEM_LIMIT = 56 * 1024 * 1024
LOG2E = math.log2(math.e)

F32 = jnp.float32
BF16 = jnp.bfloat16


def _params(n_axes):
    return pltpu.CompilerParams(dimension_semantics=("arbitrary",) * n_axes,
                                vmem_limit_bytes=VMEM_LIMIT)


def _rms_rows(x, g):
    return x * lax.rsqrt(jnp.mean(x * x, axis=-1, keepdims=True) + EPS) * g


def _lambda(lam_ref, lam_init):
    a = jnp.sum(lam_ref[0:1, :] * lam_ref[1:2, :], axis=-1, keepdims=True)
    b = jnp.sum(lam_ref[2:3, :] * lam_ref[3:4, :], axis=-1, keepdims=True)
    return jnp.exp(a) - jnp.exp(b) + lam_init


def _mxu_weight(w_ref, wb_ref):
    w = w_ref[...].astype(BF16)
    if wb_ref is not None:
        wb_ref[...] = w
    return w


def _norm_matmul_kernel(*refs, epilogue, scale, emit, dup):
    x_ref, g_ref, w_ref = refs[:3]
    refs = refs[3:]
    if epilogue == "rope":
        cos_ref, sin_ref = refs[:2]
        refs = refs[2:]
    out_refs = refs[:2] if dup else refs[:1]
    wb_ref = refs[len(out_refs)] if emit else None
    xn_ref = refs[-1]

    def put(sl, val):
        for o_ref in out_refs:
            o_ref[:, sl] = val.astype(o_ref.dtype)

    @pl.when(pl.program_id(1) == 0)
    def _():
        xn_ref[...] = _rms_rows(x_ref[...], g_ref[...]).astype(BF16)

    y = jnp.dot(xn_ref[...], _mxu_weight(w_ref, wb_ref), preferred_element_type=F32)
    if epilogue == "gelu":
        put(slice(None), 0.5 * y * (1.0 + lax.erf(y * (2.0 ** -0.5))))
    elif epilogue == "rope":
        cos = cos_ref[...]
        sin = sin_ref[...]
        for h in range(y.shape[1] // HEAD_DIM):
            sl = slice(h * HEAD_DIM, (h + 1) * HEAD_DIM)
            yh = y[:, sl]
            oh = yh * cos + pltpu.roll(yh, HEAD_DIM // 2, 1) * sin
            if scale != 1.0:
                oh = oh * scale
            put(sl, oh)
    else:
        put(slice(None), y)


def _weight_copy_out(emit, M, tm, K, N, bk, bn, index_map):
    if not emit:
        return [], []
    assert M == tm
    return [pl.BlockSpec((bk, bn), index_map)], [jax.ShapeDtypeStruct((K, N), BF16)]


def _norm_matmul(x, gl, wl, *, tm, tn, out_dtype, epilogue=None, rope=None, scale=1.0, emit=False,
                 dup=False):
    assert not (emit and dup)
    g, lg = gl
    w, l = wl
    M, K = x.shape
    N = w.shape[2]
    wb_specs, wb_shapes = _weight_copy_out(emit, M, tm, K, N, K, tn, lambda i, j: (0, j))
    in_specs = [
        pl.BlockSpec((tm, K), lambda i, j: (i, 0)),
        pl.BlockSpec((None, 1, K), lambda i, j: (lg, 0, 0)),
        pl.BlockSpec((None, K, tn), lambda i, j: (l, 0, j)),
    ]
    args = [x, g, w]
    if epilogue == "rope":
        cos, sin = rope
        nb = cos.shape[0] // tm
        in_specs += [pl.BlockSpec((tm, HEAD_DIM), lambda i, j: (i % nb, 0))] * 2
        args += [cos, sin]
    outs = pl.pallas_call(
        functools.partial(_norm_matmul_kernel, epilogue=epilogue, scale=scale, emit=emit, dup=dup),
        grid=(M // tm, N // tn),
        in_specs=in_specs,
        out_specs=[pl.BlockSpec((tm, tn), lambda i, j: (i, j))] * (2 if dup else 1) + wb_specs,
        out_shape=[jax.ShapeDtypeStruct((M, N), out_dtype)]
        + ([jax.ShapeDtypeStruct((M, N), BF16)] if dup else []) + wb_shapes,
        scratch_shapes=[pltpu.VMEM((tm, K), BF16)],
        compiler_params=_params(2),
        name="norm_matmul_" + (epilogue or "plain"),
    )(*args)
    if emit:
        return outs[0], outs[1][None]
    return (outs[0], outs[1]) if dup else outs[0]


def _gmlp_out_kernel(u_ref, v_ref, ws_ref, bs_ref, h_ref, w_ref, o_ref, *refs, rows, chunk, emit):
    wb_ref = refs[0] if emit else None
    gated_ref, wt_ref = refs[-2:]
    tm = u_ref.shape[0]

    @pl.when(pl.program_id(1) == 0)
    def _():
        r = lax.broadcasted_iota(jnp.int32, (rows, rows), 0)
        c = lax.broadcasted_iota(jnp.int32, (rows, rows), 1)
        mask = (r // chunk == c // chunk) & (r >= c)
        for g in range(N_GROUPS_A):
            wt_ref[g] = jnp.where(mask, ws_ref[g], 0.0).astype(BF16)

        def mix(ci, carry):
            r0 = pl.multiple_of(ci * rows, rows)
            for g in range(N_GROUPS_A):
                sl = slice(g * GROUP_DIM_A, (g + 1) * GROUP_DIM_A)
                vg = v_ref[pl.ds(r0, rows), sl].astype(BF16)
                s = jnp.dot(wt_ref[g], vg, preferred_element_type=F32) + bs_ref[:, g:g + 1]
                ug = u_ref[pl.ds(r0, rows), sl].astype(F32)
                gated_ref[pl.ds(r0, rows), sl] = (ug * s).astype(BF16)
            return carry

        lax.fori_loop(0, tm // rows, mix, 0)

    o_ref[...] = h_ref[...] + jnp.dot(gated_ref[...], _mxu_weight(w_ref, wb_ref),
                                      preferred_element_type=F32)


def _gmlp_out(z, ws, bs_t, h, wl, *, tm, tn, rows, chunk, emit=False):
    w_out, l = wl
    M = z.shape[0]
    N = w_out.shape[2]
    wb_specs, wb_shapes = _weight_copy_out(emit, M, tm, D_GATE, N, D_GATE, tn, lambda i, j: (0, j))
    outs = pl.pallas_call(
        functools.partial(_gmlp_out_kernel, rows=rows, chunk=chunk, emit=emit),
        grid=(M // tm, N // tn),
        in_specs=[
            pl.BlockSpec((tm, D_GATE), lambda i, j: (i, 0)),
            pl.BlockSpec((tm, D_GATE), lambda i, j: (i, 1)),
            pl.BlockSpec((N_GROUPS_A, rows, rows), lambda i, j: (0, 0, 0)),
            pl.BlockSpec((rows, N_GROUPS_A), lambda i, j: (0, 0)),
            pl.BlockSpec((tm, tn), lambda i, j: (i, j)),
            pl.BlockSpec((None, D_GATE, tn), lambda i, j: (l, 0, j)),
        ],
        out_specs=[pl.BlockSpec((tm, tn), lambda i, j: (i, j))] + wb_specs,
        out_shape=[jax.ShapeDtypeStruct((M, N), F32)] + wb_shapes,
        scratch_shapes=[pltpu.VMEM((tm, D_GATE), BF16),
                        pltpu.VMEM((N_GROUPS_A, rows, rows), BF16)],
        compiler_params=_params(2),
        name="gmlp_out",
    )(z, z, ws, bs_t, h, w_out)
    return (outs[0], outs[1][None]) if emit else outs[0]


def _ffn_kernel(*refs, final_norm, emit):
    x_ref, g_ref, wu_ref, wd_ref = refs[:4]
    refs = refs[4:]
    if final_norm:
        gf_ref = refs[0]
        refs = refs[1:]
    o_ref = refs[0]
    wub_ref, wdb_ref = refs[1:3] if emit else (None, None)
    xn_ref = refs[-1]
    f = pl.program_id(1)

    @pl.when(f == 0)
    def _():
        x = x_ref[...]
        xn_ref[...] = _rms_rows(x, g_ref[...]).astype(BF16)
        o_ref[...] = x

    a = jnp.dot(xn_ref[...], _mxu_weight(wu_ref, wub_ref), preferred_element_type=F32)
    a = jnp.square(jnp.maximum(a, 0.0)).astype(BF16)
    o_ref[...] += jnp.dot(a, _mxu_weight(wd_ref, wdb_ref), preferred_element_type=F32)

    if final_norm:
        @pl.when(f == pl.num_programs(1) - 1)
        def _():
            o_ref[...] = _rms_rows(o_ref[...], gf_ref[...])


def _ffn(x, gl, wul, wdl, g_final, *, tm, tf, emit=False):
    g, l = gl
    w_up, lu = wul
    w_down, ld = wdl
    M, K = x.shape
    F = w_up.shape[2]
    final_norm = g_final is not None
    in_specs = [
        pl.BlockSpec((tm, K), lambda i, f: (i, 0)),
        pl.BlockSpec((None, 1, K), lambda i, f: (l, 0, 0)),
        pl.BlockSpec((None, K, tf), lambda i, f: (lu, 0, f)),
        pl.BlockSpec((None, tf, K), lambda i, f: (ld, f, 0)),
    ]
    ub_specs, ub_shapes = _weight_copy_out(emit, M, tm, K, F, K, tf, lambda i, f: (0, f))
    db_specs, db_shapes = _weight_copy_out(emit, M, tm, F, K, tf, K, lambda i, f: (f, 0))
    args = [x, g, w_up, w_down]
    if final_norm:
        in_specs.append(pl.BlockSpec((1, K), lambda i, f: (0, 0)))
        args.append(g_final)
    outs = pl.pallas_call(
        functools.partial(_ffn_kernel, final_norm=final_norm, emit=emit),
        grid=(M // tm, F // tf),
        in_specs=in_specs,
        out_specs=[pl.BlockSpec((tm, K), lambda i, f: (i, 0))] + ub_specs + db_specs,
        out_shape=[jax.ShapeDtypeStruct((M, K), F32)] + ub_shapes + db_shapes,
        scratch_shapes=[pltpu.VMEM((tm, K), BF16)],
        compiler_params=_params(2),
        name="ffn",
    )(*args)
    return (outs[0], outs[1][None], outs[2][None]) if emit else outs[0]


def _head_rows(ref, h, n_heads):
    return ref[pl.ds(h, PAGE_SIZE, stride=n_heads), :].astype(BF16)


def _page_k(k_ref):
    return jnp.concatenate([_head_rows(k_ref, h, N_QK) for h in range(N_QK)], axis=1)


def _page_v(v_refs):
    return jnp.concatenate([_head_rows(r, hv, N_HEADS) for hv in range(N_HEADS) for r in v_refs], axis=1)


def _matmul_res_kernel(x_ref, w_ref, h_ref, o_ref, wb_ref=None):
    o_ref[...] = h_ref[...] + jnp.dot(x_ref[...], _mxu_weight(w_ref, wb_ref), preferred_element_type=F32)


def _matmul_res(x, wl, h, *, tm, tn, emit=False):
    w, l = wl
    M, K = x.shape
    N = w.shape[2]
    wb_specs, wb_shapes = _weight_copy_out(emit, M, tm, K, N, K, tn, lambda i, j: (0, j))
    outs = pl.pallas_call(
        _matmul_res_kernel,
        grid=(M // tm, N // tn),
        in_specs=[
            pl.BlockSpec((tm, K), lambda i, j: (i, 0)),
            pl.BlockSpec((None, K, tn), lambda i, j: (l, 0, j)),
            pl.BlockSpec((tm, tn), lambda i, j: (i, j)),
        ],
        out_specs=[pl.BlockSpec((tm, tn), lambda i, j: (i, j))] + wb_specs,
        out_shape=[jax.ShapeDtypeStruct((M, N), F32)] + wb_shapes,
        compiler_params=_params(2),
        name="matmul_res",
    )(x, w, h)
    return (outs[0], outs[1][None]) if emit else outs[0]


def _lane_chunks(s):
    return [s[:, c * LANES:(c + 1) * LANES] for c in range(s.shape[1] // LANES)]


def _chunk_max(chunks):
    m = chunks[0]
    for c in chunks[1:]:
        m = jnp.maximum(m, c)
    return m


def _softmax_step(s_list, m_ref, l_ref, m_blk=None):
    cols = [_lane_chunks(s) for s in s_list]
    if m_blk is None:
        m_blk = _chunk_max([c for cs in cols for c in cs])
    m_prev = m_ref[...]
    m_new = jnp.maximum(m_prev, jnp.max(m_blk, axis=-1, keepdims=True))
    alpha = jnp.exp2(m_prev - m_new)
    probs = [[jnp.exp2(c - m_new) for c in cs] for cs in cols]
    flat = [p for ps in probs for p in ps]
    l_blk = flat[0]
    for p in flat[1:]:
        l_blk = l_blk + p
    l_ref[...] = alpha * l_ref[...] + jnp.sum(l_blk, axis=-1, keepdims=True)
    m_ref[...] = m_new
    return [jnp.concatenate(ps, axis=1).astype(BF16) if len(ps) > 1 else ps[0].astype(BF16)
            for ps in probs], alpha


def _head_out(o1, o2, lam, subln, lam_init):
    d = o1 - lam * o2
    return _rms_rows(d, subln) * (1.0 - lam_init)


def _attn_prompt_kernel(qi_tab, kj_tab, q0_ref, k0_ref, qn_ref, kn_ref, v_ref, lam_ref, subln_ref, o_ref,
                        m_ref, l_ref, acc_ref, sa_ref, sb_ref, ma_ref, mb_ref, p_ref, a_ref, *, lam_init):
    step = pl.program_id(2)
    n_steps = pl.num_programs(2)
    qi = qi_tab[step]
    kj = kj_tab[step]
    nxt = jnp.where(step + 1 == n_steps, 0, step + 1)
    next_masked = qi_tab[nxt] == kj_tab[nxt]
    tq = qn_ref.shape[1]
    tk = kn_ref.shape[1]
    n_sub = m_ref.shape[0]

    def scores(q_ref, k_ref, bufs, masked):
        s_ref, mx_ref = bufs
        for sub in range(n_sub):
            sl = slice(sub * HEAD_DIM, (sub + 1) * HEAD_DIM)
            q = q_ref[0, :, sl]
            k = k_ref[0, :, sl].astype(BF16)
            s = lax.dot_general(q, k, (((1,), (1,)), ((), ())), preferred_element_type=F32)
            if masked:
                r = lax.broadcasted_iota(jnp.int32, (tq, tk), 0)
                c = lax.broadcasted_iota(jnp.int32, (tq, tk), 1)
                s = jnp.where(c <= r, s, NEG_INF)
            s_ref[sub] = s
            mx_ref[sub] = _chunk_max(_lane_chunks(s))

    def accumulate(bufs):
        s_ref, mx_ref = bufs
        for sub in range(n_sub):
            p, alpha = _softmax_step([s_ref[sub]], m_ref.at[sub], l_ref.at[sub], mx_ref[sub])
            p_ref[sub] = p[0]
            a_ref[sub] = alpha
        for sub in range(n_sub):
            vs = slice((sub // 2) * V_DIM, (sub // 2 + 1) * V_DIM)
            alpha = a_ref[sub]
            acc_ref[sub] = (jnp.concatenate([alpha, alpha], axis=1) * acc_ref[sub]
                            + jnp.dot(p_ref[sub], v_ref[0, :, vs].astype(BF16), preferred_element_type=F32))

    @pl.when((pl.program_id(0) == 0) & (pl.program_id(1) == 0) & (step == 0))
    def _():
        scores(q0_ref, k0_ref, (sa_ref, ma_ref), True)

    @pl.when(kj == 0)
    def _():
        m_ref[...] = jnp.full(m_ref.shape, NEG_INF, F32)
        l_ref[...] = jnp.zeros(l_ref.shape, F32)
        acc_ref[...] = jnp.zeros(acc_ref.shape, F32)

    for parity, (cur_ref, nxt_ref) in enumerate((((sa_ref, ma_ref), (sb_ref, mb_ref)),
                                                 ((sb_ref, mb_ref), (sa_ref, ma_ref)))):
        for masked in (False, True):
            @pl.when((step % 2 == parity) & (next_masked == masked))
            def _(cur_ref=cur_ref, nxt_ref=nxt_ref, masked=masked):
                scores(qn_ref, kn_ref, nxt_ref, masked)
                accumulate(cur_ref)

    @pl.when(kj == qi)
    def _():
        lam = _lambda(lam_ref, lam_init)
        for hv in range(n_sub // 2):
            inv1 = 1.0 / l_ref[2 * hv]
            inv2 = 1.0 / l_ref[2 * hv + 1]
            o1 = acc_ref[2 * hv] * jnp.concatenate([inv1, inv1], axis=1)
            o2 = acc_ref[2 * hv + 1] * jnp.concatenate([inv2, inv2], axis=1)
            o_ref[0, :, hv * V_DIM:(hv + 1) * V_DIM] = _head_out(
                o1, o2, lam, subln_ref[...], lam_init).astype(o_ref.dtype)


def _attn_prompt(q, k, v, lam_vecs, subln, l, lam_init, *, tq, heads_per_step):
    B, T, _ = q.shape
    nq = T // tq
    pairs = [(i, j) for i in range(nq) for j in range(i + 1)]
    qi_tab = jnp.asarray([p[0] for p in pairs], jnp.int32)
    kj_tab = jnp.asarray([p[1] for p in pairs], jnp.int32)
    hw = heads_per_step * V_DIM
    n_sub = 2 * heads_per_step
    n_hg = N_HEADS // heads_per_step
    n_steps = len(pairs)
    assert n_steps % 2 == 0

    def next_block(tab):
        def index_map(b, h, s, qt, kt):
            wrap_s = (s + 1 == n_steps).astype(jnp.int32)
            s_n = (s + 1) * (1 - wrap_s)
            wrap_h = ((h + wrap_s) == n_hg).astype(jnp.int32)
            h_n = (h + wrap_s) * (1 - wrap_h)
            b_n = jnp.minimum(b + wrap_h, B - 1)
            return (b_n, (qt if tab == "q" else kt)[s_n], h_n)
        return index_map

    grid_spec = pltpu.PrefetchScalarGridSpec(
        num_scalar_prefetch=2,
        grid=(B, n_hg, n_steps),
        in_specs=[
            pl.BlockSpec((1, tq, hw), lambda b, h, s, qt, kt: (0, 0, 0)),
            pl.BlockSpec((1, tq, hw), lambda b, h, s, qt, kt: (0, 0, 0)),
            pl.BlockSpec((1, tq, hw), next_block("q")),
            pl.BlockSpec((1, tq, hw), next_block("k")),
            pl.BlockSpec((1, tq, hw), lambda b, h, s, qt, kt: (b, kt[s], h)),
            pl.BlockSpec((None, 4, HEAD_DIM), lambda b, h, s, qt, kt: (l, 0, 0)),
            pl.BlockSpec((None, 1, V_DIM), lambda b, h, s, qt, kt: (l, 0, 0)),
        ],
        out_specs=pl.BlockSpec((1, tq, hw), lambda b, h, s, qt, kt: (b, qt[s], h)),
        scratch_shapes=[pltpu.VMEM((n_sub, tq, LANES), F32), pltpu.VMEM((n_sub, tq, LANES), F32),
                        pltpu.VMEM((n_sub, tq, V_DIM), F32),
                        pltpu.VMEM((n_sub, tq, tq), F32), pltpu.VMEM((n_sub, tq, tq), F32),
                        pltpu.VMEM((n_sub, tq, LANES), F32), pltpu.VMEM((n_sub, tq, LANES), F32),
                        pltpu.VMEM((n_sub, tq, tq), BF16), pltpu.VMEM((n_sub, tq, LANES), F32)],
    )
    return pl.pallas_call(
        functools.partial(_attn_prompt_kernel, lam_init=lam_init),
        grid_spec=grid_spec,
        out_shape=jax.ShapeDtypeStruct((B, T, N_HEADS * V_DIM), BF16),
        compiler_params=_params(3),
        name="attn_prompt",
    )(qi_tab, kj_tab, q, k, q, k, v, lam_vecs, subln)


def _attn_sample_step(k_tiles, v_tiles, q_ref, kn_ref, vn_ref, lam_ref, subln_ref, o_ref,
                      qbd_ref, m_ref, l_ref, acc_ref, lam_init):
    p_idx = pl.program_id(1)
    tq = q_ref.shape[1]
    rows = N_QK * tq
    d_all = N_QK * HEAD_DIM

    @pl.when(p_idx == 0)
    def _():
        m_ref[...] = jnp.full(m_ref.shape, NEG_INF, F32)
        l_ref[...] = jnp.zeros(l_ref.shape, F32)
        acc_ref[...] = jnp.zeros(acc_ref.shape, F32)
        qt = jnp.concatenate([q_ref[0]] * N_QK, axis=0)
        r = lax.broadcasted_iota(jnp.int32, (rows, d_all), 0)
        c = lax.broadcasted_iota(jnp.int32, (rows, d_all), 1)
        qbd_ref[...] = jnp.where(r // tq == c // HEAD_DIM, qt, 0.0).astype(BF16)

    def update(k_list, v_list, mask):
        qbd = qbd_ref[...]
        s_list = []
        for kp in k_list:
            s = lax.dot_general(qbd, kp, (((1,), (1,)), ((), ())), preferred_element_type=F32)
            if mask is not None:
                s = jnp.where(mask, s, NEG_INF)
            s_list.append(s)
        p_list, alpha = _softmax_step(s_list, m_ref, l_ref)
        for hv in range(N_HEADS):
            rs = slice(hv * 2 * tq, (hv + 1) * 2 * tq)
            pv = None
            for p, v_heads in zip(p_list, v_list):
                t = jnp.dot(p[rs, :], v_heads[hv], preferred_element_type=F32)
                pv = t if pv is None else pv + t
            a = alpha[rs, :]
            acc_ref[rs, :] = jnp.concatenate([a, a], axis=1) * acc_ref[rs, :] + pv

    update(k_tiles, [[v[:, hv * V_DIM:(hv + 1) * V_DIM] for hv in range(N_HEADS)] for v in v_tiles], None)

    @pl.when(p_idx == pl.num_programs(1) - 1)
    def _():
        pad = jnp.zeros((PAGE_SIZE - tq, d_all), F32)
        kn = jnp.concatenate([kn_ref[0], pad], axis=0).astype(BF16)
        vn = jnp.concatenate([vn_ref[0], pad], axis=0).astype(BF16)
        r = lax.broadcasted_iota(jnp.int32, (rows, PAGE_SIZE), 0)
        c = lax.broadcasted_iota(jnp.int32, (rows, PAGE_SIZE), 1)
        update([kn], [[vn[:, hv * V_DIM:(hv + 1) * V_DIM] for hv in range(N_HEADS)]], c <= r % tq)
        lam = _lambda(lam_ref, lam_init)
        inv = 1.0 / l_ref[...]
        o = acc_ref[...] * jnp.concatenate([inv, inv], axis=1)
        for hv in range(N_HEADS):
            o1 = o[hv * 2 * tq:hv * 2 * tq + tq, :]
            o2 = o[hv * 2 * tq + tq:(hv + 1) * 2 * tq, :]
            o_ref[0, :, hv * V_DIM:(hv + 1) * V_DIM] = _head_out(
                o1, o2, lam, subln_ref[...], lam_init).astype(o_ref.dtype)


def _attn_sample_paged_kernel(pt_ref, q_ref, *refs, pages, lam_init):
    k_refs = refs[:pages]
    v_refs = refs[pages:3 * pages]
    kn_ref, vn_ref, lam_ref, subln_ref, o_ref, kg_ref, vg_ref = refs[3 * pages:3 * pages + 7]
    ks = [_page_k(k) for k in k_refs]
    vs = [_page_v(v_refs[2 * i:2 * i + 2]) for i in range(pages)]
    for i in range(pages):
        kg_ref[i] = ks[i]
        vg_ref[i] = vs[i]
    _attn_sample_step(ks, vs, q_ref, kn_ref, vn_ref, lam_ref, subln_ref, o_ref, *refs[3 * pages + 7:],
                      lam_init)


def _attn_sample_gathered_kernel(q_ref, kg_ref, vg_ref, kn_ref, vn_ref, lam_ref, subln_ref, o_ref,
                                 *scratch, lam_init):
    n_keys = kg_ref.shape[0] * PAGE_SIZE
    d_all = kg_ref.shape[2]
    _attn_sample_step([kg_ref[...].reshape(n_keys, d_all)], [vg_ref[...].reshape(n_keys, d_all)], q_ref,
                      kn_ref, vn_ref, lam_ref, subln_ref, o_ref, *scratch, lam_init)


def _attn_sample_scratch(Tq, D):
    return [pltpu.VMEM((N_QK * Tq, D), BF16),
            pltpu.VMEM((N_QK * Tq, LANES), F32), pltpu.VMEM((N_QK * Tq, LANES), F32),
            pltpu.VMEM((N_QK * Tq, V_DIM), F32)]


def _attn_sample_paged(q, cache_k, cache_v, page_table, k_new, v_new, lam_vecs, subln, l, lam_init, *,
                       pages_per_step):
    Bd, Tq, D = q.shape
    n_pages = page_table.shape[1]
    pps = pages_per_step
    n_steps = n_pages // pps

    def page_map(i, half):
        return lambda b, p, pt: (pt[b * n_pages + p * pps + i], 0, half)

    k_specs = [pl.BlockSpec((None, PAGE_SIZE * N_QK, HEAD_DIM), page_map(i, 0)) for i in range(pps)]
    v_specs = [pl.BlockSpec((None, PAGE_SIZE * N_HEADS, LANES), page_map(i, half))
               for i in range(pps) for half in range(V_DIM // LANES)]
    row_spec = pl.BlockSpec((1, Tq, D), lambda b, p, pt: (b, 0, 0))
    past_spec = pl.BlockSpec((pps, PAGE_SIZE, D), lambda b, p, pt: (b * n_steps + p, 0, 0))
    past_shape = jax.ShapeDtypeStruct((Bd * n_pages, PAGE_SIZE, D), BF16)
    grid_spec = pltpu.PrefetchScalarGridSpec(
        num_scalar_prefetch=1,
        grid=(Bd, n_steps),
        in_specs=[row_spec] + k_specs + v_specs + [
            row_spec, row_spec,
            pl.BlockSpec((None, 4, HEAD_DIM), lambda b, p, pt: (l, 0, 0)),
            pl.BlockSpec((None, 1, V_DIM), lambda b, p, pt: (l, 0, 0)),
        ],
        out_specs=[row_spec, past_spec, past_spec],
        scratch_shapes=_attn_sample_scratch(Tq, D),
    )
    o, kg, vg = pl.pallas_call(
        functools.partial(_attn_sample_paged_kernel, pages=pps, lam_init=lam_init),
        grid_spec=grid_spec,
        out_shape=[jax.ShapeDtypeStruct((Bd, Tq, D), BF16), past_shape, past_shape],
        compiler_params=_params(2),
        name="attn_sample_paged",
    )(page_table.reshape(-1), q, *([cache_k] * pps), *([cache_v] * (2 * pps)), k_new, v_new, lam_vecs, subln)
    return o, (kg, vg)


def _attn_sample_gathered(q, past, k_new, v_new, lam_vecs, subln, l, lam_init, *, pages_per_step):
    kg, vg = past
    Bd, Tq, D = q.shape
    pps = pages_per_step
    n_steps = kg.shape[0] // (Bd * pps)
    row_spec = pl.BlockSpec((1, Tq, D), lambda b, p: (b, 0, 0))
    past_spec = pl.BlockSpec((pps, PAGE_SIZE, D), lambda b, p: (b * n_steps + p, 0, 0))
    return pl.pallas_call(
        functools.partial(_attn_sample_gathered_kernel, lam_init=lam_init),
        grid=(Bd, n_steps),
        in_specs=[row_spec, past_spec, past_spec, row_spec, row_spec,
                  pl.BlockSpec((None, 4, HEAD_DIM), lambda b, p: (l, 0, 0)),
                  pl.BlockSpec((None, 1, V_DIM), lambda b, p: (l, 0, 0))],
        out_specs=row_spec,
        out_shape=jax.ShapeDtypeStruct((Bd, Tq, D), BF16),
        scratch_shapes=_attn_sample_scratch(Tq, D),
        compiler_params=_params(2),
        name="attn_sample_gathered",
    )(q, kg, vg, k_new, v_new, lam_vecs, subln)


def _rope_tables(pos):
    half = HEAD_DIM // 2
    inv = ROPE_THETA ** (-jnp.arange(half, dtype=F32) / half)
    ang = pos.astype(F32)[:, None] * inv[None, :]
    cos = jnp.cos(ang)
    sin = jnp.sin(ang)
    return jnp.concatenate([cos, cos], axis=-1), jnp.concatenate([-sin, sin], axis=-1)


class _Group:
    def __init__(self, x, pos, cfg):
        self.B, self.T, _ = x.shape
        self.h = x.reshape(self.B * self.T, D_MODEL)
        self.cfg = cfg
        cos, sin = _rope_tables(pos)
        if self.T < cfg["tm"]:
            cos = jnp.tile(cos, (cfg["tm"] // self.T, 1))
            sin = jnp.tile(sin, (cfg["tm"] // self.T, 1))
        self.rope = (cos, sin)
        self.v_rows = []
        self.k_sh = self.v_sh = self.k_mxu = self.v_mxu = None
        self.past = None


def _layer(G, l, p, W, Wb=None, paged=None):
    cfg, B, T, h = G.cfg, G.B, G.T, G.h
    M = B * T
    tm, rows, chunk = cfg["tm"], cfg["rows"], cfg["chunk"]
    cos, sin = G.rope
    emit = Wb is not None

    def keep(name, outs):
        if not emit:
            return outs
        for n, wb in zip(name.split(","), outs[1:]):
            Wb[n].append((wb, 0))
        return outs[0]

    if l < N_A_LAYERS:
        z = keep("w_in_a", _norm_matmul(h, p["norm_a"][l], W["w_in_a"][l], tm=cfg["tm_in"], tn=cfg["tn_in"],
                                        out_dtype=cfg["z_dtype"], epilogue="gelu", emit=emit))
        ws = p["w_s_a"][l][:, :chunk, :chunk]
        bs_t = p["b_s_a"][l][:, :chunk].T
        if rows > chunk:
            ws = jnp.tile(ws, (1, rows // chunk, rows // chunk))
            bs_t = jnp.tile(bs_t, (rows // chunk, 1))
        h = keep("w_out_a", _gmlp_out(z, ws, bs_t, h, W["w_out_a"][l], tm=cfg["tm_gmlp"], tn=cfg["tn_gmlp"],
                                      rows=rows, chunk=chunk, emit=emit))
        G.v_rows.append(z[:, D_GATE:])
    else:
        if l == N_A_LAYERS:
            dup = paged is None
            k_sh = keep("w_k", _norm_matmul(h, p["norm_kv"][0], W["w_k"][0], tm=tm, tn=cfg["tn"], out_dtype=F32,
                                            epilogue="rope", rope=(cos, sin), emit=emit, dup=dup))
            v_sh = keep("w_v", _norm_matmul(h, p["norm_kv"][0], W["w_v"][0], tm=tm, tn=cfg["tn"], out_dtype=F32,
                                            emit=emit, dup=dup))
            if dup:
                (k_sh, G.k_mxu), (v_sh, G.v_mxu) = k_sh, v_sh
            G.k_sh, G.v_sh = k_sh, v_sh
        j = l - N_A_LAYERS
        lam_init = 0.8 - 0.6 * math.exp(-0.3 * l)
        q = keep("w_q", _norm_matmul(h, p["norm_b"][j], W["w_q"][j], tm=tm, tn=cfg["tn"],
                                     out_dtype=cfg["q_dtype"], epilogue="rope", rope=(cos, sin),
                                     scale=HEAD_DIM ** -0.5 * LOG2E, emit=emit))
        if paged is None:
            o = _attn_prompt(q.reshape(B, T, -1), G.k_mxu.reshape(B, T, -1), G.v_mxu.reshape(B, T, -1),
                             p["lam_vecs"], p["subln_b"], j, lam_init, tq=cfg["tq"],
                             heads_per_step=cfg["hps"])
        elif G.past is None:
            o, G.past = _attn_sample_paged(q.reshape(B, T, -1), *paged,
                                           G.k_sh.reshape(B, T, -1), G.v_sh.reshape(B, T, -1),
                                           p["lam_vecs"], p["subln_b"], j, lam_init,
                                           pages_per_step=cfg["pps"])
        else:
            o = _attn_sample_gathered(q.reshape(B, T, -1), G.past,
                                      G.k_sh.reshape(B, T, -1), G.v_sh.reshape(B, T, -1),
                                      p["lam_vecs"], p["subln_b"], j, lam_init,
                                      pages_per_step=cfg["pps_gathered"])
        h = keep("w_o_b", _matmul_res(o.reshape(M, -1), W["w_o_b"][j], h, tm=tm, tn=cfg["tn"], emit=emit))
    G.h = keep("w_up,w_down", _ffn(h, p["norm_ffn"][l], W["w_up"][l], W["w_down"][l],
                                   p["norm_f"] if l == DEPTH - 1 else None,
                                   tm=cfg["tm_ffn"], tf=cfg["tf"], emit=emit))


def kernel(x_prompt, x_sample, cache_k, cache_v, page_table, norm_a, w_in_a, w_s_a, b_s_a, w_out_a, norm_kv, w_k, w_v, norm_b, w_q, lambda_q1, lambda_k1, lambda_q2, lambda_k2, subln_b, w_o_b, norm_ffn, w_up, w_down, norm_f):
    def layers(a):
        return [(a, l) for l in range(a.shape[0])]

    def gains(g):
        return layers(g.reshape(g.shape[0], 1, g.shape[1]))

    p = dict(norm_a=gains(norm_a), w_s_a=w_s_a, b_s_a=b_s_a, norm_kv=gains(norm_kv[None]),
             norm_b=gains(norm_b),
             lam_vecs=jnp.stack([lambda_q1, lambda_k1, lambda_q2, lambda_k2], axis=1),
             subln_b=subln_b.reshape(subln_b.shape[0], 1, -1), norm_ffn=gains(norm_ffn),
             norm_f=norm_f.reshape(1, -1))
    W = dict(w_in_a=layers(w_in_a), w_out_a=layers(w_out_a), w_k=layers(w_k[None]), w_v=layers(w_v[None]),
             w_q=layers(w_q), w_o_b=layers(w_o_b), w_up=layers(w_up), w_down=layers(w_down))

    B, T, _ = x_prompt.shape
    Bd, Td, _ = x_sample.shape

    Ms = Bd * Td
    cfg_s = dict(tm=Ms, tn=1024, tm_in=Ms, tn_in=1024, tm_gmlp=Ms, tn_gmlp=512, tm_ffn=Ms, tf=512,
                 pps=4, pps_gathered=16, rows=Ms, chunk=Td, z_dtype=F32, q_dtype=F32)
    cfg_p = dict(tm=512, tn=2048, tm_in=1024, tn_in=1024, tm_gmlp=512, tn_gmlp=1024, tm_ffn=512, tf=1024,
                 tq=512, hps=4, rows=CHUNK, chunk=CHUNK, z_dtype=BF16, q_dtype=BF16)
    S = _Group(x_sample, PAST_LEN + jnp.arange(Td, dtype=jnp.int32), cfg_s)
    P = _Group(x_prompt, jnp.arange(T, dtype=jnp.int32), cfg_p)

    n_pool = cache_k.shape[0]
    paged = (cache_k.reshape(n_pool, PAGE_SIZE * N_QK, HEAD_DIM),
             cache_v.reshape(n_pool, PAGE_SIZE * N_HEADS, V_DIM), page_table)
    Wb = {name: [] for name in W}
    for l in range(DEPTH):
        _layer(S, l, p, W, Wb=Wb, paged=paged)
    for l in range(DEPTH):
        _layer(P, l, p, Wb)

    return (P.h.reshape(B, T, D_MODEL), S.h.reshape(Bd, Td, D_MODEL),
            P.k_sh.reshape(B, T, N_QK, HEAD_DIM), P.v_sh.reshape(B, T, N_HEADS, V_DIM),
            S.k_sh.reshape(Bd, Td, N_QK, HEAD_DIM), S.v_sh.reshape(Bd, Td, N_HEADS, V_DIM),
            jnp.stack(S.v_rows).reshape(N_A_LAYERS, Bd, Td, D_GATE))
```

```python
import functools
import math

import jax
import jax.numpy as jnp
from jax import lax
from jax.experimental import pallas as pl
from jax.experimental.pallas import tpu as pltpu

D_MODEL = 2048
DEPTH = 4
PAST_LEN = 16384
PAGE_SIZE = 128
N_A_LAYERS = DEPTH // 2
CHUNK = 128
D_GATE = 2 * D_MODEL
N_GROUPS_A = 16
GROUP_DIM_A = D_GATE // N_GROUPS_A
HEAD_DIM = 128
N_HEADS = D_MODEL // (2 * HEAD_DIM)
N_QK = 2 * N_HEADS
V_DIM = 2 * HEAD_DIM
D_FF = 4 * D_MODEL
ROPE_THETA = 10000.0
EPS = 1e-5
NEG_INF = -1e30

LANES = 128
VMEM_LIMIT = 56 * 1024 * 1024
LOG2E = math.log2(math.e)

F32 = jnp.float32
BF16 = jnp.bfloat16


def _params(n_axes):
    return pltpu.CompilerParams(dimension_semantics=("arbitrary",) * n_axes,
                                vmem_limit_bytes=VMEM_LIMIT)


def _rms_rows(x, g):
    return x * lax.rsqrt(jnp.mean(x * x, axis=-1, keepdims=True) + EPS) * g


def _lambda(lam_ref, lam_init):
    a = jnp.sum(lam_ref[0:1, :] * lam_ref[1:2, :], axis=-1, keepdims=True)
    b = jnp.sum(lam_ref[2:3, :] * lam_ref[3:4, :], axis=-1, keepdims=True)
    return jnp.exp(a) - jnp.exp(b) + lam_init


def _mxu_weight(w_ref, wb_ref):
    w = w_ref[...].astype(BF16)
    if wb_ref is not None:
        wb_ref[...] = w
    return w


def _norm_matmul_kernel(*refs, epilogue, scale, emit, dup):
    x_ref, g_ref, w_ref = refs[:3]
    refs = refs[3:]
    if epilogue == "rope":
        cos_ref, sin_ref = refs[:2]
        refs = refs[2:]
    out_refs = refs[:2] if dup else refs[:1]
    wb_ref = refs[len(out_refs)] if emit else None
    xn_ref = refs[-1]

    def put(sl, val):
        for o_ref in out_refs:
            o_ref[:, sl] = val.astype(o_ref.dtype)

    @pl.when(pl.program_id(1) == 0)
    def _():
        xn_ref[...] = _rms_rows(x_ref[...], g_ref[...]).astype(BF16)

    y = jnp.dot(xn_ref[...], _mxu_weight(w_ref, wb_ref), preferred_element_type=F32)
    if epilogue == "gelu":
        put(slice(None), 0.5 * y * (1.0 + lax.erf(y * (2.0 ** -0.5))))
    elif epilogue == "rope":
        cos = cos_ref[...]
        sin = sin_ref[...]
        for h in range(y.shape[1] // HEAD_DIM):
            sl = slice(h * HEAD_DIM, (h + 1) * HEAD_DIM)
            yh = y[:, sl]
            oh = yh * cos + pltpu.roll(yh, HEAD_DIM // 2, 1) * sin
            if scale != 1.0:
                oh = oh * scale
            put(sl, oh)
    else:
        put(slice(None), y)


def _weight_copy_out(emit, M, tm, K, N, bk, bn, index_map):
    if not emit:
        return [], []
    assert M == tm
    return [pl.BlockSpec((bk, bn), index_map)], [jax.ShapeDtypeStruct((K, N), BF16)]


def _norm_matmul(x, gl, wl, *, tm, tn, out_dtype, epilogue=None, rope=None, scale=1.0, emit=False,
                 dup=False):
    assert not (emit and dup)
    g, lg = gl
    w, l = wl
    M, K = x.shape
    N = w.shape[2]
    wb_specs, wb_shapes = _weight_copy_out(emit, M, tm, K, N, K, tn, lambda i, j: (0, j))
    in_specs = [
        pl.BlockSpec((tm, K), lambda i, j: (i, 0)),
        pl.BlockSpec((None, 1, K), lambda i, j: (lg, 0, 0)),
        pl.BlockSpec((None, K, tn), lambda i, j: (l, 0, j)),
    ]
    args = [x, g, w]
    if epilogue == "rope":
        cos, sin = rope
        nb = cos.shape[0] // tm
        in_specs += [pl.BlockSpec((tm, HEAD_DIM), lambda i, j: (i % nb, 0))] * 2
        args += [cos, sin]
    outs = pl.pallas_call(
        functools.partial(_norm_matmul_kernel, epilogue=epilogue, scale=scale, emit=emit, dup=dup),
        grid=(M // tm, N // tn),
        in_specs=in_specs,
        out_specs=[pl.BlockSpec((tm, tn), lambda i, j: (i, j))] * (2 if dup else 1) + wb_specs,
        out_shape=[jax.ShapeDtypeStruct((M, N), out_dtype)]
        + ([jax.ShapeDtypeStruct((M, N), BF16)] if dup else []) + wb_shapes,
        scratch_shapes=[pltpu.VMEM((tm, K), BF16)],
        compiler_params=_params(2),
        name="norm_matmul_" + (epilogue or "plain"),
    )(*args)
    if emit:
        return outs[0], outs[1][None]
    return (outs[0], outs[1]) if dup else outs[0]


def _gmlp_out_kernel(u_ref, v_ref, ws_ref, bs_ref, h_ref, w_ref, o_ref, *refs, rows, chunk, emit):
    wb_ref = refs[0] if emit else None
    gated_ref, wt_ref = refs[-2:]
    tm = u_ref.shape[0]

    @pl.when(pl.program_id(1) == 0)
    def _():
        r = lax.broadcasted_iota(jnp.int32, (rows, rows), 0)
        c = lax.broadcasted_iota(jnp.int32, (rows, rows), 1)
        mask = (r // chunk == c // chunk) & (r >= c)
        for g in range(N_GROUPS_A):
            wt_ref[g] = jnp.where(mask, ws_ref[g], 0.0).astype(BF16)

        def mix(ci, carry):
            r0 = pl.multiple_of(ci * rows, rows)
            for g in range(N_GROUPS_A):
                sl = slice(g * GROUP_DIM_A, (g + 1) * GROUP_DIM_A)
                vg = v_ref[pl.ds(r0, rows), sl].astype(BF16)
                s = jnp.dot(wt_ref[g], vg, preferred_element_type=F32) + bs_ref[:, g:g + 1]
                ug = u_ref[pl.ds(r0, rows), sl].astype(F32)
                gated_ref[pl.ds(r0, rows), sl] = (ug * s).astype(BF16)
            return carry

        lax.fori_loop(0, tm // rows, mix, 0)

    o_ref[...] = h_ref[...] + jnp.dot(gated_ref[...], _mxu_weight(w_ref, wb_ref),
                                      preferred_element_type=F32)


def _gmlp_out(z, ws, bs_t, h, wl, *, tm, tn, rows, chunk, emit=False):
    w_out, l = wl
    M = z.shape[0]
    N = w_out.shape[2]
    wb_specs, wb_shapes = _weight_copy_out(emit, M, tm, D_GATE, N, D_GATE, tn, lambda i, j: (0, j))
    outs = pl.pallas_call(
        functools.partial(_gmlp_out_kernel, rows=rows, chunk=chunk, emit=emit),
        grid=(M // tm, N // tn),
        in_specs=[
            pl.BlockSpec((tm, D_GATE), lambda i, j: (i, 0)),
            pl.BlockSpec((tm, D_GATE), lambda i, j: (i, 1)),
            pl.BlockSpec((N_GROUPS_A, rows, rows), lambda i, j: (0, 0, 0)),
            pl.BlockSpec((rows, N_GROUPS_A), lambda i, j: (0, 0)),
            pl.BlockSpec((tm, tn), lambda i, j: (i, j)),
            pl.BlockSpec((None, D_GATE, tn), lambda i, j: (l, 0, j)),
        ],
        out_specs=[pl.BlockSpec((tm, tn), lambda i, j: (i, j))] + wb_specs,
        out_shape=[jax.ShapeDtypeStruct((M, N), F32)] + wb_shapes,
        scratch_shapes=[pltpu.VMEM((tm, D_GATE), BF16),
                        pltpu.VMEM((N_GROUPS_A, rows, rows), BF16)],
        compiler_params=_params(2),
        name="gmlp_out",
    )(z, z, ws, bs_t, h, w_out)
    return (outs[0], outs[1][None]) if emit else outs[0]


def _ffn_kernel(*refs, final_norm, emit):
    x_ref, g_ref, wu_ref, wd_ref = refs[:4]
    refs = refs[4:]
    if final_norm:
        gf_ref = refs[0]
        refs = refs[1:]
    o_ref = refs[0]
    wub_ref, wdb_ref = refs[1:3] if emit else (None, None)
    xn_ref = refs[-1]
    f = pl.program_id(1)

    @pl.when(f == 0)
    def _():
        x = x_ref[...]
        xn_ref[...] = _rms_rows(x, g_ref[...]).astype(BF16)
        o_ref[...] = x

    a = jnp.dot(xn_ref[...], _mxu_weight(wu_ref, wub_ref), preferred_element_type=F32)
    a = jnp.square(jnp.maximum(a, 0.0)).astype(BF16)
    o_ref[...] += jnp.dot(a, _mxu_weight(wd_ref, wdb_ref), preferred_element_type=F32)

    if final_norm:
        @pl.when(f == pl.num_programs(1) - 1)
        def _():
            o_ref[...] = _rms_rows(o_ref[...], gf_ref[...])


def _ffn(x, gl, wul, wdl, g_final, *, tm, tf, emit=False):
    g, l = gl
    w_up, lu = wul
    w_down, ld = wdl
    M, K = x.shape
    F = w_up.shape[2]
    final_norm = g_final is not None
    in_specs = [
        pl.BlockSpec((tm, K), lambda i, f: (i, 0)),
        pl.BlockSpec((None, 1, K), lambda i, f: (l, 0, 0)),
        pl.BlockSpec((None, K, tf), lambda i, f: (lu, 0, f)),
        pl.BlockSpec((None, tf, K), lambda i, f: (ld, f, 0)),
    ]
    ub_specs, ub_shapes = _weight_copy_out(emit, M, tm, K, F, K, tf, lambda i, f: (0, f))
    db_specs, db_shapes = _weight_copy_out(emit, M, tm, F, K, tf, K, lambda i, f: (f, 0))
    args = [x, g, w_up, w_down]
    if final_norm:
        in_specs.append(pl.BlockSpec((1, K), lambda i, f: (0, 0)))
        args.append(g_final)
    outs = pl.pallas_call(
        functools.partial(_ffn_kernel, final_norm=final_norm, emit=emit),
        grid=(M // tm, F // tf),
        in_specs=in_specs,
        out_specs=[pl.BlockSpec((tm, K), lambda i, f: (i, 0))] + ub_specs + db_specs,
        out_shape=[jax.ShapeDtypeStruct((M, K), F32)] + ub_shapes + db_shapes,
        scratch_shapes=[pltpu.VMEM((tm, K), BF16)],
        compiler_params=_params(2),
        name="ffn",
    )(*args)
    return (outs[0], outs[1][None], outs[2][None]) if emit else outs[0]


def _head_rows(ref, h, n_heads):
    return ref[pl.ds(h, PAGE_SIZE, stride=n_heads), :].astype(BF16)


def _page_k(k_ref):
    return jnp.concatenate([_head_rows(k_ref, h, N_QK) for h in range(N_QK)], axis=1)


def _matmul_res_kernel(x_ref, w_ref, h_ref, o_ref, wb_ref=None):
    o_ref[...] = h_ref[...] + jnp.dot(x_ref[...], _mxu_weight(w_ref, wb_ref), preferred_element_type=F32)


def _matmul_res(x, wl, h, *, tm, tn, emit=False):
    w, l = wl
    M, K = x.shape
    N = w.shape[2]
    wb_specs, wb_shapes = _weight_copy_out(emit, M, tm, K, N, K, tn, lambda i, j: (0, j))
    outs = pl.pallas_call(
        _matmul_res_kernel,
        grid=(M // tm, N // tn),
        in_specs=[
            pl.BlockSpec((tm, K), lambda i, j: (i, 0)),
            pl.BlockSpec((None, K, tn), lambda i, j: (l, 0, j)),
            pl.BlockSpec((tm, tn), lambda i, j: (i, j)),
        ],
        out_specs=[pl.BlockSpec((tm, tn), lambda i, j: (i, j))] + wb_specs,
        out_shape=[jax.ShapeDtypeStruct((M, N), F32)] + wb_shapes,
        compiler_params=_params(2),
        name="matmul_res",
    )(x, w, h)
    return (outs[0], outs[1][None]) if emit else outs[0]


def _lane_chunks(s):
    return [s[:, c * LANES:(c + 1) * LANES] for c in range(s.shape[1] // LANES)]


def _chunk_max(chunks):
    m = chunks[0]
    for c in chunks[1:]:
        m = jnp.maximum(m, c)
    return m


def _softmax_step(s_list, m_ref, l_ref, m_blk=None):
    cols = [_lane_chunks(s) for s in s_list]
    if m_blk is None:
        m_blk = _chunk_max([c for cs in cols for c in cs])
    m_prev = m_ref[...]
    m_new = jnp.maximum(m_prev, jnp.max(m_blk, axis=-1, keepdims=True))
    alpha = jnp.exp2(m_prev - m_new)
    probs = [[jnp.exp2(c - m_new) for c in cs] for cs in cols]
    flat = [p for ps in probs for p in ps]
    l_blk = flat[0]
    for p in flat[1:]:
        l_blk = l_blk + p
    l_ref[...] = alpha * l_ref[...] + jnp.sum(l_blk, axis=-1, keepdims=True)
    m_ref[...] = m_new
    return [jnp.concatenate(ps, axis=1).astype(BF16) if len(ps) > 1 else ps[0].astype(BF16)
            for ps in probs], alpha


def _head_out(o1, o2, lam, subln, lam_init):
    d = o1 - lam * o2
    return _rms_rows(d, subln) * (1.0 - lam_init)


def _attn_prompt_kernel(qi_tab, kj_tab, q0_ref, k0_ref, qn_ref, kn_ref, v_ref, lam_ref, subln_ref, o_ref,
                        m_ref, l_ref, acc_ref, sa_ref, sb_ref, ma_ref, mb_ref, p_ref, a_ref, *, lam_init):
    step = pl.program_id(2)
    n_steps = pl.num_programs(2)
    qi = qi_tab[step]
    kj = kj_tab[step]
    nxt = jnp.where(step + 1 == n_steps, 0, step + 1)
    next_masked = qi_tab[nxt] == kj_tab[nxt]
    tq = qn_ref.shape[1]
    tk = kn_ref.shape[1]
    n_sub = m_ref.shape[0]

    def scores(q_ref, k_ref, bufs, masked):
        s_ref, mx_ref = bufs
        for sub in range(n_sub):
            sl = slice(sub * HEAD_DIM, (sub + 1) * HEAD_DIM)
            q = q_ref[0, :, sl]
            k = k_ref[0, :, sl].astype(BF16)
            s = lax.dot_general(q, k, (((1,), (1,)), ((), ())), preferred_element_type=F32)
            if masked:
                r = lax.broadcasted_iota(jnp.int32, (tq, tk), 0)
                c = lax.broadcasted_iota(jnp.int32, (tq, tk), 1)
                s = jnp.where(c <= r, s, NEG_INF)
            s_ref[sub] = s
            mx_ref[sub] = _chunk_max(_lane_chunks(s))

    def accumulate(bufs):
        s_ref, mx_ref = bufs
        for sub in range(n_sub):
            p, alpha = _softmax_step([s_ref[sub]], m_ref.at[sub], l_ref.at[sub], mx_ref[sub])
            p_ref[sub] = p[0]
            a_ref[sub] = alpha
        for sub in range(n_sub):
            vs = slice((sub // 2) * V_DIM, (sub // 2 + 1) * V_DIM)
            alpha = a_ref[sub]
            acc_ref[sub] = (jnp.concatenate([alpha, alpha], axis=1) * acc_ref[sub]
                            + jnp.dot(p_ref[sub], v_ref[0, :, vs].astype(BF16), preferred_element_type=F32))

    @pl.when((pl.program_id(0) == 0) & (pl.program_id(1) == 0) & (step == 0))
    def _():
        scores(q0_ref, k0_ref, (sa_ref, ma_ref), True)

    @pl.when(kj == 0)
    def _():
        m_ref[...] = jnp.full(m_ref.shape, NEG_INF, F32)
        l_ref[...] = jnp.zeros(l_ref.shape, F32)
        acc_ref[...] = jnp.zeros(acc_ref.shape, F32)

    for parity, (cur_ref, nxt_ref) in enumerate((((sa_ref, ma_ref), (sb_ref, mb_ref)),
                                                 ((sb_ref, mb_ref), (sa_ref, ma_ref)))):
        for masked in (False, True):
            @pl.when((step % 2 == parity) & (next_masked == masked))
            def _(cur_ref=cur_ref, nxt_ref=nxt_ref, masked=masked):
                scores(qn_ref, kn_ref, nxt_ref, masked)
                accumulate(cur_ref)

    @pl.when(kj == qi)
    def _():
        lam = _lambda(lam_ref, lam_init)
        for hv in range(n_sub // 2):
            inv1 = 1.0 / l_ref[2 * hv]
            inv2 = 1.0 / l_ref[2 * hv + 1]
            o1 = acc_ref[2 * hv] * jnp.concatenate([inv1, inv1], axis=1)
            o2 = acc_ref[2 * hv + 1] * jnp.concatenate([inv2, inv2], axis=1)
            o_ref[0, :, hv * V_DIM:(hv + 1) * V_DIM] = _head_out(
                o1, o2, lam, subln_ref[...], lam_init).astype(o_ref.dtype)


def _attn_prompt(q, k, v, lam_vecs, subln, l, lam_init, *, tq, heads_per_step):
    B, T, _ = q.shape
    nq = T // tq
    pairs = [(i, j) for i in range(nq) for j in range(i + 1)]
    qi_tab = jnp.asarray([p[0] for p in pairs], jnp.int32)
    kj_tab = jnp.asarray([p[1] for p in pairs], jnp.int32)
    hw = heads_per_step * V_DIM
    n_sub = 2 * heads_per_step
    n_hg = N_HEADS // heads_per_step
    n_steps = len(pairs)
    assert n_steps % 2 == 0

    def next_block(tab):
        def index_map(b, h, s, qt, kt):
            wrap_s = (s + 1 == n_steps).astype(jnp.int32)
            s_n = (s + 1) * (1 - wrap_s)
            wrap_h = ((h + wrap_s) == n_hg).astype(jnp.int32)
            h_n = (h + wrap_s) * (1 - wrap_h)
            b_n = jnp.minimum(b + wrap_h, B - 1)
            return (b_n, (qt if tab == "q" else kt)[s_n], h_n)
        return index_map

    grid_spec = pltpu.PrefetchScalarGridSpec(
        num_scalar_prefetch=2,
        grid=(B, n_hg, n_steps),
        in_specs=[
            pl.BlockSpec((1, tq, hw), lambda b, h, s, qt, kt: (0, 0, 0)),
            pl.BlockSpec((1, tq, hw), lambda b, h, s, qt, kt: (0, 0, 0)),
            pl.BlockSpec((1, tq, hw), next_block("q")),
            pl.BlockSpec((1, tq, hw), next_block("k")),
            pl.BlockSpec((1, tq, hw), lambda b, h, s, qt, kt: (b, kt[s], h)),
            pl.BlockSpec((None, 4, HEAD_DIM), lambda b, h, s, qt, kt: (l, 0, 0)),
            pl.BlockSpec((None, 1, V_DIM), lambda b, h, s, qt, kt: (l, 0, 0)),
        ],
        out_specs=pl.BlockSpec((1, tq, hw), lambda b, h, s, qt, kt: (b, qt[s], h)),
        scratch_shapes=[pltpu.VMEM((n_sub, tq, LANES), F32), pltpu.VMEM((n_sub, tq, LANES), F32),
                        pltpu.VMEM((n_sub, tq, V_DIM), F32),
                        pltpu.VMEM((n_sub, tq, tq), F32), pltpu.VMEM((n_sub, tq, tq), F32),
                        pltpu.VMEM((n_sub, tq, LANES), F32), pltpu.VMEM((n_sub, tq, LANES), F32),
                        pltpu.VMEM((n_sub, tq, tq), BF16), pltpu.VMEM((n_sub, tq, LANES), F32)],
    )
    return pl.pallas_call(
        functools.partial(_attn_prompt_kernel, lam_init=lam_init),
        grid_spec=grid_spec,
        out_shape=jax.ShapeDtypeStruct((B, T, N_HEADS * V_DIM), BF16),
        compiler_params=_params(3),
        name="attn_prompt",
    )(qi_tab, kj_tab, q, k, q, k, v, lam_vecs, subln)


def _attn_sample_step(load_tiles, q_ref, kn_ref, vn_ref, lam_ref, subln_ref, o_ref,
                      qbd_ref, m_ref, l_ref, acc_ref, lam_init):
    p_idx = pl.program_id(1)
    tq = q_ref.shape[1]
    rows = N_QK * tq
    d_all = N_QK * HEAD_DIM

    @pl.when(p_idx == 0)
    def _():
        m_ref[...] = jnp.full(m_ref.shape, NEG_INF, F32)
        l_ref[...] = jnp.zeros(l_ref.shape, F32)
        acc_ref[...] = jnp.zeros(acc_ref.shape, F32)
        qt = jnp.concatenate([q_ref[0]] * N_QK, axis=0)
        r = lax.broadcasted_iota(jnp.int32, (rows, d_all), 0)
        c = lax.broadcasted_iota(jnp.int32, (rows, d_all), 1)
        qbd_ref[...] = jnp.where(r // tq == c // HEAD_DIM, qt, 0.0).astype(BF16)

    def update(k_list, v_list, mask):
        qbd = qbd_ref[...]
        s_list = []
        for kp in k_list:
            s = lax.dot_general(qbd, kp, (((1,), (1,)), ((), ())), preferred_element_type=F32)
            if mask is not None:
                s = jnp.where(mask, s, NEG_INF)
            s_list.append(s)
        p_list, alpha = _softmax_step(s_list, m_ref, l_ref)
        for hv in range(N_HEADS):
            rs = slice(hv * 2 * tq, (hv + 1) * 2 * tq)
            pv = None
            for p, v_heads in zip(p_list, v_list):
                t = jnp.dot(p[rs, :], v_heads[hv], preferred_element_type=F32)
                pv = t if pv is None else pv + t
            a = alpha[rs, :]
            acc_ref[rs, :] = jnp.concatenate([a, a], axis=1) * acc_ref[rs, :] + pv

    update(*load_tiles(), None)

    @pl.when(p_idx == pl.num_programs(1) - 1)
    def _():
        pad = jnp.zeros((PAGE_SIZE - tq, d_all), F32)
        kn = jnp.concatenate([kn_ref[0], pad], axis=0).astype(BF16)
        vn = jnp.concatenate([vn_ref[0], pad], axis=0).astype(BF16)
        r = lax.broadcasted_iota(jnp.int32, (rows, PAGE_SIZE), 0)
        c = lax.broadcasted_iota(jnp.int32, (rows, PAGE_SIZE), 1)
        update([kn], [[vn[:, hv * V_DIM:(hv + 1) * V_DIM] for hv in range(N_HEADS)]], c <= r % tq)
        lam = _lambda(lam_ref, lam_init)
        inv = 1.0 / l_ref[...]
        o = acc_ref[...] * jnp.concatenate([inv, inv], axis=1)
        for hv in range(N_HEADS):
            o1 = o[hv * 2 * tq:hv * 2 * tq + tq, :]
            o2 = o[hv * 2 * tq + tq:(hv + 1) * 2 * tq, :]
            o_ref[0, :, hv * V_DIM:(hv + 1) * V_DIM] = _head_out(
                o1, o2, lam, subln_ref[...], lam_init).astype(o_ref.dtype)


def _attn_sample_kernel(pt_ref, q_ref, *refs, pages, lam_init):
    k_refs = refs[:pages]
    v_refs = refs[pages:3 * pages]

    def load_tiles():
        return ([_page_k(k) for k in k_refs],
                [[jnp.concatenate([_head_rows(r, hv, N_HEADS) for r in v_refs[2 * i:2 * i + 2]], axis=1)
                  for hv in range(N_HEADS)] for i in range(pages)])

    _attn_sample_step(load_tiles, q_ref, *refs[3 * pages:], lam_init)


def _attn_sample(q, cache_k, cache_v, page_table, k_new, v_new, lam_vecs, subln, l, lam_init, *,
                 pages_per_step):
    Bd, Tq, D = q.shape
    n_pages = page_table.shape[1]
    pps = pages_per_step

    def page_map(i, half):
        return lambda b, p, pt: (pt[b * n_pages + p * pps + i], 0, half)

    k_specs = [pl.BlockSpec((None, PAGE_SIZE * N_QK, HEAD_DIM), page_map(i, 0)) for i in range(pps)]
    v_specs = [pl.BlockSpec((None, PAGE_SIZE * N_HEADS, LANES), page_map(i, half))
               for i in range(pps) for half in range(V_DIM // LANES)]
    row_spec = pl.BlockSpec((1, Tq, D), lambda b, p, pt: (b, 0, 0))
    grid_spec = pltpu.PrefetchScalarGridSpec(
        num_scalar_prefetch=1,
        grid=(Bd, n_pages // pps),
        in_specs=[row_spec] + k_specs + v_specs + [
            row_spec, row_spec,
            pl.BlockSpec((None, 4, HEAD_DIM), lambda b, p, pt: (l, 0, 0)),
            pl.BlockSpec((None, 1, V_DIM), lambda b, p, pt: (l, 0, 0)),
        ],
        out_specs=row_spec,
        scratch_shapes=[pltpu.VMEM((N_QK * Tq, D), BF16),
                        pltpu.VMEM((N_QK * Tq, LANES), F32), pltpu.VMEM((N_QK * Tq, LANES), F32),
                        pltpu.VMEM((N_QK * Tq, V_DIM), F32)],
    )
    return pl.pallas_call(
        functools.partial(_attn_sample_kernel, pages=pps, lam_init=lam_init),
        grid_spec=grid_spec,
        out_shape=jax.ShapeDtypeStruct((Bd, Tq, D), BF16),
        compiler_params=_params(2),
        name="attn_sample",
    )(page_table.reshape(-1), q, *([cache_k] * pps), *([cache_v] * (2 * pps)), k_new, v_new, lam_vecs, subln)


def _rope_tables(pos):
    half = HEAD_DIM // 2
    inv = ROPE_THETA ** (-jnp.arange(half, dtype=F32) / half)
    ang = pos.astype(F32)[:, None] * inv[None, :]
    cos = jnp.cos(ang)
    sin = jnp.sin(ang)
    return jnp.concatenate([cos, cos], axis=-1), jnp.concatenate([-sin, sin], axis=-1)


class _Group:
    def __init__(self, x, pos, cfg):
        self.B, self.T, _ = x.shape
        self.h = x.reshape(self.B * self.T, D_MODEL)
        self.cfg = cfg
        cos, sin = _rope_tables(pos)
        if self.T < cfg["tm"]:
            cos = jnp.tile(cos, (cfg["tm"] // self.T, 1))
            sin = jnp.tile(sin, (cfg["tm"] // self.T, 1))
        self.rope = (cos, sin)
        self.v_rows = []
        self.k_sh = self.v_sh = self.k_mxu = self.v_mxu = None


def _layer(G, l, p, W, Wb=None, paged=None):
    cfg, B, T, h = G.cfg, G.B, G.T, G.h
    M = B * T
    tm, rows, chunk = cfg["tm"], cfg["rows"], cfg["chunk"]
    cos, sin = G.rope
    emit = Wb is not None

    def keep(name, outs):
        if not emit:
            return outs
        for n, wb in zip(name.split(","), outs[1:]):
            Wb[n].append((wb, 0))
        return outs[0]

    if l < N_A_LAYERS:
        z = keep("w_in_a", _norm_matmul(h, p["norm_a"][l], W["w_in_a"][l], tm=cfg["tm_in"], tn=cfg["tn_in"],
                                        out_dtype=cfg["z_dtype"], epilogue="gelu", emit=emit))
        ws = p["w_s_a"][l][:, :chunk, :chunk]
        bs_t = p["b_s_a"][l][:, :chunk].T
        if rows > chunk:
            ws = jnp.tile(ws, (1, rows // chunk, rows // chunk))
            bs_t = jnp.tile(bs_t, (rows // chunk, 1))
        h = keep("w_out_a", _gmlp_out(z, ws, bs_t, h, W["w_out_a"][l], tm=cfg["tm_gmlp"], tn=cfg["tn_gmlp"],
                                      rows=rows, chunk=chunk, emit=emit))
        G.v_rows.append(z[:, D_GATE:])
    else:
        if l == N_A_LAYERS:
            dup = paged is None
            k_sh = keep("w_k", _norm_matmul(h, p["norm_kv"][0], W["w_k"][0], tm=tm, tn=cfg["tn"], out_dtype=F32,
                                            epilogue="rope", rope=(cos, sin), emit=emit, dup=dup))
            v_sh = keep("w_v", _norm_matmul(h, p["norm_kv"][0], W["w_v"][0], tm=tm, tn=cfg["tn"], out_dtype=F32,
                                            emit=emit, dup=dup))
            if dup:
                (k_sh, G.k_mxu), (v_sh, G.v_mxu) = k_sh, v_sh
            G.k_sh, G.v_sh = k_sh, v_sh
        j = l - N_A_LAYERS
        lam_init = 0.8 - 0.6 * math.exp(-0.3 * l)
        q = keep("w_q", _norm_matmul(h, p["norm_b"][j], W["w_q"][j], tm=tm, tn=cfg["tn"],
                                     out_dtype=cfg["q_dtype"], epilogue="rope", rope=(cos, sin),
                                     scale=HEAD_DIM ** -0.5 * LOG2E, emit=emit))
        if paged is None:
            o = _attn_prompt(q.reshape(B, T, -1), G.k_mxu.reshape(B, T, -1), G.v_mxu.reshape(B, T, -1),
                             p["lam_vecs"], p["subln_b"], j, lam_init, tq=cfg["tq"],
                             heads_per_step=cfg["hps"])
        else:
            o = _attn_sample(q.reshape(B, T, -1), *paged, G.k_sh.reshape(B, T, -1), G.v_sh.reshape(B, T, -1),
                             p["lam_vecs"], p["subln_b"], j, lam_init, pages_per_step=cfg["pps"])
        h = keep("w_o_b", _matmul_res(o.reshape(M, -1), W["w_o_b"][j], h, tm=tm, tn=cfg["tn"], emit=emit))
    G.h = keep("w_up,w_down", _ffn(h, p["norm_ffn"][l], W["w_up"][l], W["w_down"][l],
                                   p["norm_f"] if l == DEPTH - 1 else None,
                                   tm=cfg["tm_ffn"], tf=cfg["tf"], emit=emit))


def kernel(x_prompt, x_sample, cache_k, cache_v, page_table, norm_a, w_in_a, w_s_a, b_s_a, w_out_a, norm_kv, w_k, w_v, norm_b, w_q, lambda_q1, lambda_k1, lambda_q2, lambda_k2, subln_b, w_o_b, norm_ffn, w_up, w_down, norm_f):
    def layers(a):
        return [(a, l) for l in range(a.shape[0])]

    def gains(g):
        return layers(g.reshape(g.shape[0], 1, g.shape[1]))

    p = dict(norm_a=gains(norm_a), w_s_a=w_s_a, b_s_a=b_s_a, norm_kv=gains(norm_kv[None]),
             norm_b=gains(norm_b),
             lam_vecs=jnp.stack([lambda_q1, lambda_k1, lambda_q2, lambda_k2], axis=1),
             subln_b=subln_b.reshape(subln_b.shape[0], 1, -1), norm_ffn=gains(norm_ffn),
             norm_f=norm_f.reshape(1, -1))
    W = dict(w_in_a=layers(w_in_a), w_out_a=layers(w_out_a), w_k=layers(w_k[None]), w_v=layers(w_v[None]),
             w_q=layers(w_q), w_o_b=layers(w_o_b), w_up=layers(w_up), w_down=layers(w_down))

    B, T, _ = x_prompt.shape
    Bd, Td, _ = x_sample.shape

    Ms = Bd * Td
    cfg_s = dict(tm=Ms, tn=1024, tm_in=Ms, tn_in=1024, tm_gmlp=Ms, tn_gmlp=512, tm_ffn=Ms, tf=512,
                 pps=8, rows=Ms, chunk=Td, z_dtype=F32, q_dtype=F32)
    cfg_p = dict(tm=512, tn=2048, tm_in=1024, tn_in=1024, tm_gmlp=512, tn_gmlp=1024, tm_ffn=512, tf=1024,
                 tq=512, hps=4, rows=CHUNK, chunk=CHUNK, z_dtype=BF16, q_dtype=BF16)
    S = _Group(x_sample, PAST_LEN + jnp.arange(Td, dtype=jnp.int32), cfg_s)
    P = _Group(x_prompt, jnp.arange(T, dtype=jnp.int32), cfg_p)

    n_pool = cache_k.shape[0]
    paged = (cache_k.reshape(n_pool, PAGE_SIZE * N_QK, HEAD_DIM),
             cache_v.reshape(n_pool, PAGE_SIZE * N_HEADS, V_DIM), page_table)
    Wb = {name: [] for name in W}
    for l in range(DEPTH):
        _layer(S, l, p, W, Wb=Wb, paged=paged)
    for l in range(DEPTH):
        _layer(P, l, p, Wb)

    return (P.h.reshape(B, T, D_MODEL), S.h.reshape(Bd, Td, D_MODEL),
            P.k_sh.reshape(B, T, N_QK, HEAD_DIM), P.v_sh.reshape(B, T, N_HEADS, V_DIM),
            S.k_sh.reshape(Bd, Td, N_QK, HEAD_DIM), S.v_sh.reshape(Bd, Td, N_HEADS, V_DIM),
            jnp.stack(S.v_rows).reshape(N_A_LAYERS, Bd, Td, D_GATE))
```

```python
import functools
import math

import jax
import jax.numpy as jnp
from jax import lax
from jax.experimental import pallas as pl
from jax.experimental.pallas import tpu as pltpu

D_MODEL = 2048
DEPTH = 4
PAST_LEN = 16384
PAGE_SIZE = 128
N_A_LAYERS = DEPTH // 2
CHUNK = 128
D_GATE = 2 * D_MODEL
N_GROUPS_A = 16
GROUP_DIM_A = D_GATE // N_GROUPS_A
HEAD_DIM = 128
N_HEADS = D_MODEL // (2 * HEAD_DIM)
N_QK = 2 * N_HEADS
V_DIM = 2 * HEAD_DIM
D_FF = 4 * D_MODEL
ROPE_THETA = 10000.0
EPS = 1e-5
NEG_INF = -1e30

LANES = 128
VMEM_LIMIT = 56 * 1024 * 1024
LOG2E = math.log2(math.e)

F32 = jnp.float32
BF16 = jnp.bfloat16


def _params(n_axes):
    return pltpu.CompilerParams(dimension_semantics=("arbitrary",) * n_axes,
                                vmem_limit_bytes=VMEM_LIMIT)


def _rms_rows(x, g):
    return x * lax.rsqrt(jnp.mean(x * x, axis=-1, keepdims=True) + EPS) * g


def _lambda(lam_ref, lam_init):
    a = jnp.sum(lam_ref[0:1, :] * lam_ref[1:2, :], axis=-1, keepdims=True)
    b = jnp.sum(lam_ref[2:3, :] * lam_ref[3:4, :], axis=-1, keepdims=True)
    return jnp.exp(a) - jnp.exp(b) + lam_init


def _mxu_weight(w_ref, wb_ref):
    w = w_ref[...].astype(BF16)
    if wb_ref is not None:
        wb_ref[...] = w
    return w


def _norm_matmul_kernel(*refs, epilogue, scale, emit, dup):
    x_ref, g_ref, w_ref = refs[:3]
    refs = refs[3:]
    if epilogue == "rope":
        cos_ref, sin_ref = refs[:2]
        refs = refs[2:]
    out_refs = refs[:2] if dup else refs[:1]
    wb_ref = refs[len(out_refs)] if emit else None
    xn_ref = refs[-1]

    def put(sl, val):
        for o_ref in out_refs:
            o_ref[:, sl] = val.astype(o_ref.dtype)

    @pl.when(pl.program_id(1) == 0)
    def _():
        xn_ref[...] = _rms_rows(x_ref[...], g_ref[...]).astype(BF16)

    y = jnp.dot(xn_ref[...], _mxu_weight(w_ref, wb_ref), preferred_element_type=F32)
    if epilogue == "gelu":
        put(slice(None), 0.5 * y * (1.0 + lax.erf(y * (2.0 ** -0.5))))
    elif epilogue == "rope":
        cos = cos_ref[...]
        sin = sin_ref[...]
        for h in range(y.shape[1] // HEAD_DIM):
            sl = slice(h * HEAD_DIM, (h + 1) * HEAD_DIM)
            yh = y[:, sl]
            oh = yh * cos + pltpu.roll(yh, HEAD_DIM // 2, 1) * sin
            if scale != 1.0:
                oh = oh * scale
            put(sl, oh)
    else:
        put(slice(None), y)


def _weight_copy_out(emit, M, tm, K, N, bk, bn, index_map):
    if not emit:
        return [], []
    assert M == tm
    return [pl.BlockSpec((bk, bn), index_map)], [jax.ShapeDtypeStruct((K, N), BF16)]


def _norm_matmul(x, gl, wl, *, tm, tn, out_dtype, epilogue=None, rope=None, scale=1.0, emit=False,
                 dup=False):
    assert not (emit and dup)
    g, lg = gl
    w, l = wl
    M, K = x.shape
    N = w.shape[2]
    wb_specs, wb_shapes = _weight_copy_out(emit, M, tm, K, N, K, tn, lambda i, j: (0, j))
    in_specs = [
        pl.BlockSpec((tm, K), lambda i, j: (i, 0)),
        pl.BlockSpec((None, 1, K), lambda i, j: (lg, 0, 0)),
        pl.BlockSpec((None, K, tn), lambda i, j: (l, 0, j)),
    ]
    args = [x, g, w]
    if epilogue == "rope":
        cos, sin = rope
        nb = cos.shape[0] // tm
        in_specs += [pl.BlockSpec((tm, HEAD_DIM), lambda i, j: (i % nb, 0))] * 2
        args += [cos, sin]
    outs = pl.pallas_call(
        functools.partial(_norm_matmul_kernel, epilogue=epilogue, scale=scale, emit=emit, dup=dup),
        grid=(M // tm, N // tn),
        in_specs=in_specs,
        out_specs=[pl.BlockSpec((tm, tn), lambda i, j: (i, j))] * (2 if dup else 1) + wb_specs,
        out_shape=[jax.ShapeDtypeStruct((M, N), out_dtype)]
        + ([jax.ShapeDtypeStruct((M, N), BF16)] if dup else []) + wb_shapes,
        scratch_shapes=[pltpu.VMEM((tm, K), BF16)],
        compiler_params=_params(2),
        name="norm_matmul_" + (epilogue or "plain"),
    )(*args)
    if emit:
        return outs[0], outs[1][None]
    return (outs[0], outs[1]) if dup else outs[0]


def _gmlp_out_kernel(u_ref, v_ref, ws_ref, bs_ref, h_ref, w_ref, o_ref, *refs, rows, chunk, emit):
    wb_ref = refs[0] if emit else None
    gated_ref, wt_ref = refs[-2:]
    tm = u_ref.shape[0]

    @pl.when(pl.program_id(1) == 0)
    def _():
        r = lax.broadcasted_iota(jnp.int32, (rows, rows), 0)
        c = lax.broadcasted_iota(jnp.int32, (rows, rows), 1)
        mask = (r // chunk == c // chunk) & (r >= c)
        for g in range(N_GROUPS_A):
            wt_ref[g] = jnp.where(mask, ws_ref[g], 0.0).astype(BF16)

        def mix(ci, carry):
            r0 = pl.multiple_of(ci * rows, rows)
            for g in range(N_GROUPS_A):
                sl = slice(g * GROUP_DIM_A, (g + 1) * GROUP_DIM_A)
                vg = v_ref[pl.ds(r0, rows), sl].astype(BF16)
                s = jnp.dot(wt_ref[g], vg, preferred_element_type=F32) + bs_ref[:, g:g + 1]
                ug = u_ref[pl.ds(r0, rows), sl].astype(F32)
                gated_ref[pl.ds(r0, rows), sl] = (ug * s).astype(BF16)
            return carry

        lax.fori_loop(0, tm // rows, mix, 0)

    o_ref[...] = h_ref[...] + jnp.dot(gated_ref[...], _mxu_weight(w_ref, wb_ref),
                                      preferred_element_type=F32)


def _gmlp_out(z, ws, bs_t, h, wl, *, tm, tn, rows, chunk, emit=False):
    w_out, l = wl
    M = z.shape[0]
    N = w_out.shape[2]
    wb_specs, wb_shapes = _weight_copy_out(emit, M, tm, D_GATE, N, D_GATE, tn, lambda i, j: (0, j))
    outs = pl.pallas_call(
        functools.partial(_gmlp_out_kernel, rows=rows, chunk=chunk, emit=emit),
        grid=(M // tm, N // tn),
        in_specs=[
            pl.BlockSpec((tm, D_GATE), lambda i, j: (i, 0)),
            pl.BlockSpec((tm, D_GATE), lambda i, j: (i, 1)),
            pl.BlockSpec((N_GROUPS_A, rows, rows), lambda i, j: (0, 0, 0)),
            pl.BlockSpec((rows, N_GROUPS_A), lambda i, j: (0, 0)),
            pl.BlockSpec((tm, tn), lambda i, j: (i, j)),
            pl.BlockSpec((None, D_GATE, tn), lambda i, j: (l, 0, j)),
        ],
        out_specs=[pl.BlockSpec((tm, tn), lambda i, j: (i, j))] + wb_specs,
        out_shape=[jax.ShapeDtypeStruct((M, N), F32)] + wb_shapes,
        scratch_shapes=[pltpu.VMEM((tm, D_GATE), BF16),
                        pltpu.VMEM((N_GROUPS_A, rows, rows), BF16)],
        compiler_params=_params(2),
        name="gmlp_out",
    )(z, z, ws, bs_t, h, w_out)
    return (outs[0], outs[1][None]) if emit else outs[0]


def _gmlp_gate_kernel(x_ref, g_ref, wu_ref, wv_ref, ws_ref, bs_ref, o_ref, xn_ref):
    @pl.when(pl.program_id(1) == 0)
    def _():
        xn_ref[...] = _rms_rows(x_ref[...], g_ref[...]).astype(BF16)

    def gelu(y):
        return 0.5 * y * (1.0 + lax.erf(y * (2.0 ** -0.5)))

    xn = xn_ref[...]
    u = gelu(jnp.dot(xn, wu_ref[...], preferred_element_type=F32))
    v = gelu(jnp.dot(xn, wv_ref[...], preferred_element_type=F32)).astype(BF16)
    r = lax.broadcasted_iota(jnp.int32, (CHUNK, CHUNK), 0)
    c = lax.broadcasted_iota(jnp.int32, (CHUNK, CHUNK), 1)
    for gg in range(ws_ref.shape[0]):
        wt = jnp.where(r >= c, ws_ref[gg], 0.0).astype(BF16)
        b = bs_ref[gg]
        b = jnp.concatenate([b] * (GROUP_DIM_A // LANES), axis=1)
        cs = slice(gg * GROUP_DIM_A, (gg + 1) * GROUP_DIM_A)
        for ci in range(u.shape[0] // CHUNK):
            rs = slice(ci * CHUNK, (ci + 1) * CHUNK)
            s = jnp.dot(wt, v[rs, cs], preferred_element_type=F32) + b
            o_ref[rs, cs] = (u[rs, cs] * s).astype(o_ref.dtype)


def _gmlp_gate(x, gl, wl, ws, bs, *, tm, tn):
    g, lg = gl
    w, l = wl
    M, K = x.shape
    gpt = tn // GROUP_DIM_A
    nt = D_GATE // tn
    bs_rep = jnp.broadcast_to(bs[:, :, None], bs.shape + (LANES,))
    return pl.pallas_call(
        _gmlp_gate_kernel,
        grid=(M // tm, nt),
        in_specs=[
            pl.BlockSpec((tm, K), lambda i, j: (i, 0)),
            pl.BlockSpec((None, 1, K), lambda i, j: (lg, 0, 0)),
            pl.BlockSpec((None, K, tn), lambda i, j: (l, 0, j)),
            pl.BlockSpec((None, K, tn), lambda i, j: (l, 0, nt + j)),
            pl.BlockSpec((gpt, CHUNK, CHUNK), lambda i, j: (j, 0, 0)),
            pl.BlockSpec((gpt, CHUNK, LANES), lambda i, j: (j, 0, 0)),
        ],
        out_specs=pl.BlockSpec((tm, tn), lambda i, j: (i, j)),
        out_shape=jax.ShapeDtypeStruct((M, D_GATE), BF16),
        scratch_shapes=[pltpu.VMEM((tm, K), BF16)],
        compiler_params=_params(2),
        name="gmlp_gate",
    )(x, g, w, w, ws, bs_rep)


def _ffn_kernel(*refs, final_norm, emit):
    x_ref, g_ref, wu_ref, wd_ref = refs[:4]
    refs = refs[4:]
    if final_norm:
        gf_ref = refs[0]
        refs = refs[1:]
    o_ref = refs[0]
    wub_ref, wdb_ref = refs[1:3] if emit else (None, None)
    xn_ref = refs[-1]
    f = pl.program_id(1)

    @pl.when(f == 0)
    def _():
        x = x_ref[...]
        xn_ref[...] = _rms_rows(x, g_ref[...]).astype(BF16)
        o_ref[...] = x

    a = jnp.dot(xn_ref[...], _mxu_weight(wu_ref, wub_ref), preferred_element_type=F32)
    a = jnp.square(jnp.maximum(a, 0.0)).astype(BF16)
    o_ref[...] += jnp.dot(a, _mxu_weight(wd_ref, wdb_ref), preferred_element_type=F32)

    if final_norm:
        @pl.when(f == pl.num_programs(1) - 1)
        def _():
            o_ref[...] = _rms_rows(o_ref[...], gf_ref[...])


def _ffn(x, gl, wul, wdl, g_final, *, tm, tf, emit=False):
    g, l = gl
    w_up, lu = wul
    w_down, ld = wdl
    M, K = x.shape
    F = w_up.shape[2]
    final_norm = g_final is not None
    in_specs = [
        pl.BlockSpec((tm, K), lambda i, f: (i, 0)),
        pl.BlockSpec((None, 1, K), lambda i, f: (l, 0, 0)),
        pl.BlockSpec((None, K, tf), lambda i, f: (lu, 0, f)),
        pl.BlockSpec((None, tf, K), lambda i, f: (ld, f, 0)),
    ]
    ub_specs, ub_shapes = _weight_copy_out(emit, M, tm, K, F, K, tf, lambda i, f: (0, f))
    db_specs, db_shapes = _weight_copy_out(emit, M, tm, F, K, tf, K, lambda i, f: (f, 0))
    args = [x, g, w_up, w_down]
    if final_norm:
        in_specs.append(pl.BlockSpec((1, K), lambda i, f: (0, 0)))
        args.append(g_final)
    outs = pl.pallas_call(
        functools.partial(_ffn_kernel, final_norm=final_norm, emit=emit),
        grid=(M // tm, F // tf),
        in_specs=in_specs,
        out_specs=[pl.BlockSpec((tm, K), lambda i, f: (i, 0))] + ub_specs + db_specs,
        out_shape=[jax.ShapeDtypeStruct((M, K), F32)] + ub_shapes + db_shapes,
        scratch_shapes=[pltpu.VMEM((tm, K), BF16)],
        compiler_params=_params(2),
        name="ffn",
    )(*args)
    return (outs[0], outs[1][None], outs[2][None]) if emit else outs[0]


def _head_rows(ref, h, n_heads):
    return ref[pl.ds(h, PAGE_SIZE, stride=n_heads), :].astype(BF16)


def _page_k(k_ref):
    return jnp.concatenate([_head_rows(k_ref, h, N_QK) for h in range(N_QK)], axis=1)


def _matmul_res_kernel(x_ref, w_ref, h_ref, o_ref, wb_ref=None):
    o_ref[...] = h_ref[...] + jnp.dot(x_ref[...], _mxu_weight(w_ref, wb_ref), preferred_element_type=F32)


def _matmul_res(x, wl, h, *, tm, tn, emit=False):
    w, l = wl
    M, K = x.shape
    N = w.shape[2]
    wb_specs, wb_shapes = _weight_copy_out(emit, M, tm, K, N, K, tn, lambda i, j: (0, j))
    outs = pl.pallas_call(
        _matmul_res_kernel,
        grid=(M // tm, N // tn),
        in_specs=[
            pl.BlockSpec((tm, K), lambda i, j: (i, 0)),
            pl.BlockSpec((None, K, tn), lambda i, j: (l, 0, j)),
            pl.BlockSpec((tm, tn), lambda i, j: (i, j)),
        ],
        out_specs=[pl.BlockSpec((tm, tn), lambda i, j: (i, j))] + wb_specs,
        out_shape=[jax.ShapeDtypeStruct((M, N), F32)] + wb_shapes,
        compiler_params=_params(2),
        name="matmul_res",
    )(x, w, h)
    return (outs[0], outs[1][None]) if emit else outs[0]


def _lane_chunks(s):
    return [s[:, c * LANES:(c + 1) * LANES] for c in range(s.shape[1] // LANES)]


def _chunk_max(chunks):
    m = chunks[0]
    for c in chunks[1:]:
        m = jnp.maximum(m, c)
    return m


def _softmax_step(s_list, m_ref, l_ref, m_blk=None):
    cols = [_lane_chunks(s) for s in s_list]
    if m_blk is None:
        m_blk = _chunk_max([c for cs in cols for c in cs])
    m_prev = m_ref[...]
    m_new = jnp.maximum(m_prev, jnp.max(m_blk, axis=-1, keepdims=True))
    alpha = jnp.exp2(m_prev - m_new)
    probs = [[jnp.exp2(c - m_new) for c in cs] for cs in cols]
    flat = [p for ps in probs for p in ps]
    l_blk = flat[0]
    for p in flat[1:]:
        l_blk = l_blk + p
    l_ref[...] = alpha * l_ref[...] + jnp.sum(l_blk, axis=-1, keepdims=True)
    m_ref[...] = m_new
    return [jnp.concatenate(ps, axis=1).astype(BF16) if len(ps) > 1 else ps[0].astype(BF16)
            for ps in probs], alpha


def _head_out(o1, o2, lam, subln, lam_init):
    d = o1 - lam * o2
    return _rms_rows(d, subln) * (1.0 - lam_init)


def _attn_prompt_kernel(qi_tab, kj_tab, q0_ref, k0_ref, qn_ref, kn_ref, v_ref, lam_ref, subln_ref, o_ref,
                        m_ref, l_ref, acc_ref, sa_ref, sb_ref, ma_ref, mb_ref, p_ref, a_ref, *, lam_init):
    step = pl.program_id(2)
    n_steps = pl.num_programs(2)
    qi = qi_tab[step]
    kj = kj_tab[step]
    nxt = jnp.where(step + 1 == n_steps, 0, step + 1)
    next_masked = qi_tab[nxt] == kj_tab[nxt]
    tq = qn_ref.shape[1]
    tk = kn_ref.shape[1]
    n_sub = m_ref.shape[0]

    def scores(q_ref, k_ref, bufs, masked):
        s_ref, mx_ref = bufs
        for sub in range(n_sub):
            sl = slice(sub * HEAD_DIM, (sub + 1) * HEAD_DIM)
            q = q_ref[0, :, sl]
            k = k_ref[0, :, sl].astype(BF16)
            s = lax.dot_general(q, k, (((1,), (1,)), ((), ())), preferred_element_type=F32)
            if masked:
                r = lax.broadcasted_iota(jnp.int32, (tq, tk), 0)
                c = lax.broadcasted_iota(jnp.int32, (tq, tk), 1)
                s = jnp.where(c <= r, s, NEG_INF)
            s_ref[sub] = s
            mx_ref[sub] = _chunk_max(_lane_chunks(s))

    def accumulate(bufs):
        s_ref, mx_ref = bufs
        for sub in range(n_sub):
            p, alpha = _softmax_step([s_ref[sub]], m_ref.at[sub], l_ref.at[sub], mx_ref[sub])
            p_ref[sub] = p[0]
            a_ref[sub] = alpha
        for sub in range(n_sub):
            vs = slice((sub // 2) * V_DIM, (sub // 2 + 1) * V_DIM)
            alpha = a_ref[sub]
            acc_ref[sub] = (jnp.concatenate([alpha, alpha], axis=1) * acc_ref[sub]
                            + jnp.dot(p_ref[sub], v_ref[0, :, vs].astype(BF16), preferred_element_type=F32))

    @pl.when((pl.program_id(0) == 0) & (pl.program_id(1) == 0) & (step == 0))
    def _():
        scores(q0_ref, k0_ref, (sa_ref, ma_ref), True)

    @pl.when(kj == 0)
    def _():
        m_ref[...] = jnp.full(m_ref.shape, NEG_INF, F32)
        l_ref[...] = jnp.zeros(l_ref.shape, F32)
        acc_ref[...] = jnp.zeros(acc_ref.shape, F32)

    for parity, (cur_ref, nxt_ref) in enumerate((((sa_ref, ma_ref), (sb_ref, mb_ref)),
                                                 ((sb_ref, mb_ref), (sa_ref, ma_ref)))):
        for masked in (False, True):
            @pl.when((step % 2 == parity) & (next_masked == masked))
            def _(cur_ref=cur_ref, nxt_ref=nxt_ref, masked=masked):
                scores(qn_ref, kn_ref, nxt_ref, masked)
                accumulate(cur_ref)

    @pl.when(kj == qi)
    def _():
        lam = _lambda(lam_ref, lam_init)
        for hv in range(n_sub // 2):
            inv1 = 1.0 / l_ref[2 * hv]
            inv2 = 1.0 / l_ref[2 * hv + 1]
            o1 = acc_ref[2 * hv] * jnp.concatenate([inv1, inv1], axis=1)
            o2 = acc_ref[2 * hv + 1] * jnp.concatenate([inv2, inv2], axis=1)
            o_ref[0, :, hv * V_DIM:(hv + 1) * V_DIM] = _head_out(
                o1, o2, lam, subln_ref[...], lam_init).astype(o_ref.dtype)


def _attn_prompt(q, k, v, lam_vecs, subln, l, lam_init, *, tq, heads_per_step):
    B, T, _ = q.shape
    nq = T // tq
    pairs = [(i, j) for i in range(nq) for j in range(i + 1)]
    qi_tab = jnp.asarray([p[0] for p in pairs], jnp.int32)
    kj_tab = jnp.asarray([p[1] for p in pairs], jnp.int32)
    hw = heads_per_step * V_DIM
    n_sub = 2 * heads_per_step
    n_hg = N_HEADS // heads_per_step
    n_steps = len(pairs)
    assert n_steps % 2 == 0

    def next_block(tab):
        def index_map(b, h, s, qt, kt):
            wrap_s = (s + 1 == n_steps).astype(jnp.int32)
            s_n = (s + 1) * (1 - wrap_s)
            wrap_h = ((h + wrap_s) == n_hg).astype(jnp.int32)
            h_n = (h + wrap_s) * (1 - wrap_h)
            b_n = jnp.minimum(b + wrap_h, B - 1)
            return (b_n, (qt if tab == "q" else kt)[s_n], h_n)
        return index_map

    grid_spec = pltpu.PrefetchScalarGridSpec(
        num_scalar_prefetch=2,
        grid=(B, n_hg, n_steps),
        in_specs=[
            pl.BlockSpec((1, tq, hw), lambda b, h, s, qt, kt: (0, 0, 0)),
            pl.BlockSpec((1, tq, hw), lambda b, h, s, qt, kt: (0, 0, 0)),
            pl.BlockSpec((1, tq, hw), next_block("q")),
            pl.BlockSpec((1, tq, hw), next_block("k")),
            pl.BlockSpec((1, tq, hw), lambda b, h, s, qt, kt: (b, kt[s], h)),
            pl.BlockSpec((None, 4, HEAD_DIM), lambda b, h, s, qt, kt: (l, 0, 0)),
            pl.BlockSpec((None, 1, V_DIM), lambda b, h, s, qt, kt: (l, 0, 0)),
        ],
        out_specs=pl.BlockSpec((1, tq, hw), lambda b, h, s, qt, kt: (b, qt[s], h)),
        scratch_shapes=[pltpu.VMEM((n_sub, tq, LANES), F32), pltpu.VMEM((n_sub, tq, LANES), F32),
                        pltpu.VMEM((n_sub, tq, V_DIM), F32),
                        pltpu.VMEM((n_sub, tq, tq), F32), pltpu.VMEM((n_sub, tq, tq), F32),
                        pltpu.VMEM((n_sub, tq, LANES), F32), pltpu.VMEM((n_sub, tq, LANES), F32),
                        pltpu.VMEM((n_sub, tq, tq), BF16), pltpu.VMEM((n_sub, tq, LANES), F32)],
    )
    return pl.pallas_call(
        functools.partial(_attn_prompt_kernel, lam_init=lam_init),
        grid_spec=grid_spec,
        out_shape=jax.ShapeDtypeStruct((B, T, N_HEADS * V_DIM), BF16),
        compiler_params=_params(3),
        name="attn_prompt",
    )(qi_tab, kj_tab, q, k, q, k, v, lam_vecs, subln)


def _attn_sample_step(load_tiles, q_ref, kn_ref, vn_ref, lam_ref, subln_ref, o_ref,
                      qbd_ref, m_ref, l_ref, acc_ref, lam_init):
    p_idx = pl.program_id(1)
    tq = q_ref.shape[1]
    rows = N_QK * tq
    d_all = N_QK * HEAD_DIM

    @pl.when(p_idx == 0)
    def _():
        m_ref[...] = jnp.full(m_ref.shape, NEG_INF, F32)
        l_ref[...] = jnp.zeros(l_ref.shape, F32)
        acc_ref[...] = jnp.zeros(acc_ref.shape, F32)
        qt = jnp.concatenate([q_ref[0]] * N_QK, axis=0)
        r = lax.broadcasted_iota(jnp.int32, (rows, d_all), 0)
        c = lax.broadcasted_iota(jnp.int32, (rows, d_all), 1)
        qbd_ref[...] = jnp.where(r // tq == c // HEAD_DIM, qt, 0.0).astype(BF16)

    def update(k_list, v_list, mask):
        qbd = qbd_ref[...]
        s_list = []
        for kp in k_list:
            s = lax.dot_general(qbd, kp, (((1,), (1,)), ((), ())), preferred_element_type=F32)
            if mask is not None:
                s = jnp.where(mask, s, NEG_INF)
            s_list.append(s)
        p_list, alpha = _softmax_step(s_list, m_ref, l_ref)
        for hv in range(N_HEADS):
            rs = slice(hv * 2 * tq, (hv + 1) * 2 * tq)
            pv = None
            for p, v_heads in zip(p_list, v_list):
                t = jnp.dot(p[rs, :], v_heads[hv], preferred_element_type=F32)
                pv = t if pv is None else pv + t
            a = alpha[rs, :]
            acc_ref[rs, :] = jnp.concatenate([a, a], axis=1) * acc_ref[rs, :] + pv

    update(*load_tiles(), None)

    @pl.when(p_idx == pl.num_programs(1) - 1)
    def _():
        pad = jnp.zeros((PAGE_SIZE - tq, d_all), F32)
        kn = jnp.concatenate([kn_ref[0], pad], axis=0).astype(BF16)
        vn = jnp.concatenate([vn_ref[0], pad], axis=0).astype(BF16)
        r = lax.broadcasted_iota(jnp.int32, (rows, PAGE_SIZE), 0)
        c = lax.broadcasted_iota(jnp.int32, (rows, PAGE_SIZE), 1)
        update([kn], [[vn[:, hv * V_DIM:(hv + 1) * V_DIM] for hv in range(N_HEADS)]], c <= r % tq)
        lam = _lambda(lam_ref, lam_init)
        inv = 1.0 / l_ref[...]
        o = acc_ref[...] * jnp.concatenate([inv, inv], axis=1)
        for hv in range(N_HEADS):
            o1 = o[hv * 2 * tq:hv * 2 * tq + tq, :]
            o2 = o[hv * 2 * tq + tq:(hv + 1) * 2 * tq, :]
            o_ref[0, :, hv * V_DIM:(hv + 1) * V_DIM] = _head_out(
                o1, o2, lam, subln_ref[...], lam_init).astype(o_ref.dtype)


def _attn_sample_kernel(pt_ref, q_ref, *refs, pages, lam_init):
    k_refs = refs[:pages]
    v_refs = refs[pages:3 * pages]

    def load_tiles():
        return ([_page_k(k) for k in k_refs],
                [[jnp.concatenate([_head_rows(r, hv, N_HEADS) for r in v_refs[2 * i:2 * i + 2]], axis=1)
                  for hv in range(N_HEADS)] for i in range(pages)])

    _attn_sample_step(load_tiles, q_ref, *refs[3 * pages:], lam_init)


def _attn_sample(q, cache_k, cache_v, page_table, k_new, v_new, lam_vecs, subln, l, lam_init, *,
                 pages_per_step):
    Bd, Tq, D = q.shape
    n_pages = page_table.shape[1]
    pps = pages_per_step

    def page_map(i, half):
        return lambda b, p, pt: (pt[b * n_pages + p * pps + i], 0, half)

    k_specs = [pl.BlockSpec((None, PAGE_SIZE * N_QK, HEAD_DIM), page_map(i, 0)) for i in range(pps)]
    v_specs = [pl.BlockSpec((None, PAGE_SIZE * N_HEADS, LANES), page_map(i, half))
               for i in range(pps) for half in range(V_DIM // LANES)]
    row_spec = pl.BlockSpec((1, Tq, D), lambda b, p, pt: (b, 0, 0))
    grid_spec = pltpu.PrefetchScalarGridSpec(
        num_scalar_prefetch=1,
        grid=(Bd, n_pages // pps),
        in_specs=[row_spec] + k_specs + v_specs + [
            row_spec, row_spec,
            pl.BlockSpec((None, 4, HEAD_DIM), lambda b, p, pt: (l, 0, 0)),
            pl.BlockSpec((None, 1, V_DIM), lambda b, p, pt: (l, 0, 0)),
        ],
        out_specs=row_spec,
        scratch_shapes=[pltpu.VMEM((N_QK * Tq, D), BF16),
                        pltpu.VMEM((N_QK * Tq, LANES), F32), pltpu.VMEM((N_QK * Tq, LANES), F32),
                        pltpu.VMEM((N_QK * Tq, V_DIM), F32)],
    )
    return pl.pallas_call(
        functools.partial(_attn_sample_kernel, pages=pps, lam_init=lam_init),
        grid_spec=grid_spec,
        out_shape=jax.ShapeDtypeStruct((Bd, Tq, D), BF16),
        compiler_params=_params(2),
        name="attn_sample",
    )(page_table.reshape(-1), q, *([cache_k] * pps), *([cache_v] * (2 * pps)), k_new, v_new, lam_vecs, subln)


def _rope_tables(pos):
    half = HEAD_DIM // 2
    inv = ROPE_THETA ** (-jnp.arange(half, dtype=F32) / half)
    ang = pos.astype(F32)[:, None] * inv[None, :]
    cos = jnp.cos(ang)
    sin = jnp.sin(ang)
    return jnp.concatenate([cos, cos], axis=-1), jnp.concatenate([-sin, sin], axis=-1)


class _Group:
    def __init__(self, x, pos, cfg):
        self.B, self.T, _ = x.shape
        self.h = x.reshape(self.B * self.T, D_MODEL)
        self.cfg = cfg
        cos, sin = _rope_tables(pos)
        if self.T < cfg["tm"]:
            cos = jnp.tile(cos, (cfg["tm"] // self.T, 1))
            sin = jnp.tile(sin, (cfg["tm"] // self.T, 1))
        self.rope = (cos, sin)
        self.v_rows = []
        self.k_sh = self.v_sh = self.k_mxu = self.v_mxu = None


def _layer(G, l, p, W, Wb=None, paged=None):
    cfg, B, T, h = G.cfg, G.B, G.T, G.h
    M = B * T
    tm, rows, chunk = cfg["tm"], cfg["rows"], cfg["chunk"]
    cos, sin = G.rope
    emit = Wb is not None

    def keep(name, outs):
        if not emit:
            return outs
        for n, wb in zip(name.split(","), outs[1:]):
            Wb[n].append((wb, 0))
        return outs[0]

    if l < N_A_LAYERS and cfg["fused_gate"]:
        assert not emit and chunk == CHUNK
        gated = _gmlp_gate(h, p["norm_a"][l], W["w_in_a"][l], p["w_s_a"][l], p["b_s_a"][l],
                           tm=cfg["tm_in"], tn=cfg["tn_in"])
        h = _matmul_res(gated, W["w_out_a"][l], h, tm=cfg["tm_gmlp"], tn=cfg["tn_gmlp"])
    elif l < N_A_LAYERS:
        z = keep("w_in_a", _norm_matmul(h, p["norm_a"][l], W["w_in_a"][l], tm=cfg["tm_in"], tn=cfg["tn_in"],
                                        out_dtype=cfg["z_dtype"], epilogue="gelu", emit=emit))
        ws = p["w_s_a"][l][:, :chunk, :chunk]
        bs_t = p["b_s_a"][l][:, :chunk].T
        if rows > chunk:
            ws = jnp.tile(ws, (1, rows // chunk, rows // chunk))
            bs_t = jnp.tile(bs_t, (rows // chunk, 1))
        h = keep("w_out_a", _gmlp_out(z, ws, bs_t, h, W["w_out_a"][l], tm=cfg["tm_gmlp"], tn=cfg["tn_gmlp"],
                                      rows=rows, chunk=chunk, emit=emit))
        G.v_rows.append(z[:, D_GATE:])
    else:
        if l == N_A_LAYERS:
            dup = paged is None
            k_sh = keep("w_k", _norm_matmul(h, p["norm_kv"][0], W["w_k"][0], tm=tm, tn=cfg["tn"], out_dtype=F32,
                                            epilogue="rope", rope=(cos, sin), emit=emit, dup=dup))
            v_sh = keep("w_v", _norm_matmul(h, p["norm_kv"][0], W["w_v"][0], tm=tm, tn=cfg["tn"], out_dtype=F32,
                                            emit=emit, dup=dup))
            if dup:
                (k_sh, G.k_mxu), (v_sh, G.v_mxu) = k_sh, v_sh
            G.k_sh, G.v_sh = k_sh, v_sh
        j = l - N_A_LAYERS
        lam_init = 0.8 - 0.6 * math.exp(-0.3 * l)
        q = keep("w_q", _norm_matmul(h, p["norm_b"][j], W["w_q"][j], tm=tm, tn=cfg["tn"],
                                     out_dtype=cfg["q_dtype"], epilogue="rope", rope=(cos, sin),
                                     scale=HEAD_DIM ** -0.5 * LOG2E, emit=emit))
        if paged is None:
            o = _attn_prompt(q.reshape(B, T, -1), G.k_mxu.reshape(B, T, -1), G.v_mxu.reshape(B, T, -1),
                             p["lam_vecs"], p["subln_b"], j, lam_init, tq=cfg["tq"],
                             heads_per_step=cfg["hps"])
        else:
            o = _attn_sample(q.reshape(B, T, -1), *paged, G.k_sh.reshape(B, T, -1), G.v_sh.reshape(B, T, -1),
                             p["lam_vecs"], p["subln_b"], j, lam_init, pages_per_step=cfg["pps"])
        h = keep("w_o_b", _matmul_res(o.reshape(M, -1), W["w_o_b"][j], h, tm=tm, tn=cfg["tn"], emit=emit))
    G.h = keep("w_up,w_down", _ffn(h, p["norm_ffn"][l], W["w_up"][l], W["w_down"][l],
                                   p["norm_f"] if l == DEPTH - 1 else None,
                                   tm=cfg["tm_ffn"], tf=cfg["tf"], emit=emit))


def kernel(x_prompt, x_sample, cache_k, cache_v, page_table, norm_a, w_in_a, w_s_a, b_s_a, w_out_a, norm_kv, w_k, w_v, norm_b, w_q, lambda_q1, lambda_k1, lambda_q2, lambda_k2, subln_b, w_o_b, norm_ffn, w_up, w_down, norm_f):
    def layers(a):
        return [(a, l) for l in range(a.shape[0])]

    def gains(g):
        return layers(g.reshape(g.shape[0], 1, g.shape[1]))

    p = dict(norm_a=gains(norm_a), w_s_a=w_s_a, b_s_a=b_s_a, norm_kv=gains(norm_kv[None]),
             norm_b=gains(norm_b),
             lam_vecs=jnp.stack([lambda_q1, lambda_k1, lambda_q2, lambda_k2], axis=1),
             subln_b=subln_b.reshape(subln_b.shape[0], 1, -1), norm_ffn=gains(norm_ffn),
             norm_f=norm_f.reshape(1, -1))
    W = dict(w_in_a=layers(w_in_a), w_out_a=layers(w_out_a), w_k=layers(w_k[None]), w_v=layers(w_v[None]),
             w_q=layers(w_q), w_o_b=layers(w_o_b), w_up=layers(w_up), w_down=layers(w_down))

    B, T, _ = x_prompt.shape
    Bd, Td, _ = x_sample.shape

    Ms = Bd * Td
    cfg_s = dict(tm=Ms, tn=1024, tm_in=Ms, tn_in=1024, tm_gmlp=Ms, tn_gmlp=512, tm_ffn=Ms, tf=512,
                 pps=8, rows=Ms, chunk=Td, z_dtype=F32, q_dtype=F32, fused_gate=False)
    cfg_p = dict(tm=512, tn=2048, tm_in=1024, tn_in=512, tm_gmlp=1024, tn_gmlp=512, tm_ffn=512, tf=1024,
                 tq=512, hps=4, rows=CHUNK, chunk=CHUNK, q_dtype=BF16, fused_gate=True)
    S = _Group(x_sample, PAST_LEN + jnp.arange(Td, dtype=jnp.int32), cfg_s)
    P = _Group(x_prompt, jnp.arange(T, dtype=jnp.int32), cfg_p)

    n_pool = cache_k.shape[0]
    paged = (cache_k.reshape(n_pool, PAGE_SIZE * N_QK, HEAD_DIM),
             cache_v.reshape(n_pool, PAGE_SIZE * N_HEADS, V_DIM), page_table)
    Wb = {name: [] for name in W}
    for l in range(DEPTH):
        _layer(S, l, p, W, Wb=Wb, paged=paged)
    for l in range(DEPTH):
        _layer(P, l, p, Wb)

    return (P.h.reshape(B, T, D_MODEL), S.h.reshape(Bd, Td, D_MODEL),
            P.k_sh.reshape(B, T, N_QK, HEAD_DIM), P.v_sh.reshape(B, T, N_HEADS, V_DIM),
            S.k_sh.reshape(Bd, Td, N_QK, HEAD_DIM), S.v_sh.reshape(Bd, Td, N_HEADS, V_DIM),
            jnp.stack(S.v_rows).reshape(N_A_LAYERS, Bd, Td, D_GATE))
```

```python
import functools
import math

import jax
import jax.numpy as jnp
from jax import lax
from jax.experimental import pallas as pl
from jax.experimental.pallas import tpu as pltpu

D_MODEL = 2048
DEPTH = 4
PAST_LEN = 16384
PAGE_SIZE = 128
N_A_LAYERS = DEPTH // 2
CHUNK = 128
D_GATE = 2 * D_MODEL
N_GROUPS_A = 16
GROUP_DIM_A = D_GATE // N_GROUPS_A
HEAD_DIM = 128
N_HEADS = D_MODEL // (2 * HEAD_DIM)
N_QK = 2 * N_HEADS
V_DIM = 2 * HEAD_DIM
D_FF = 4 * D_MODEL
ROPE_THETA = 10000.0
EPS = 1e-5
NEG_INF = -1e30

LANES = 128
VMEM_LIMIT = 56 * 1024 * 1024
LOG2E = math.log2(math.e)

F32 = jnp.float32
BF16 = jnp.bfloat16


def _params(n_axes):
    return pltpu.CompilerParams(dimension_semantics=("arbitrary",) * n_axes,
                                vmem_limit_bytes=VMEM_LIMIT)


def _rms_rows(x, g):
    return x * lax.rsqrt(jnp.mean(x * x, axis=-1, keepdims=True) + EPS) * g


def _lambda(lam_ref, lam_init):
    a = jnp.sum(lam_ref[0:1, :] * lam_ref[1:2, :], axis=-1, keepdims=True)
    b = jnp.sum(lam_ref[2:3, :] * lam_ref[3:4, :], axis=-1, keepdims=True)
    return jnp.exp(a) - jnp.exp(b) + lam_init


def _mxu_weight(w_ref, wb_ref):
    w = w_ref[...].astype(BF16)
    if wb_ref is not None:
        wb_ref[...] = w
    return w


def _norm_matmul_kernel(*refs, epilogue, scale, emit, dup):
    x_ref, g_ref, w_ref = refs[:3]
    refs = refs[3:]
    if epilogue == "rope":
        cos_ref, sin_ref = refs[:2]
        refs = refs[2:]
    out_refs = refs[:2] if dup else refs[:1]
    wb_ref = refs[len(out_refs)] if emit else None
    xn_ref = refs[-1]

    def put(sl, val):
        for o_ref in out_refs:
            o_ref[:, sl] = val.astype(o_ref.dtype)

    @pl.when(pl.program_id(1) == 0)
    def _():
        xn_ref[...] = _rms_rows(x_ref[...], g_ref[...]).astype(BF16)

    y = jnp.dot(xn_ref[...], _mxu_weight(w_ref, wb_ref), preferred_element_type=F32)
    if epilogue == "gelu":
        put(slice(None), 0.5 * y * (1.0 + lax.erf(y * (2.0 ** -0.5))))
    elif epilogue == "rope":
        cos = cos_ref[...]
        sin = sin_ref[...]
        for h in range(y.shape[1] // HEAD_DIM):
            sl = slice(h * HEAD_DIM, (h + 1) * HEAD_DIM)
            yh = y[:, sl]
            oh = yh * cos + pltpu.roll(yh, HEAD_DIM // 2, 1) * sin
            if scale != 1.0:
                oh = oh * scale
            put(sl, oh)
    else:
        put(slice(None), y)


def _weight_copy_out(emit, M, tm, K, N, bk, bn, index_map):
    if not emit:
        return [], []
    assert M == tm
    return [pl.BlockSpec((bk, bn), index_map)], [jax.ShapeDtypeStruct((K, N), BF16)]


def _norm_matmul(x, gl, wl, *, tm, tn, out_dtype, epilogue=None, rope=None, scale=1.0, emit=False,
                 dup=False):
    assert not (emit and dup)
    g, lg = gl
    w, l = wl
    M, K = x.shape
    N = w.shape[2]
    wb_specs, wb_shapes = _weight_copy_out(emit, M, tm, K, N, K, tn, lambda i, j: (0, j))
    in_specs = [
        pl.BlockSpec((tm, K), lambda i, j: (i, 0)),
        pl.BlockSpec((None, 1, K), lambda i, j: (lg, 0, 0)),
        pl.BlockSpec((None, K, tn), lambda i, j: (l, 0, j)),
    ]
    args = [x, g, w]
    if epilogue == "rope":
        cos, sin = rope
        nb = cos.shape[0] // tm
        in_specs += [pl.BlockSpec((tm, HEAD_DIM), lambda i, j: (i % nb, 0))] * 2
        args += [cos, sin]
    outs = pl.pallas_call(
        functools.partial(_norm_matmul_kernel, epilogue=epilogue, scale=scale, emit=emit, dup=dup),
        grid=(M // tm, N // tn),
        in_specs=in_specs,
        out_specs=[pl.BlockSpec((tm, tn), lambda i, j: (i, j))] * (2 if dup else 1) + wb_specs,
        out_shape=[jax.ShapeDtypeStruct((M, N), out_dtype)]
        + ([jax.ShapeDtypeStruct((M, N), BF16)] if dup else []) + wb_shapes,
        scratch_shapes=[pltpu.VMEM((tm, K), BF16)],
        compiler_params=_params(2),
        name="norm_matmul_" + (epilogue or "plain"),
    )(*args)
    if emit:
        return outs[0], outs[1][None]
    return (outs[0], outs[1]) if dup else outs[0]


def _gmlp_out_kernel(u_ref, v_ref, ws_ref, bs_ref, h_ref, w_ref, o_ref, *refs, rows, chunk, emit):
    wb_ref = refs[0] if emit else None
    gated_ref, wt_ref = refs[-2:]
    tm = u_ref.shape[0]

    @pl.when(pl.program_id(1) == 0)
    def _():
        r = lax.broadcasted_iota(jnp.int32, (rows, rows), 0)
        c = lax.broadcasted_iota(jnp.int32, (rows, rows), 1)
        mask = (r // chunk == c // chunk) & (r >= c)
        for g in range(N_GROUPS_A):
            wt_ref[g] = jnp.where(mask, ws_ref[g], 0.0).astype(BF16)

        def mix(ci, carry):
            r0 = pl.multiple_of(ci * rows, rows)
            for g in range(N_GROUPS_A):
                sl = slice(g * GROUP_DIM_A, (g + 1) * GROUP_DIM_A)
                vg = v_ref[pl.ds(r0, rows), sl].astype(BF16)
                s = jnp.dot(wt_ref[g], vg, preferred_element_type=F32) + bs_ref[:, g:g + 1]
                ug = u_ref[pl.ds(r0, rows), sl].astype(F32)
                gated_ref[pl.ds(r0, rows), sl] = (ug * s).astype(BF16)
            return carry

        lax.fori_loop(0, tm // rows, mix, 0)

    o_ref[...] = h_ref[...] + jnp.dot(gated_ref[...], _mxu_weight(w_ref, wb_ref),
                                      preferred_element_type=F32)


def _gmlp_out(z, ws, bs_t, h, wl, *, tm, tn, rows, chunk, emit=False):
    w_out, l = wl
    M = z.shape[0]
    N = w_out.shape[2]
    wb_specs, wb_shapes = _weight_copy_out(emit, M, tm, D_GATE, N, D_GATE, tn, lambda i, j: (0, j))
    outs = pl.pallas_call(
        functools.partial(_gmlp_out_kernel, rows=rows, chunk=chunk, emit=emit),
        grid=(M // tm, N // tn),
        in_specs=[
            pl.BlockSpec((tm, D_GATE), lambda i, j: (i, 0)),
            pl.BlockSpec((tm, D_GATE), lambda i, j: (i, 1)),
            pl.BlockSpec((N_GROUPS_A, rows, rows), lambda i, j: (0, 0, 0)),
            pl.BlockSpec((rows, N_GROUPS_A), lambda i, j: (0, 0)),
            pl.BlockSpec((tm, tn), lambda i, j: (i, j)),
            pl.BlockSpec((None, D_GATE, tn), lambda i, j: (l, 0, j)),
        ],
        out_specs=[pl.BlockSpec((tm, tn), lambda i, j: (i, j))] + wb_specs,
        out_shape=[jax.ShapeDtypeStruct((M, N), F32)] + wb_shapes,
        scratch_shapes=[pltpu.VMEM((tm, D_GATE), BF16),
                        pltpu.VMEM((N_GROUPS_A, rows, rows), BF16)],
        compiler_params=_params(2),
        name="gmlp_out",
    )(z, z, ws, bs_t, h, w_out)
    return (outs[0], outs[1][None]) if emit else outs[0]


def _gmlp_gate_kernel(x_ref, g_ref, wu_ref, wv_ref, ws_ref, bs_ref, o_ref, xn_ref):
    @pl.when(pl.program_id(1) == 0)
    def _():
        xn_ref[...] = _rms_rows(x_ref[...], g_ref[...]).astype(BF16)

    def gelu(y):
        return 0.5 * y * (1.0 + lax.erf(y * (2.0 ** -0.5)))

    xn = xn_ref[...]
    u = gelu(jnp.dot(xn, wu_ref[...], preferred_element_type=F32))
    v = gelu(jnp.dot(xn, wv_ref[...], preferred_element_type=F32)).astype(BF16)
    r = lax.broadcasted_iota(jnp.int32, (CHUNK, CHUNK), 0)
    c = lax.broadcasted_iota(jnp.int32, (CHUNK, CHUNK), 1)
    for gg in range(ws_ref.shape[0]):
        wt = jnp.where(r >= c, ws_ref[gg], 0.0).astype(BF16)
        b = bs_ref[gg]
        b = jnp.concatenate([b] * (GROUP_DIM_A // LANES), axis=1)
        cs = slice(gg * GROUP_DIM_A, (gg + 1) * GROUP_DIM_A)
        for ci in range(u.shape[0] // CHUNK):
            rs = slice(ci * CHUNK, (ci + 1) * CHUNK)
            s = jnp.dot(wt, v[rs, cs], preferred_element_type=F32) + b
            o_ref[rs, cs] = (u[rs, cs] * s).astype(o_ref.dtype)


def _gmlp_gate(x, gl, wl, ws, bs, *, tm, tn):
    g, lg = gl
    w, l = wl
    M, K = x.shape
    gpt = tn // GROUP_DIM_A
    nt = D_GATE // tn
    bs_rep = jnp.broadcast_to(bs[:, :, None], bs.shape + (LANES,))
    return pl.pallas_call(
        _gmlp_gate_kernel,
        grid=(M // tm, nt),
        in_specs=[
            pl.BlockSpec((tm, K), lambda i, j: (i, 0)),
            pl.BlockSpec((None, 1, K), lambda i, j: (lg, 0, 0)),
            pl.BlockSpec((None, K, tn), lambda i, j: (l, 0, j)),
            pl.BlockSpec((None, K, tn), lambda i, j: (l, 0, nt + j)),
            pl.BlockSpec((gpt, CHUNK, CHUNK), lambda i, j: (j, 0, 0)),
            pl.BlockSpec((gpt, CHUNK, LANES), lambda i, j: (j, 0, 0)),
        ],
        out_specs=pl.BlockSpec((tm, tn), lambda i, j: (i, j)),
        out_shape=jax.ShapeDtypeStruct((M, D_GATE), BF16),
        scratch_shapes=[pltpu.VMEM((tm, K), BF16)],
        compiler_params=_params(2),
        name="gmlp_gate",
    )(x, g, w, w, ws, bs_rep)


def _ffn_kernel(*refs, final_norm, emit):
    x_ref, g_ref, wu_ref, wd_ref = refs[:4]
    refs = refs[4:]
    if final_norm:
        gf_ref = refs[0]
        refs = refs[1:]
    o_ref = refs[0]
    wub_ref, wdb_ref = refs[1:3] if emit else (None, None)
    xn_ref = refs[-1]
    f = pl.program_id(1)

    @pl.when(f == 0)
    def _():
        x = x_ref[...]
        xn_ref[...] = _rms_rows(x, g_ref[...]).astype(BF16)
        o_ref[...] = x

    a = jnp.dot(xn_ref[...], _mxu_weight(wu_ref, wub_ref), preferred_element_type=F32)
    a = jnp.square(jnp.maximum(a, 0.0)).astype(BF16)
    o_ref[...] += jnp.dot(a, _mxu_weight(wd_ref, wdb_ref), preferred_element_type=F32)

    if final_norm:
        @pl.when(f == pl.num_programs(1) - 1)
        def _():
            o_ref[...] = _rms_rows(o_ref[...], gf_ref[...])


def _ffn(x, gl, wul, wdl, g_final, *, tm, tf, emit=False):
    g, l = gl
    w_up, lu = wul
    w_down, ld = wdl
    M, K = x.shape
    F = w_up.shape[2]
    final_norm = g_final is not None
    in_specs = [
        pl.BlockSpec((tm, K), lambda i, f: (i, 0)),
        pl.BlockSpec((None, 1, K), lambda i, f: (l, 0, 0)),
        pl.BlockSpec((None, K, tf), lambda i, f: (lu, 0, f)),
        pl.BlockSpec((None, tf, K), lambda i, f: (ld, f, 0)),
    ]
    ub_specs, ub_shapes = _weight_copy_out(emit, M, tm, K, F, K, tf, lambda i, f: (0, f))
    db_specs, db_shapes = _weight_copy_out(emit, M, tm, F, K, tf, K, lambda i, f: (f, 0))
    args = [x, g, w_up, w_down]
    if final_norm:
        in_specs.append(pl.BlockSpec((1, K), lambda i, f: (0, 0)))
        args.append(g_final)
    outs = pl.pallas_call(
        functools.partial(_ffn_kernel, final_norm=final_norm, emit=emit),
        grid=(M // tm, F // tf),
        in_specs=in_specs,
        out_specs=[pl.BlockSpec((tm, K), lambda i, f: (i, 0))] + ub_specs + db_specs,
        out_shape=[jax.ShapeDtypeStruct((M, K), F32)] + ub_shapes + db_shapes,
        scratch_shapes=[pltpu.VMEM((tm, K), BF16)],
        compiler_params=_params(2),
        name="ffn",
    )(*args)
    return (outs[0], outs[1][None], outs[2][None]) if emit else outs[0]


def _head_rows(ref, h, n_heads):
    return ref[pl.ds(h, PAGE_SIZE, stride=n_heads), :].astype(BF16)


def _page_k(k_ref):
    return jnp.concatenate([_head_rows(k_ref, h, N_QK) for h in range(N_QK)], axis=1)


def _matmul_res_kernel(x_ref, w_ref, h_ref, o_ref, wb_ref=None):
    o_ref[...] = h_ref[...] + jnp.dot(x_ref[...], _mxu_weight(w_ref, wb_ref), preferred_element_type=F32)


def _matmul_res(x, wl, h, *, tm, tn, emit=False):
    w, l = wl
    M, K = x.shape
    N = w.shape[2]
    wb_specs, wb_shapes = _weight_copy_out(emit, M, tm, K, N, K, tn, lambda i, j: (0, j))
    outs = pl.pallas_call(
        _matmul_res_kernel,
        grid=(M // tm, N // tn),
        in_specs=[
            pl.BlockSpec((tm, K), lambda i, j: (i, 0)),
            pl.BlockSpec((None, K, tn), lambda i, j: (l, 0, j)),
            pl.BlockSpec((tm, tn), lambda i, j: (i, j)),
        ],
        out_specs=[pl.BlockSpec((tm, tn), lambda i, j: (i, j))] + wb_specs,
        out_shape=[jax.ShapeDtypeStruct((M, N), F32)] + wb_shapes,
        compiler_params=_params(2),
        name="matmul_res",
    )(x, w, h)
    return (outs[0], outs[1][None]) if emit else outs[0]


def _lane_chunks(s):
    return [s[:, c * LANES:(c + 1) * LANES] for c in range(s.shape[1] // LANES)]


def _chunk_max(chunks):
    m = chunks[0]
    for c in chunks[1:]:
        m = jnp.maximum(m, c)
    return m


def _softmax_step(s_list, m_ref, l_ref, m_blk=None):
    cols = [_lane_chunks(s) for s in s_list]
    if m_blk is None:
        m_blk = _chunk_max([c for cs in cols for c in cs])
    m_prev = m_ref[...]
    m_new = jnp.maximum(m_prev, jnp.max(m_blk, axis=-1, keepdims=True))
    alpha = jnp.exp2(m_prev - m_new)
    probs = [[jnp.exp2(c - m_new) for c in cs] for cs in cols]
    flat = [p for ps in probs for p in ps]
    l_blk = flat[0]
    for p in flat[1:]:
        l_blk = l_blk + p
    l_ref[...] = alpha * l_ref[...] + jnp.sum(l_blk, axis=-1, keepdims=True)
    m_ref[...] = m_new
    return [jnp.concatenate(ps, axis=1).astype(BF16) if len(ps) > 1 else ps[0].astype(BF16)
            for ps in probs], alpha


def _head_out(o1, o2, lam, subln, lam_init):
    d = o1 - lam * o2
    return _rms_rows(d, subln) * (1.0 - lam_init)


def _attn_prompt_kernel(qi_tab, kj_tab, q0_ref, k0_ref, qn_ref, kn_ref, v_ref, lam_ref, subln_ref, o_ref,
                        m_ref, l_ref, acc_ref, sa_ref, sb_ref, ma_ref, mb_ref, p_ref, a_ref, *, lam_init):
    step = pl.program_id(2)
    n_steps = pl.num_programs(2)
    qi = qi_tab[step]
    kj = kj_tab[step]
    nxt = jnp.where(step + 1 == n_steps, 0, step + 1)
    next_masked = qi_tab[nxt] == kj_tab[nxt]
    tq = qn_ref.shape[1]
    tk = kn_ref.shape[1]
    n_sub = m_ref.shape[0]

    def scores(q_ref, k_ref, bufs, masked):
        s_ref, mx_ref = bufs
        for sub in range(n_sub):
            sl = slice(sub * HEAD_DIM, (sub + 1) * HEAD_DIM)
            q = q_ref[0, :, sl]
            k = k_ref[0, :, sl].astype(BF16)
            s = lax.dot_general(q, k, (((1,), (1,)), ((), ())), preferred_element_type=F32)
            if masked:
                r = lax.broadcasted_iota(jnp.int32, (tq, tk), 0)
                c = lax.broadcasted_iota(jnp.int32, (tq, tk), 1)
                s = jnp.where(c <= r, s, NEG_INF)
            s_ref[sub] = s
            mx_ref[sub] = _chunk_max(_lane_chunks(s))

    def accumulate(bufs):
        s_ref, mx_ref = bufs
        for sub in range(n_sub):
            p, alpha = _softmax_step([s_ref[sub]], m_ref.at[sub], l_ref.at[sub], mx_ref[sub])
            p_ref[sub] = p[0]
            a_ref[sub] = alpha
        for sub in range(n_sub):
            vs = slice((sub // 2) * V_DIM, (sub // 2 + 1) * V_DIM)
            alpha = a_ref[sub]
            acc_ref[sub] = (jnp.concatenate([alpha, alpha], axis=1) * acc_ref[sub]
                            + jnp.dot(p_ref[sub], v_ref[0, :, vs].astype(BF16), preferred_element_type=F32))

    @pl.when((pl.program_id(0) == 0) & (pl.program_id(1) == 0) & (step == 0))
    def _():
        scores(q0_ref, k0_ref, (sa_ref, ma_ref), True)

    @pl.when(kj == 0)
    def _():
        m_ref[...] = jnp.full(m_ref.shape, NEG_INF, F32)
        l_ref[...] = jnp.zeros(l_ref.shape, F32)
        acc_ref[...] = jnp.zeros(acc_ref.shape, F32)

    for parity, (cur_ref, nxt_ref) in enumerate((((sa_ref, ma_ref), (sb_ref, mb_ref)),
                                                 ((sb_ref, mb_ref), (sa_ref, ma_ref)))):
        for masked in (False, True):
            @pl.when((step % 2 == parity) & (next_masked == masked))
            def _(cur_ref=cur_ref, nxt_ref=nxt_ref, masked=masked):
                scores(qn_ref, kn_ref, nxt_ref, masked)
                accumulate(cur_ref)

    @pl.when(kj == qi)
    def _():
        lam = _lambda(lam_ref, lam_init)
        for hv in range(n_sub // 2):
            inv1 = 1.0 / l_ref[2 * hv]
            inv2 = 1.0 / l_ref[2 * hv + 1]
            o1 = acc_ref[2 * hv] * jnp.concatenate([inv1, inv1], axis=1)
            o2 = acc_ref[2 * hv + 1] * jnp.concatenate([inv2, inv2], axis=1)
            o_ref[0, :, hv * V_DIM:(hv + 1) * V_DIM] = _head_out(
                o1, o2, lam, subln_ref[...], lam_init).astype(o_ref.dtype)


def _attn_prompt(q, k, v, lam_vecs, subln, l, lam_init, *, tq, heads_per_step):
    B, T, _ = q.shape
    nq = T // tq
    pairs = [(i, j) for i in range(nq) for j in range(i + 1)]
    qi_tab = jnp.asarray([p[0] for p in pairs], jnp.int32)
    kj_tab = jnp.asarray([p[1] for p in pairs], jnp.int32)
    hw = heads_per_step * V_DIM
    n_sub = 2 * heads_per_step
    n_hg = N_HEADS // heads_per_step
    n_steps = len(pairs)
    assert n_steps % 2 == 0

    def next_block(tab):
        def index_map(b, h, s, qt, kt):
            wrap_s = (s + 1 == n_steps).astype(jnp.int32)
            s_n = (s + 1) * (1 - wrap_s)
            wrap_h = ((h + wrap_s) == n_hg).astype(jnp.int32)
            h_n = (h + wrap_s) * (1 - wrap_h)
            b_n = jnp.minimum(b + wrap_h, B - 1)
            return (b_n, (qt if tab == "q" else kt)[s_n], h_n)
        return index_map

    grid_spec = pltpu.PrefetchScalarGridSpec(
        num_scalar_prefetch=2,
        grid=(B, n_hg, n_steps),
        in_specs=[
            pl.BlockSpec((1, tq, hw), lambda b, h, s, qt, kt: (0, 0, 0)),
            pl.BlockSpec((1, tq, hw), lambda b, h, s, qt, kt: (0, 0, 0)),
            pl.BlockSpec((1, tq, hw), next_block("q")),
            pl.BlockSpec((1, tq, hw), next_block("k")),
            pl.BlockSpec((1, tq, hw), lambda b, h, s, qt, kt: (b, kt[s], h)),
            pl.BlockSpec((None, 4, HEAD_DIM), lambda b, h, s, qt, kt: (l, 0, 0)),
            pl.BlockSpec((None, 1, V_DIM), lambda b, h, s, qt, kt: (l, 0, 0)),
        ],
        out_specs=pl.BlockSpec((1, tq, hw), lambda b, h, s, qt, kt: (b, qt[s], h)),
        scratch_shapes=[pltpu.VMEM((n_sub, tq, LANES), F32), pltpu.VMEM((n_sub, tq, LANES), F32),
                        pltpu.VMEM((n_sub, tq, V_DIM), F32),
                        pltpu.VMEM((n_sub, tq, tq), F32), pltpu.VMEM((n_sub, tq, tq), F32),
                        pltpu.VMEM((n_sub, tq, LANES), F32), pltpu.VMEM((n_sub, tq, LANES), F32),
                        pltpu.VMEM((n_sub, tq, tq), BF16), pltpu.VMEM((n_sub, tq, LANES), F32)],
    )
    return pl.pallas_call(
        functools.partial(_attn_prompt_kernel, lam_init=lam_init),
        grid_spec=grid_spec,
        out_shape=jax.ShapeDtypeStruct((B, T, N_HEADS * V_DIM), BF16),
        compiler_params=_params(3),
        name="attn_prompt",
    )(qi_tab, kj_tab, q, k, q, k, v, lam_vecs, subln)


def _attn_sample_step(load_tiles, q_ref, kn_ref, vn_ref, lam_ref, subln_ref, o_ref,
                      qbd_ref, m_ref, l_ref, acc_ref, lam_init):
    p_idx = pl.program_id(1)
    tq = q_ref.shape[1]
    rows = N_QK * tq
    d_all = N_QK * HEAD_DIM

    @pl.when(p_idx == 0)
    def _():
        m_ref[...] = jnp.full(m_ref.shape, NEG_INF, F32)
        l_ref[...] = jnp.zeros(l_ref.shape, F32)
        acc_ref[...] = jnp.zeros(acc_ref.shape, F32)
        qt = jnp.concatenate([q_ref[0]] * N_QK, axis=0)
        r = lax.broadcasted_iota(jnp.int32, (rows, d_all), 0)
        c = lax.broadcasted_iota(jnp.int32, (rows, d_all), 1)
        qbd_ref[...] = jnp.where(r // tq == c // HEAD_DIM, qt, 0.0).astype(BF16)

    def update(k_list, v_list, mask):
        qbd = qbd_ref[...]
        s_list = []
        for kp in k_list:
            s = lax.dot_general(qbd, kp, (((1,), (1,)), ((), ())), preferred_element_type=F32)
            if mask is not None:
                s = jnp.where(mask, s, NEG_INF)
            s_list.append(s)
        p_list, alpha = _softmax_step(s_list, m_ref, l_ref)
        for hv in range(N_HEADS):
            rs = slice(hv * 2 * tq, (hv + 1) * 2 * tq)
            pv = None
            for p, v_heads in zip(p_list, v_list):
                t = jnp.dot(p[rs, :], v_heads[hv], preferred_element_type=F32)
                pv = t if pv is None else pv + t
            a = alpha[rs, :]
            acc_ref[rs, :] = jnp.concatenate([a, a], axis=1) * acc_ref[rs, :] + pv

    update(*load_tiles(), None)

    @pl.when(p_idx == pl.num_programs(1) - 1)
    def _():
        pad = jnp.zeros((PAGE_SIZE - tq, d_all), F32)
        kn = jnp.concatenate([kn_ref[0], pad], axis=0).astype(BF16)
        vn = jnp.concatenate([vn_ref[0], pad], axis=0).astype(BF16)
        r = lax.broadcasted_iota(jnp.int32, (rows, PAGE_SIZE), 0)
        c = lax.broadcasted_iota(jnp.int32, (rows, PAGE_SIZE), 1)
        update([kn], [[vn[:, hv * V_DIM:(hv + 1) * V_DIM] for hv in range(N_HEADS)]], c <= r % tq)
        lam = _lambda(lam_ref, lam_init)
        inv = 1.0 / l_ref[...]
        o = acc_ref[...] * jnp.concatenate([inv, inv], axis=1)
        for hv in range(N_HEADS):
            o1 = o[hv * 2 * tq:hv * 2 * tq + tq, :]
            o2 = o[hv * 2 * tq + tq:(hv + 1) * 2 * tq, :]
            o_ref[0, :, hv * V_DIM:(hv + 1) * V_DIM] = _head_out(
                o1, o2, lam, subln_ref[...], lam_init).astype(o_ref.dtype)


def _attn_sample_kernel(pt_ref, q_ref, *refs, pages, lam_init):
    k_refs = refs[:pages]
    v_refs = refs[pages:3 * pages]

    def load_tiles():
        return ([_page_k(k) for k in k_refs],
                [[jnp.concatenate([_head_rows(r, hv, N_HEADS) for r in v_refs[2 * i:2 * i + 2]], axis=1)
                  for hv in range(N_HEADS)] for i in range(pages)])

    _attn_sample_step(load_tiles, q_ref, *refs[3 * pages:], lam_init)


def _attn_sample(q, cache_k, cache_v, page_table, k_new, v_new, lam_vecs, subln, l, lam_init, *,
                 pages_per_step):
    Bd, Tq, D = q.shape
    n_pages = page_table.shape[1]
    pps = pages_per_step

    def page_map(i, half):
        return lambda b, p, pt: (pt[b * n_pages + p * pps + i], 0, half)

    k_specs = [pl.BlockSpec((None, PAGE_SIZE * N_QK, HEAD_DIM), page_map(i, 0)) for i in range(pps)]
    v_specs = [pl.BlockSpec((None, PAGE_SIZE * N_HEADS, LANES), page_map(i, half))
               for i in range(pps) for half in range(V_DIM // LANES)]
    row_spec = pl.BlockSpec((1, Tq, D), lambda b, p, pt: (b, 0, 0))
    grid_spec = pltpu.PrefetchScalarGridSpec(
        num_scalar_prefetch=1,
        grid=(Bd, n_pages // pps),
        in_specs=[row_spec] + k_specs + v_specs + [
            row_spec, row_spec,
            pl.BlockSpec((None, 4, HEAD_DIM), lambda b, p, pt: (l, 0, 0)),
            pl.BlockSpec((None, 1, V_DIM), lambda b, p, pt: (l, 0, 0)),
        ],
        out_specs=row_spec,
        scratch_shapes=[pltpu.VMEM((N_QK * Tq, D), BF16),
                        pltpu.VMEM((N_QK * Tq, LANES), F32), pltpu.VMEM((N_QK * Tq, LANES), F32),
                        pltpu.VMEM((N_QK * Tq, V_DIM), F32)],
    )
    return pl.pallas_call(
        functools.partial(_attn_sample_kernel, pages=pps, lam_init=lam_init),
        grid_spec=grid_spec,
        out_shape=jax.ShapeDtypeStruct((Bd, Tq, D), BF16),
        compiler_params=_params(2),
        name="attn_sample",
    )(page_table.reshape(-1), q, *([cache_k] * pps), *([cache_v] * (2 * pps)), k_new, v_new, lam_vecs, subln)


def _rope_tables(pos):
    half = HEAD_DIM // 2
    inv = ROPE_THETA ** (-jnp.arange(half, dtype=F32) / half)
    ang = pos.astype(F32)[:, None] * inv[None, :]
    cos = jnp.cos(ang)
    sin = jnp.sin(ang)
    return jnp.concatenate([cos, cos], axis=-1), jnp.concatenate([-sin, sin], axis=-1)


class _Group:
    def __init__(self, x, pos, cfg):
        self.B, self.T, _ = x.shape
        self.h = x.reshape(self.B * self.T, D_MODEL)
        self.cfg = cfg
        cos, sin = _rope_tables(pos)
        if self.T < cfg["tm"]:
            cos = jnp.tile(cos, (cfg["tm"] // self.T, 1))
            sin = jnp.tile(sin, (cfg["tm"] // self.T, 1))
        self.rope = (cos, sin)
        self.v_rows = []
        self.k_sh = self.v_sh = self.k_mxu = self.v_mxu = None


def _layer(G, l, p, W, Wb=None, paged=None):
    cfg, B, T, h = G.cfg, G.B, G.T, G.h
    M = B * T
    tm, rows, chunk = cfg["tm"], cfg["rows"], cfg["chunk"]
    cos, sin = G.rope
    emit = Wb is not None

    def keep(name, outs):
        if not emit:
            return outs
        for n, wb in zip(name.split(","), outs[1:]):
            Wb[n].append((wb, 0))
        return outs[0]

    if l < N_A_LAYERS and cfg["fused_gate"]:
        assert not emit and chunk == CHUNK
        gated = _gmlp_gate(h, p["norm_a"][l], W["w_in_a"][l], p["w_s_a"][l], p["b_s_a"][l],
                           tm=cfg["tm_in"], tn=cfg["tn_in"])
        h = _matmul_res(gated, W["w_out_a"][l], h, tm=cfg["tm_gmlp"], tn=cfg["tn_gmlp"])
    elif l < N_A_LAYERS:
        z = keep("w_in_a", _norm_matmul(h, p["norm_a"][l], W["w_in_a"][l], tm=cfg["tm_in"], tn=cfg["tn_in"],
                                        out_dtype=cfg["z_dtype"], epilogue="gelu", emit=emit))
        ws = p["w_s_a"][l][:, :chunk, :chunk]
        bs_t = p["b_s_a"][l][:, :chunk].T
        if rows > chunk:
            ws = jnp.tile(ws, (1, rows // chunk, rows // chunk))
            bs_t = jnp.tile(bs_t, (rows // chunk, 1))
        h = keep("w_out_a", _gmlp_out(z, ws, bs_t, h, W["w_out_a"][l], tm=cfg["tm_gmlp"], tn=cfg["tn_gmlp"],
                                      rows=rows, chunk=chunk, emit=emit))
        G.v_rows.append(z[:, D_GATE:])
    else:
        if l == N_A_LAYERS:
            dup = paged is None
            k_sh = keep("w_k", _norm_matmul(h, p["norm_kv"][0], W["w_k"][0], tm=tm, tn=cfg["tn"], out_dtype=F32,
                                            epilogue="rope", rope=(cos, sin), emit=emit, dup=dup))
            v_sh = keep("w_v", _norm_matmul(h, p["norm_kv"][0], W["w_v"][0], tm=tm, tn=cfg["tn"], out_dtype=F32,
                                            emit=emit, dup=dup))
            if dup:
                (k_sh, G.k_mxu), (v_sh, G.v_mxu) = k_sh, v_sh
            G.k_sh, G.v_sh = k_sh, v_sh
        j = l - N_A_LAYERS
        lam_init = 0.8 - 0.6 * math.exp(-0.3 * l)
        q = keep("w_q", _norm_matmul(h, p["norm_b"][j], W["w_q"][j], tm=tm, tn=cfg["tn"],
                                     out_dtype=cfg["q_dtype"], epilogue="rope", rope=(cos, sin),
                                     scale=HEAD_DIM ** -0.5 * LOG2E, emit=emit))
        if paged is None:
            o = _attn_prompt(q.reshape(B, T, -1), G.k_mxu.reshape(B, T, -1), G.v_mxu.reshape(B, T, -1),
                             p["lam_vecs"], p["subln_b"], j, lam_init, tq=cfg["tq"],
                             heads_per_step=cfg["hps"])
        else:
            o = _attn_sample(q.reshape(B, T, -1), *paged, G.k_sh.reshape(B, T, -1), G.v_sh.reshape(B, T, -1),
                             p["lam_vecs"], p["subln_b"], j, lam_init, pages_per_step=cfg["pps"])
        h = keep("w_o_b", _matmul_res(o.reshape(M, -1), W["w_o_b"][j], h, tm=tm, tn=cfg["tn"], emit=emit))
    G.h = keep("w_up,w_down", _ffn(h, p["norm_ffn"][l], W["w_up"][l], W["w_down"][l],
                                   p["norm_f"] if l == DEPTH - 1 else None,
                                   tm=cfg["tm_ffn"], tf=cfg["tf"], emit=emit))


def kernel(x_prompt, x_sample, cache_k, cache_v, page_table, norm_a, w_in_a, w_s_a, b_s_a, w_out_a, norm_kv, w_k, w_v, norm_b, w_q, lambda_q1, lambda_k1, lambda_q2, lambda_k2, subln_b, w_o_b, norm_ffn, w_up, w_down, norm_f):
    def layers(a):
        return [(a, l) for l in range(a.shape[0])]

    def gains(g):
        return layers(g.reshape(g.shape[0], 1, g.shape[1]))

    p = dict(norm_a=gains(norm_a), w_s_a=w_s_a, b_s_a=b_s_a, norm_kv=gains(norm_kv[None]),
             norm_b=gains(norm_b),
             lam_vecs=jnp.stack([lambda_q1, lambda_k1, lambda_q2, lambda_k2], axis=1),
             subln_b=subln_b.reshape(subln_b.shape[0], 1, -1), norm_ffn=gains(norm_ffn),
             norm_f=norm_f.reshape(1, -1))
    W = dict(w_in_a=layers(w_in_a), w_out_a=layers(w_out_a), w_k=layers(w_k[None]), w_v=layers(w_v[None]),
             w_q=layers(w_q), w_o_b=layers(w_o_b), w_up=layers(w_up), w_down=layers(w_down))

    B, T, _ = x_prompt.shape
    Bd, Td, _ = x_sample.shape

    Ms = Bd * Td
    cfg_s = dict(tm=Ms, tn=1024, tm_in=Ms, tn_in=1024, tm_gmlp=Ms, tn_gmlp=512, tm_ffn=Ms, tf=512,
                 pps=8, rows=Ms, chunk=Td, z_dtype=F32, q_dtype=F32, fused_gate=False)
    cfg_p = dict(tm=512, tn=2048, tm_in=1024, tn_in=1024, tm_gmlp=1024, tn_gmlp=512, tm_ffn=512, tf=1024,
                 tq=512, hps=4, rows=CHUNK, chunk=CHUNK, q_dtype=BF16, fused_gate=True)
    S = _Group(x_sample, PAST_LEN + jnp.arange(Td, dtype=jnp.int32), cfg_s)
    P = _Group(x_prompt, jnp.arange(T, dtype=jnp.int32), cfg_p)

    n_pool = cache_k.shape[0]
    paged = (cache_k.reshape(n_pool, PAGE_SIZE * N_QK, HEAD_DIM),
             cache_v.reshape(n_pool, PAGE_SIZE * N_HEADS, V_DIM), page_table)
    Wb = {name: [] for name in W}
    for l in range(DEPTH):
        _layer(S, l, p, W, Wb=Wb, paged=paged)
    for l in range(DEPTH):
        _layer(P, l, p, Wb)

    return (P.h.reshape(B, T, D_MODEL), S.h.reshape(Bd, Td, D_MODEL),
            P.k_sh.reshape(B, T, N_QK, HEAD_DIM), P.v_sh.reshape(B, T, N_HEADS, V_DIM),
            S.k_sh.reshape(Bd, Td, N_QK, HEAD_DIM), S.v_sh.reshape(Bd, Td, N_HEADS, V_DIM),
            jnp.stack(S.v_rows).reshape(N_A_LAYERS, Bd, Td, D_GATE))
```

```python
import functools
import math

import jax
import jax.numpy as jnp
from jax import lax
from jax.experimental import pallas as pl
from jax.experimental.pallas import tpu as pltpu

D_MODEL = 2048
DEPTH = 4
PAST_LEN = 16384
PAGE_SIZE = 128
N_A_LAYERS = DEPTH // 2
CHUNK = 128
D_GATE = 2 * D_MODEL
N_GROUPS_A = 16
GROUP_DIM_A = D_GATE // N_GROUPS_A
HEAD_DIM = 128
N_HEADS = D_MODEL // (2 * HEAD_DIM)
N_QK = 2 * N_HEADS
V_DIM = 2 * HEAD_DIM
D_FF = 4 * D_MODEL
ROPE_THETA = 10000.0
EPS = 1e-5
NEG_INF = -1e30

LANES = 128
VMEM_LIMIT = 60 * 1024 * 1024
LOG2E = math.log2(math.e)

F32 = jnp.float32
BF16 = jnp.bfloat16


def _params(n_axes):
    return pltpu.CompilerParams(dimension_semantics=("arbitrary",) * n_axes,
                                vmem_limit_bytes=VMEM_LIMIT)


def _rms_rows(x, g):
    return x * lax.rsqrt(jnp.mean(x * x, axis=-1, keepdims=True) + EPS) * g


def _lambda(lam_ref, lam_init):
    a = jnp.sum(lam_ref[0:1, :] * lam_ref[1:2, :], axis=-1, keepdims=True)
    b = jnp.sum(lam_ref[2:3, :] * lam_ref[3:4, :], axis=-1, keepdims=True)
    return jnp.exp(a) - jnp.exp(b) + lam_init


def _mxu_weight(w_ref, wb_ref):
    w = w_ref[...].astype(BF16)
    if wb_ref is not None:
        wb_ref[...] = w
    return w


def _norm_matmul_kernel(*refs, epilogue, scale, emit, dup):
    x_ref, g_ref, w_ref = refs[:3]
    refs = refs[3:]
    if epilogue == "rope":
        cos_ref, sin_ref = refs[:2]
        refs = refs[2:]
    out_refs = refs[:2] if dup else refs[:1]
    wb_ref = refs[len(out_refs)] if emit else None
    xn_ref = refs[-1]

    def put(sl, val):
        for o_ref in out_refs:
            o_ref[:, sl] = val.astype(o_ref.dtype)

    @pl.when(pl.program_id(1) == 0)
    def _():
        xn_ref[...] = _rms_rows(x_ref[...], g_ref[...]).astype(BF16)

    y = jnp.dot(xn_ref[...], _mxu_weight(w_ref, wb_ref), preferred_element_type=F32)
    if epilogue == "gelu":
        put(slice(None), 0.5 * y * (1.0 + lax.erf(y * (2.0 ** -0.5))))
    elif epilogue == "rope":
        cos = cos_ref[...]
        sin = sin_ref[...]
        for h in range(y.shape[1] // HEAD_DIM):
            sl = slice(h * HEAD_DIM, (h + 1) * HEAD_DIM)
            yh = y[:, sl]
            oh = yh * cos + pltpu.roll(yh, HEAD_DIM // 2, 1) * sin
            if scale != 1.0:
                oh = oh * scale
            put(sl, oh)
    else:
        put(slice(None), y)


def _weight_copy_out(emit, M, tm, K, N, bk, bn, index_map):
    if not emit:
        return [], []
    assert M == tm
    return [pl.BlockSpec((bk, bn), index_map)], [jax.ShapeDtypeStruct((K, N), BF16)]


def _norm_matmul(x, gl, wl, *, tm, tn, out_dtype, epilogue=None, rope=None, scale=1.0, emit=False,
                 dup=False):
    assert not (emit and dup)
    g, lg = gl
    w, l = wl
    M, K = x.shape
    N = w.shape[2]
    wb_specs, wb_shapes = _weight_copy_out(emit, M, tm, K, N, K, tn, lambda i, j: (0, j))
    in_specs = [
        pl.BlockSpec((tm, K), lambda i, j: (i, 0)),
        pl.BlockSpec((None, 1, K), lambda i, j: (lg, 0, 0)),
        pl.BlockSpec((None, K, tn), lambda i, j: (l, 0, j)),
    ]
    args = [x, g, w]
    if epilogue == "rope":
        cos, sin = rope
        nb = cos.shape[0] // tm
        in_specs += [pl.BlockSpec((tm, HEAD_DIM), lambda i, j: (i % nb, 0))] * 2
        args += [cos, sin]
    outs = pl.pallas_call(
        functools.partial(_norm_matmul_kernel, epilogue=epilogue, scale=scale, emit=emit, dup=dup),
        grid=(M // tm, N // tn),
        in_specs=in_specs,
        out_specs=[pl.BlockSpec((tm, tn), lambda i, j: (i, j))] * (2 if dup else 1) + wb_specs,
        out_shape=[jax.ShapeDtypeStruct((M, N), out_dtype)]
        + ([jax.ShapeDtypeStruct((M, N), BF16)] if dup else []) + wb_shapes,
        scratch_shapes=[pltpu.VMEM((tm, K), BF16)],
        compiler_params=_params(2),
        name="norm_matmul_" + (epilogue or "plain"),
    )(*args)
    if emit:
        return outs[0], outs[1][None]
    return (outs[0], outs[1]) if dup else outs[0]


def _gmlp_out_kernel(u_ref, v_ref, ws_ref, bs_ref, h_ref, w_ref, o_ref, *refs, rows, chunk, emit):
    wb_ref = refs[0] if emit else None
    gated_ref, wt_ref = refs[-2:]
    tm = u_ref.shape[0]

    @pl.when(pl.program_id(1) == 0)
    def _():
        r = lax.broadcasted_iota(jnp.int32, (rows, rows), 0)
        c = lax.broadcasted_iota(jnp.int32, (rows, rows), 1)
        mask = (r // chunk == c // chunk) & (r >= c)
        for g in range(N_GROUPS_A):
            wt_ref[g] = jnp.where(mask, ws_ref[g], 0.0).astype(BF16)

        def mix(ci, carry):
            r0 = pl.multiple_of(ci * rows, rows)
            for g in range(N_GROUPS_A):
                sl = slice(g * GROUP_DIM_A, (g + 1) * GROUP_DIM_A)
                vg = v_ref[pl.ds(r0, rows), sl].astype(BF16)
                s = jnp.dot(wt_ref[g], vg, preferred_element_type=F32) + bs_ref[:, g:g + 1]
                ug = u_ref[pl.ds(r0, rows), sl].astype(F32)
                gated_ref[pl.ds(r0, rows), sl] = (ug * s).astype(BF16)
            return carry

        lax.fori_loop(0, tm // rows, mix, 0)

    o_ref[...] = h_ref[...] + jnp.dot(gated_ref[...], _mxu_weight(w_ref, wb_ref),
                                      preferred_element_type=F32)


def _gmlp_out(z, ws, bs_t, h, wl, *, tm, tn, rows, chunk, emit=False):
    w_out, l = wl
    M = z.shape[0]
    N = w_out.shape[2]
    wb_specs, wb_shapes = _weight_copy_out(emit, M, tm, D_GATE, N, D_GATE, tn, lambda i, j: (0, j))
    outs = pl.pallas_call(
        functools.partial(_gmlp_out_kernel, rows=rows, chunk=chunk, emit=emit),
        grid=(M // tm, N // tn),
        in_specs=[
            pl.BlockSpec((tm, D_GATE), lambda i, j: (i, 0)),
            pl.BlockSpec((tm, D_GATE), lambda i, j: (i, 1)),
            pl.BlockSpec((N_GROUPS_A, rows, rows), lambda i, j: (0, 0, 0)),
            pl.BlockSpec((rows, N_GROUPS_A), lambda i, j: (0, 0)),
            pl.BlockSpec((tm, tn), lambda i, j: (i, j)),
            pl.BlockSpec((None, D_GATE, tn), lambda i, j: (l, 0, j)),
        ],
        out_specs=[pl.BlockSpec((tm, tn), lambda i, j: (i, j))] + wb_specs,
        out_shape=[jax.ShapeDtypeStruct((M, N), F32)] + wb_shapes,
        scratch_shapes=[pltpu.VMEM((tm, D_GATE), BF16),
                        pltpu.VMEM((N_GROUPS_A, rows, rows), BF16)],
        compiler_params=_params(2),
        name="gmlp_out",
    )(z, z, ws, bs_t, h, w_out)
    return (outs[0], outs[1][None]) if emit else outs[0]


def _gmlp_gate_kernel(x_ref, g_ref, wu_ref, wv_ref, ws_ref, bs_ref, o_ref, xn_ref):
    @pl.when(pl.program_id(1) == 0)
    def _():
        xn_ref[...] = _rms_rows(x_ref[...], g_ref[...]).astype(BF16)

    def gelu(y):
        return 0.5 * y * (1.0 + lax.erf(y * (2.0 ** -0.5)))

    xn = xn_ref[...]
    u = gelu(jnp.dot(xn, wu_ref[...], preferred_element_type=F32))
    v = gelu(jnp.dot(xn, wv_ref[...], preferred_element_type=F32)).astype(BF16)
    r = lax.broadcasted_iota(jnp.int32, (CHUNK, CHUNK), 0)
    c = lax.broadcasted_iota(jnp.int32, (CHUNK, CHUNK), 1)
    for gg in range(ws_ref.shape[0]):
        wt = jnp.where(r >= c, ws_ref[gg], 0.0).astype(BF16)
        b = bs_ref[gg]
        b = jnp.concatenate([b] * (GROUP_DIM_A // LANES), axis=1)
        cs = slice(gg * GROUP_DIM_A, (gg + 1) * GROUP_DIM_A)
        for ci in range(u.shape[0] // CHUNK):
            rs = slice(ci * CHUNK, (ci + 1) * CHUNK)
            s = jnp.dot(wt, v[rs, cs], preferred_element_type=F32) + b
            o_ref[rs, cs] = (u[rs, cs] * s).astype(o_ref.dtype)


def _gmlp_gate(x, gl, wl, ws, bs, *, tm, tn):
    g, lg = gl
    w, l = wl
    M, K = x.shape
    gpt = tn // GROUP_DIM_A
    nt = D_GATE // tn
    bs_rep = jnp.broadcast_to(bs[:, :, None], bs.shape + (LANES,))
    return pl.pallas_call(
        _gmlp_gate_kernel,
        grid=(M // tm, nt),
        in_specs=[
            pl.BlockSpec((tm, K), lambda i, j: (i, 0)),
            pl.BlockSpec((None, 1, K), lambda i, j: (lg, 0, 0)),
            pl.BlockSpec((None, K, tn), lambda i, j: (l, 0, j)),
            pl.BlockSpec((None, K, tn), lambda i, j: (l, 0, nt + j)),
            pl.BlockSpec((gpt, CHUNK, CHUNK), lambda i, j: (j, 0, 0)),
            pl.BlockSpec((gpt, CHUNK, LANES), lambda i, j: (j, 0, 0)),
        ],
        out_specs=pl.BlockSpec((tm, tn), lambda i, j: (i, j)),
        out_shape=jax.ShapeDtypeStruct((M, D_GATE), BF16),
        scratch_shapes=[pltpu.VMEM((tm, K), BF16)],
        compiler_params=_params(2),
        name="gmlp_gate",
    )(x, g, w, w, ws, bs_rep)


def _ffn_kernel(*refs, final_norm, emit):
    x_ref, g_ref, wu_ref, wd_ref = refs[:4]
    refs = refs[4:]
    if final_norm:
        gf_ref = refs[0]
        refs = refs[1:]
    o_ref = refs[0]
    wub_ref, wdb_ref = refs[1:3] if emit else (None, None)
    xn_ref = refs[-1]
    f = pl.program_id(1)

    @pl.when(f == 0)
    def _():
        x = x_ref[...]
        xn_ref[...] = _rms_rows(x, g_ref[...]).astype(BF16)
        o_ref[...] = x

    a = jnp.dot(xn_ref[...], _mxu_weight(wu_ref, wub_ref), preferred_element_type=F32)
    a = jnp.square(jnp.maximum(a, 0.0)).astype(BF16)
    o_ref[...] += jnp.dot(a, _mxu_weight(wd_ref, wdb_ref), preferred_element_type=F32)

    if final_norm:
        @pl.when(f == pl.num_programs(1) - 1)
        def _():
            o_ref[...] = _rms_rows(o_ref[...], gf_ref[...])


def _ffn(x, gl, wul, wdl, g_final, *, tm, tf, emit=False):
    g, l = gl
    w_up, lu = wul
    w_down, ld = wdl
    M, K = x.shape
    F = w_up.shape[2]
    final_norm = g_final is not None
    in_specs = [
        pl.BlockSpec((tm, K), lambda i, f: (i, 0)),
        pl.BlockSpec((None, 1, K), lambda i, f: (l, 0, 0)),
        pl.BlockSpec((None, K, tf), lambda i, f: (lu, 0, f)),
        pl.BlockSpec((None, tf, K), lambda i, f: (ld, f, 0)),
    ]
    ub_specs, ub_shapes = _weight_copy_out(emit, M, tm, K, F, K, tf, lambda i, f: (0, f))
    db_specs, db_shapes = _weight_copy_out(emit, M, tm, F, K, tf, K, lambda i, f: (f, 0))
    args = [x, g, w_up, w_down]
    if final_norm:
        in_specs.append(pl.BlockSpec((1, K), lambda i, f: (0, 0)))
        args.append(g_final)
    outs = pl.pallas_call(
        functools.partial(_ffn_kernel, final_norm=final_norm, emit=emit),
        grid=(M // tm, F // tf),
        in_specs=in_specs,
        out_specs=[pl.BlockSpec((tm, K), lambda i, f: (i, 0))] + ub_specs + db_specs,
        out_shape=[jax.ShapeDtypeStruct((M, K), F32)] + ub_shapes + db_shapes,
        scratch_shapes=[pltpu.VMEM((tm, K), BF16)],
        compiler_params=_params(2),
        name="ffn",
    )(*args)
    return (outs[0], outs[1][None], outs[2][None]) if emit else outs[0]


def _head_rows(ref, h, n_heads):
    return ref[pl.ds(h, PAGE_SIZE, stride=n_heads), :].astype(BF16)


def _page_k(k_ref):
    return jnp.concatenate([_head_rows(k_ref, h, N_QK) for h in range(N_QK)], axis=1)


def _matmul_res_kernel(x_ref, w_ref, h_ref, o_ref, wb_ref=None):
    o_ref[...] = h_ref[...] + jnp.dot(x_ref[...], _mxu_weight(w_ref, wb_ref), preferred_element_type=F32)


def _matmul_res(x, wl, h, *, tm, tn, emit=False):
    w, l = wl
    M, K = x.shape
    N = w.shape[2]
    wb_specs, wb_shapes = _weight_copy_out(emit, M, tm, K, N, K, tn, lambda i, j: (0, j))
    outs = pl.pallas_call(
        _matmul_res_kernel,
        grid=(M // tm, N // tn),
        in_specs=[
            pl.BlockSpec((tm, K), lambda i, j: (i, 0)),
            pl.BlockSpec((None, K, tn), lambda i, j: (l, 0, j)),
            pl.BlockSpec((tm, tn), lambda i, j: (i, j)),
        ],
        out_specs=[pl.BlockSpec((tm, tn), lambda i, j: (i, j))] + wb_specs,
        out_shape=[jax.ShapeDtypeStruct((M, N), F32)] + wb_shapes,
        compiler_params=_params(2),
        name="matmul_res",
    )(x, w, h)
    return (outs[0], outs[1][None]) if emit else outs[0]


def _lane_chunks(s):
    return [s[:, c * LANES:(c + 1) * LANES] for c in range(s.shape[1] // LANES)]


def _chunk_max(chunks):
    m = chunks[0]
    for c in chunks[1:]:
        m = jnp.maximum(m, c)
    return m


def _softmax_step(s_list, m_ref, l_ref, m_blk=None):
    cols = [_lane_chunks(s) for s in s_list]
    if m_blk is None:
        m_blk = _chunk_max([c for cs in cols for c in cs])
    m_prev = m_ref[...]
    m_new = jnp.maximum(m_prev, jnp.max(m_blk, axis=-1, keepdims=True))
    alpha = jnp.exp2(m_prev - m_new)
    probs = [[jnp.exp2(c - m_new) for c in cs] for cs in cols]
    flat = [p for ps in probs for p in ps]
    l_blk = flat[0]
    for p in flat[1:]:
        l_blk = l_blk + p
    l_ref[...] = alpha * l_ref[...] + jnp.sum(l_blk, axis=-1, keepdims=True)
    m_ref[...] = m_new
    return [jnp.concatenate(ps, axis=1).astype(BF16) if len(ps) > 1 else ps[0].astype(BF16)
            for ps in probs], alpha


def _head_out(o1, o2, lam, subln, lam_init):
    d = o1 - lam * o2
    return _rms_rows(d, subln) * (1.0 - lam_init)


def _attn_prompt_kernel(qi_tab, kj_tab, q0_ref, k0_ref, qn_ref, kn_ref, v_ref, lam_ref, subln_ref, o_ref,
                        m_ref, l_ref, acc_ref, sa_ref, sb_ref, ma_ref, mb_ref, p_ref, a_ref, *, lam_init):
    step = pl.program_id(2)
    n_steps = pl.num_programs(2)
    qi = qi_tab[step]
    kj = kj_tab[step]
    nxt = jnp.where(step + 1 == n_steps, 0, step + 1)
    next_masked = qi_tab[nxt] == kj_tab[nxt]
    tq = qn_ref.shape[1]
    tk = kn_ref.shape[1]
    n_sub = m_ref.shape[0]

    def scores(q_ref, k_ref, bufs, masked):
        s_ref, mx_ref = bufs
        for sub in range(n_sub):
            sl = slice(sub * HEAD_DIM, (sub + 1) * HEAD_DIM)
            q = q_ref[0, :, sl]
            k = k_ref[0, :, sl].astype(BF16)
            s = lax.dot_general(q, k, (((1,), (1,)), ((), ())), preferred_element_type=F32)
            if masked:
                r = lax.broadcasted_iota(jnp.int32, (tq, tk), 0)
                c = lax.broadcasted_iota(jnp.int32, (tq, tk), 1)
                s = jnp.where(c <= r, s, NEG_INF)
            s_ref[sub] = s
            mx_ref[sub] = _chunk_max(_lane_chunks(s))

    def accumulate(bufs):
        s_ref, mx_ref = bufs
        for sub in range(n_sub):
            p, alpha = _softmax_step([s_ref[sub]], m_ref.at[sub], l_ref.at[sub], mx_ref[sub])
            p_ref[sub] = p[0]
            a_ref[sub] = alpha
        for sub in range(n_sub):
            vs = slice((sub // 2) * V_DIM, (sub // 2 + 1) * V_DIM)
            alpha = a_ref[sub]
            acc_ref[sub] = (jnp.concatenate([alpha, alpha], axis=1) * acc_ref[sub]
                            + jnp.dot(p_ref[sub], v_ref[0, :, vs].astype(BF16), preferred_element_type=F32))

    @pl.when((pl.program_id(0) == 0) & (pl.program_id(1) == 0) & (step == 0))
    def _():
        scores(q0_ref, k0_ref, (sa_ref, ma_ref), True)

    @pl.when(kj == 0)
    def _():
        m_ref[...] = jnp.full(m_ref.shape, NEG_INF, F32)
        l_ref[...] = jnp.zeros(l_ref.shape, F32)
        acc_ref[...] = jnp.zeros(acc_ref.shape, F32)

    for parity, (cur_ref, nxt_ref) in enumerate((((sa_ref, ma_ref), (sb_ref, mb_ref)),
                                                 ((sb_ref, mb_ref), (sa_ref, ma_ref)))):
        for masked in (False, True):
            @pl.when((step % 2 == parity) & (next_masked == masked))
            def _(cur_ref=cur_ref, nxt_ref=nxt_ref, masked=masked):
                scores(qn_ref, kn_ref, nxt_ref, masked)
                accumulate(cur_ref)

    @pl.when(kj == qi)
    def _():
        lam = _lambda(lam_ref, lam_init)
        for hv in range(n_sub // 2):
            inv1 = 1.0 / l_ref[2 * hv]
            inv2 = 1.0 / l_ref[2 * hv + 1]
            o1 = acc_ref[2 * hv] * jnp.concatenate([inv1, inv1], axis=1)
            o2 = acc_ref[2 * hv + 1] * jnp.concatenate([inv2, inv2], axis=1)
            o_ref[0, :, hv * V_DIM:(hv + 1) * V_DIM] = _head_out(
                o1, o2, lam, subln_ref[...], lam_init).astype(o_ref.dtype)


def _attn_prompt(q, k, v, lam_vecs, subln, l, lam_init, *, tq, heads_per_step):
    B, T, _ = q.shape
    nq = T // tq
    pairs = [(i, j) for i in range(nq) for j in range(i + 1)]
    qi_tab = jnp.asarray([p[0] for p in pairs], jnp.int32)
    kj_tab = jnp.asarray([p[1] for p in pairs], jnp.int32)
    hw = heads_per_step * V_DIM
    n_sub = 2 * heads_per_step
    n_hg = N_HEADS // heads_per_step
    n_steps = len(pairs)
    assert n_steps % 2 == 0

    def next_block(tab):
        def index_map(b, h, s, qt, kt):
            wrap_s = (s + 1 == n_steps).astype(jnp.int32)
            s_n = (s + 1) * (1 - wrap_s)
            wrap_h = ((h + wrap_s) == n_hg).astype(jnp.int32)
            h_n = (h + wrap_s) * (1 - wrap_h)
            b_n = jnp.minimum(b + wrap_h, B - 1)
            return (b_n, (qt if tab == "q" else kt)[s_n], h_n)
        return index_map

    grid_spec = pltpu.PrefetchScalarGridSpec(
        num_scalar_prefetch=2,
        grid=(B, n_hg, n_steps),
        in_specs=[
            pl.BlockSpec((1, tq, hw), lambda b, h, s, qt, kt: (0, 0, 0)),
            pl.BlockSpec((1, tq, hw), lambda b, h, s, qt, kt: (0, 0, 0)),
            pl.BlockSpec((1, tq, hw), next_block("q")),
            pl.BlockSpec((1, tq, hw), next_block("k")),
            pl.BlockSpec((1, tq, hw), lambda b, h, s, qt, kt: (b, kt[s], h)),
            pl.BlockSpec((None, 4, HEAD_DIM), lambda b, h, s, qt, kt: (l, 0, 0)),
            pl.BlockSpec((None, 1, V_DIM), lambda b, h, s, qt, kt: (l, 0, 0)),
        ],
        out_specs=pl.BlockSpec((1, tq, hw), lambda b, h, s, qt, kt: (b, qt[s], h)),
        scratch_shapes=[pltpu.VMEM((n_sub, tq, LANES), F32), pltpu.VMEM((n_sub, tq, LANES), F32),
                        pltpu.VMEM((n_sub, tq, V_DIM), F32),
                        pltpu.VMEM((n_sub, tq, tq), F32), pltpu.VMEM((n_sub, tq, tq), F32),
                        pltpu.VMEM((n_sub, tq, LANES), F32), pltpu.VMEM((n_sub, tq, LANES), F32),
                        pltpu.VMEM((n_sub, tq, tq), BF16), pltpu.VMEM((n_sub, tq, LANES), F32)],
    )
    return pl.pallas_call(
        functools.partial(_attn_prompt_kernel, lam_init=lam_init),
        grid_spec=grid_spec,
        out_shape=jax.ShapeDtypeStruct((B, T, N_HEADS * V_DIM), BF16),
        compiler_params=_params(3),
        name="attn_prompt",
    )(qi_tab, kj_tab, q, k, q, k, v, lam_vecs, subln)


def _attn_sample_step(load_tiles, q_ref, kn_ref, vn_ref, lam_ref, subln_ref, o_ref,
                      qbd_ref, m_ref, l_ref, acc_ref, lam_init):
    p_idx = pl.program_id(1)
    tq = q_ref.shape[1]
    rows = N_QK * tq
    d_all = N_QK * HEAD_DIM

    @pl.when(p_idx == 0)
    def _():
        m_ref[...] = jnp.full(m_ref.shape, NEG_INF, F32)
        l_ref[...] = jnp.zeros(l_ref.shape, F32)
        acc_ref[...] = jnp.zeros(acc_ref.shape, F32)
        qt = jnp.concatenate([q_ref[0]] * N_QK, axis=0)
        r = lax.broadcasted_iota(jnp.int32, (rows, d_all), 0)
        c = lax.broadcasted_iota(jnp.int32, (rows, d_all), 1)
        qbd_ref[...] = jnp.where(r // tq == c // HEAD_DIM, qt, 0.0).astype(BF16)

    def update(k_list, v_list, mask):
        qbd = qbd_ref[...]
        s_list = []
        for kp in k_list:
            s = lax.dot_general(qbd, kp, (((1,), (1,)), ((), ())), preferred_element_type=F32)
            if mask is not None:
                s = jnp.where(mask, s, NEG_INF)
            s_list.append(s)
        p_list, alpha = _softmax_step(s_list, m_ref, l_ref)
        for hv in range(N_HEADS):
            rs = slice(hv * 2 * tq, (hv + 1) * 2 * tq)
            pv = None
            for p, v_heads in zip(p_list, v_list):
                t = jnp.dot(p[rs, :], v_heads[hv], preferred_element_type=F32)
                pv = t if pv is None else pv + t
            a = alpha[rs, :]
            acc_ref[rs, :] = jnp.concatenate([a, a], axis=1) * acc_ref[rs, :] + pv

    update(*load_tiles(), None)

    @pl.when(p_idx == pl.num_programs(1) - 1)
    def _():
        pad = jnp.zeros((PAGE_SIZE - tq, d_all), F32)
        kn = jnp.concatenate([kn_ref[0], pad], axis=0).astype(BF16)
        vn = jnp.concatenate([vn_ref[0], pad], axis=0).astype(BF16)
        r = lax.broadcasted_iota(jnp.int32, (rows, PAGE_SIZE), 0)
        c = lax.broadcasted_iota(jnp.int32, (rows, PAGE_SIZE), 1)
        update([kn], [[vn[:, hv * V_DIM:(hv + 1) * V_DIM] for hv in range(N_HEADS)]], c <= r % tq)
        lam = _lambda(lam_ref, lam_init)
        inv = 1.0 / l_ref[...]
        o = acc_ref[...] * jnp.concatenate([inv, inv], axis=1)
        for hv in range(N_HEADS):
            o1 = o[hv * 2 * tq:hv * 2 * tq + tq, :]
            o2 = o[hv * 2 * tq + tq:(hv + 1) * 2 * tq, :]
            o_ref[0, :, hv * V_DIM:(hv + 1) * V_DIM] = _head_out(
                o1, o2, lam, subln_ref[...], lam_init).astype(o_ref.dtype)


def _attn_sample_kernel(pt_ref, q_ref, *refs, pages, lam_init):
    k_refs = refs[:pages]
    v_refs = refs[pages:3 * pages]

    def load_tiles():
        return ([_page_k(k) for k in k_refs],
                [[jnp.concatenate([_head_rows(r, hv, N_HEADS) for r in v_refs[2 * i:2 * i + 2]], axis=1)
                  for hv in range(N_HEADS)] for i in range(pages)])

    _attn_sample_step(load_tiles, q_ref, *refs[3 * pages:], lam_init)


def _attn_sample(q, cache_k, cache_v, page_table, k_new, v_new, lam_vecs, subln, l, lam_init, *,
                 pages_per_step):
    Bd, Tq, D = q.shape
    n_pages = page_table.shape[1]
    pps = pages_per_step

    def page_map(i, half):
        return lambda b, p, pt: (pt[b * n_pages + p * pps + i], 0, half)

    k_specs = [pl.BlockSpec((None, PAGE_SIZE * N_QK, HEAD_DIM), page_map(i, 0)) for i in range(pps)]
    v_specs = [pl.BlockSpec((None, PAGE_SIZE * N_HEADS, LANES), page_map(i, half))
               for i in range(pps) for half in range(V_DIM // LANES)]
    row_spec = pl.BlockSpec((1, Tq, D), lambda b, p, pt: (b, 0, 0))
    grid_spec = pltpu.PrefetchScalarGridSpec(
        num_scalar_prefetch=1,
        grid=(Bd, n_pages // pps),
        in_specs=[row_spec] + k_specs + v_specs + [
            row_spec, row_spec,
            pl.BlockSpec((None, 4, HEAD_DIM), lambda b, p, pt: (l, 0, 0)),
            pl.BlockSpec((None, 1, V_DIM), lambda b, p, pt: (l, 0, 0)),
        ],
        out_specs=row_spec,
        scratch_shapes=[pltpu.VMEM((N_QK * Tq, D), BF16),
                        pltpu.VMEM((N_QK * Tq, LANES), F32), pltpu.VMEM((N_QK * Tq, LANES), F32),
                        pltpu.VMEM((N_QK * Tq, V_DIM), F32)],
    )
    return pl.pallas_call(
        functools.partial(_attn_sample_kernel, pages=pps, lam_init=lam_init),
        grid_spec=grid_spec,
        out_shape=jax.ShapeDtypeStruct((Bd, Tq, D), BF16),
        compiler_params=_params(2),
        name="attn_sample",
    )(page_table.reshape(-1), q, *([cache_k] * pps), *([cache_v] * (2 * pps)), k_new, v_new, lam_vecs, subln)


def _rope_tables(pos):
    half = HEAD_DIM // 2
    inv = ROPE_THETA ** (-jnp.arange(half, dtype=F32) / half)
    ang = pos.astype(F32)[:, None] * inv[None, :]
    cos = jnp.cos(ang)
    sin = jnp.sin(ang)
    return jnp.concatenate([cos, cos], axis=-1), jnp.concatenate([-sin, sin], axis=-1)


class _Group:
    def __init__(self, x, pos, cfg):
        self.B, self.T, _ = x.shape
        self.h = x.reshape(self.B * self.T, D_MODEL)
        self.cfg = cfg
        cos, sin = _rope_tables(pos)
        if self.T < cfg["tm"]:
            cos = jnp.tile(cos, (cfg["tm"] // self.T, 1))
            sin = jnp.tile(sin, (cfg["tm"] // self.T, 1))
        self.rope = (cos, sin)
        self.v_rows = []
        self.k_sh = self.v_sh = self.k_mxu = self.v_mxu = None


def _layer(G, l, p, W, Wb=None, paged=None):
    cfg, B, T, h = G.cfg, G.B, G.T, G.h
    M = B * T
    tm, rows, chunk = cfg["tm"], cfg["rows"], cfg["chunk"]
    cos, sin = G.rope
    emit = Wb is not None

    def keep(name, outs):
        if not emit:
            return outs
        for n, wb in zip(name.split(","), outs[1:]):
            Wb[n].append((wb, 0))
        return outs[0]

    if l < N_A_LAYERS and cfg["fused_gate"]:
        assert not emit and chunk == CHUNK
        gated = _gmlp_gate(h, p["norm_a"][l], W["w_in_a"][l], p["w_s_a"][l], p["b_s_a"][l],
                           tm=cfg["tm_in"], tn=cfg["tn_in"])
        h = _matmul_res(gated, W["w_out_a"][l], h, tm=cfg["tm_gmlp"], tn=cfg["tn_gmlp"])
    elif l < N_A_LAYERS:
        z = keep("w_in_a", _norm_matmul(h, p["norm_a"][l], W["w_in_a"][l], tm=cfg["tm_in"], tn=cfg["tn_in"],
                                        out_dtype=cfg["z_dtype"], epilogue="gelu", emit=emit))
        ws = p["w_s_a"][l][:, :chunk, :chunk]
        bs_t = p["b_s_a"][l][:, :chunk].T
        if rows > chunk:
            ws = jnp.tile(ws, (1, rows // chunk, rows // chunk))
            bs_t = jnp.tile(bs_t, (rows // chunk, 1))
        h = keep("w_out_a", _gmlp_out(z, ws, bs_t, h, W["w_out_a"][l], tm=cfg["tm_gmlp"], tn=cfg["tn_gmlp"],
                                      rows=rows, chunk=chunk, emit=emit))
        G.v_rows.append(z[:, D_GATE:])
    else:
        if l == N_A_LAYERS:
            dup = paged is None
            k_sh = keep("w_k", _norm_matmul(h, p["norm_kv"][0], W["w_k"][0], tm=tm, tn=cfg["tn"], out_dtype=F32,
                                            epilogue="rope", rope=(cos, sin), emit=emit, dup=dup))
            v_sh = keep("w_v", _norm_matmul(h, p["norm_kv"][0], W["w_v"][0], tm=tm, tn=cfg["tn"], out_dtype=F32,
                                            emit=emit, dup=dup))
            if dup:
                (k_sh, G.k_mxu), (v_sh, G.v_mxu) = k_sh, v_sh
            G.k_sh, G.v_sh = k_sh, v_sh
        j = l - N_A_LAYERS
        lam_init = 0.8 - 0.6 * math.exp(-0.3 * l)
        q = keep("w_q", _norm_matmul(h, p["norm_b"][j], W["w_q"][j], tm=tm, tn=cfg["tn"],
                                     out_dtype=cfg["q_dtype"], epilogue="rope", rope=(cos, sin),
                                     scale=HEAD_DIM ** -0.5 * LOG2E, emit=emit))
        if paged is None:
            o = _attn_prompt(q.reshape(B, T, -1), G.k_mxu.reshape(B, T, -1), G.v_mxu.reshape(B, T, -1),
                             p["lam_vecs"], p["subln_b"], j, lam_init, tq=cfg["tq"],
                             heads_per_step=cfg["hps"])
        else:
            o = _attn_sample(q.reshape(B, T, -1), *paged, G.k_sh.reshape(B, T, -1), G.v_sh.reshape(B, T, -1),
                             p["lam_vecs"], p["subln_b"], j, lam_init, pages_per_step=cfg["pps"])
        h = keep("w_o_b", _matmul_res(o.reshape(M, -1), W["w_o_b"][j], h, tm=tm, tn=cfg["tn"], emit=emit))
    G.h = keep("w_up,w_down", _ffn(h, p["norm_ffn"][l], W["w_up"][l], W["w_down"][l],
                                   p["norm_f"] if l == DEPTH - 1 else None,
                                   tm=cfg["tm_ffn"], tf=cfg["tf"], emit=emit))


def kernel(x_prompt, x_sample, cache_k, cache_v, page_table, norm_a, w_in_a, w_s_a, b_s_a, w_out_a, norm_kv, w_k, w_v, norm_b, w_q, lambda_q1, lambda_k1, lambda_q2, lambda_k2, subln_b, w_o_b, norm_ffn, w_up, w_down, norm_f):
    def layers(a):
        return [(a, l) for l in range(a.shape[0])]

    def gains(g):
        return layers(g.reshape(g.shape[0], 1, g.shape[1]))

    p = dict(norm_a=gains(norm_a), w_s_a=w_s_a, b_s_a=b_s_a, norm_kv=gains(norm_kv[None]),
             norm_b=gains(norm_b),
             lam_vecs=jnp.stack([lambda_q1, lambda_k1, lambda_q2, lambda_k2], axis=1),
             subln_b=subln_b.reshape(subln_b.shape[0], 1, -1), norm_ffn=gains(norm_ffn),
             norm_f=norm_f.reshape(1, -1))
    W = dict(w_in_a=layers(w_in_a), w_out_a=layers(w_out_a), w_k=layers(w_k[None]), w_v=layers(w_v[None]),
             w_q=layers(w_q), w_o_b=layers(w_o_b), w_up=layers(w_up), w_down=layers(w_down))

    B, T, _ = x_prompt.shape
    Bd, Td, _ = x_sample.shape

    Ms = Bd * Td
    cfg_s = dict(tm=Ms, tn=1024, tm_in=Ms, tn_in=1024, tm_gmlp=Ms, tn_gmlp=512, tm_ffn=Ms, tf=512,
                 pps=8, rows=Ms, chunk=Td, z_dtype=F32, q_dtype=F32, fused_gate=False)
    cfg_p = dict(tm=512, tn=2048, tm_in=1024, tn_in=1024, tm_gmlp=1024, tn_gmlp=512, tm_ffn=512, tf=2048,
                 tq=512, hps=4, rows=CHUNK, chunk=CHUNK, q_dtype=BF16, fused_gate=True)
    S = _Group(x_sample, PAST_LEN + jnp.arange(Td, dtype=jnp.int32), cfg_s)
    P = _Group(x_prompt, jnp.arange(T, dtype=jnp.int32), cfg_p)

    n_pool = cache_k.shape[0]
    paged = (cache_k.reshape(n_pool, PAGE_SIZE * N_QK, HEAD_DIM),
             cache_v.reshape(n_pool, PAGE_SIZE * N_HEADS, V_DIM), page_table)
    Wb = {name: [] for name in W}
    for l in range(DEPTH):
        _layer(S, l, p, W, Wb=Wb, paged=paged)
    for l in range(DEPTH):
        _layer(P, l, p, Wb)

    return (P.h.reshape(B, T, D_MODEL), S.h.reshape(Bd, Td, D_MODEL),
            P.k_sh.reshape(B, T, N_QK, HEAD_DIM), P.v_sh.reshape(B, T, N_HEADS, V_DIM),
            S.k_sh.reshape(Bd, Td, N_QK, HEAD_DIM), S.v_sh.reshape(Bd, Td, N_HEADS, V_DIM),
            jnp.stack(S.v_rows).reshape(N_A_LAYERS, Bd, Td, D_GATE))
```

```python
import functools
import math

import jax
import jax.numpy as jnp
from jax import lax
from jax.experimental import pallas as pl
from jax.experimental.pallas import tpu as pltpu

D_MODEL = 2048
DEPTH = 4
PAST_LEN = 16384
PAGE_SIZE = 128
N_A_LAYERS = DEPTH // 2
CHUNK = 128
D_GATE = 2 * D_MODEL
N_GROUPS_A = 16
GROUP_DIM_A = D_GATE // N_GROUPS_A
HEAD_DIM = 128
N_HEADS = D_MODEL // (2 * HEAD_DIM)
N_QK = 2 * N_HEADS
V_DIM = 2 * HEAD_DIM
D_FF = 4 * D_MODEL
ROPE_THETA = 10000.0
EPS = 1e-5
NEG_INF = -1e30

LANES = 128
VMEM_LIMIT = 60 * 1024 * 1024
LOG2E = math.log2(math.e)

F32 = jnp.float32
BF16 = jnp.bfloat16


def _params(n_axes):
    return pltpu.CompilerParams(dimension_semantics=("arbitrary",) * n_axes,
                                vmem_limit_bytes=VMEM_LIMIT)


def _rms_rows(x, g):
    return x * lax.rsqrt(jnp.mean(x * x, axis=-1, keepdims=True) + EPS) * g


def _lambda(lam_ref, lam_init):
    a = jnp.sum(lam_ref[0:1, :] * lam_ref[1:2, :], axis=-1, keepdims=True)
    b = jnp.sum(lam_ref[2:3, :] * lam_ref[3:4, :], axis=-1, keepdims=True)
    return jnp.exp(a) - jnp.exp(b) + lam_init


def _mxu_weight(w_ref, wb_ref):
    w = w_ref[...].astype(BF16)
    if wb_ref is not None:
        wb_ref[...] = w
    return w


def _norm_matmul_kernel(*refs, epilogue, scale, emit, dup):
    x_ref, g_ref, w_ref = refs[:3]
    refs = refs[3:]
    if epilogue == "rope":
        cos_ref, sin_ref = refs[:2]
        refs = refs[2:]
    out_refs = refs[:2] if dup else refs[:1]
    wb_ref = refs[len(out_refs)] if emit else None
    xn_ref = refs[-1]

    def put(sl, val):
        for o_ref in out_refs:
            o_ref[:, sl] = val.astype(o_ref.dtype)

    @pl.when(pl.program_id(1) == 0)
    def _():
        xn_ref[...] = _rms_rows(x_ref[...], g_ref[...]).astype(BF16)

    y = jnp.dot(xn_ref[...], _mxu_weight(w_ref, wb_ref), preferred_element_type=F32)
    if epilogue == "gelu":
        put(slice(None), 0.5 * y * (1.0 + lax.erf(y * (2.0 ** -0.5))))
    elif epilogue == "rope":
        cos = cos_ref[...]
        sin = sin_ref[...]
        for h in range(y.shape[1] // HEAD_DIM):
            sl = slice(h * HEAD_DIM, (h + 1) * HEAD_DIM)
            yh = y[:, sl]
            oh = yh * cos + pltpu.roll(yh, HEAD_DIM // 2, 1) * sin
            if scale != 1.0:
                oh = oh * scale
            put(sl, oh)
    else:
        put(slice(None), y)


def _weight_copy_out(emit, M, tm, K, N, bk, bn, index_map):
    if not emit:
        return [], []
    assert M == tm
    return [pl.BlockSpec((bk, bn), index_map)], [jax.ShapeDtypeStruct((K, N), BF16)]


def _norm_matmul(x, gl, wl, *, tm, tn, out_dtype, epilogue=None, rope=None, scale=1.0, emit=False,
                 dup=False):
    assert not (emit and dup)
    g, lg = gl
    w, l = wl
    M, K = x.shape
    N = w.shape[2]
    wb_specs, wb_shapes = _weight_copy_out(emit, M, tm, K, N, K, tn, lambda i, j: (0, j))
    in_specs = [
        pl.BlockSpec((tm, K), lambda i, j: (i, 0)),
        pl.BlockSpec((None, 1, K), lambda i, j: (lg, 0, 0)),
        pl.BlockSpec((None, K, tn), lambda i, j: (l, 0, j)),
    ]
    args = [x, g, w]
    if epilogue == "rope":
        cos, sin = rope
        nb = cos.shape[0] // tm
        in_specs += [pl.BlockSpec((tm, HEAD_DIM), lambda i, j: (i % nb, 0))] * 2
        args += [cos, sin]
    outs = pl.pallas_call(
        functools.partial(_norm_matmul_kernel, epilogue=epilogue, scale=scale, emit=emit, dup=dup),
        grid=(M // tm, N // tn),
        in_specs=in_specs,
        out_specs=[pl.BlockSpec((tm, tn), lambda i, j: (i, j))] * (2 if dup else 1) + wb_specs,
        out_shape=[jax.ShapeDtypeStruct((M, N), out_dtype)]
        + ([jax.ShapeDtypeStruct((M, N), BF16)] if dup else []) + wb_shapes,
        scratch_shapes=[pltpu.VMEM((tm, K), BF16)],
        compiler_params=_params(2),
        name="norm_matmul_" + (epilogue or "plain"),
    )(*args)
    if emit:
        return outs[0], outs[1][None]
    return (outs[0], outs[1]) if dup else outs[0]


def _gmlp_out_kernel(u_ref, v_ref, ws_ref, bs_ref, h_ref, w_ref, o_ref, *refs, rows, chunk, emit):
    wb_ref = refs[0] if emit else None
    gated_ref, wt_ref = refs[-2:]
    tm = u_ref.shape[0]

    @pl.when(pl.program_id(1) == 0)
    def _():
        r = lax.broadcasted_iota(jnp.int32, (rows, rows), 0)
        c = lax.broadcasted_iota(jnp.int32, (rows, rows), 1)
        mask = (r // chunk == c // chunk) & (r >= c)
        for g in range(N_GROUPS_A):
            wt_ref[g] = jnp.where(mask, ws_ref[g], 0.0).astype(BF16)

        def mix(ci, carry):
            r0 = pl.multiple_of(ci * rows, rows)
            for g in range(N_GROUPS_A):
                sl = slice(g * GROUP_DIM_A, (g + 1) * GROUP_DIM_A)
                vg = v_ref[pl.ds(r0, rows), sl].astype(BF16)
                s = jnp.dot(wt_ref[g], vg, preferred_element_type=F32) + bs_ref[:, g:g + 1]
                ug = u_ref[pl.ds(r0, rows), sl].astype(F32)
                gated_ref[pl.ds(r0, rows), sl] = (ug * s).astype(BF16)
            return carry

        lax.fori_loop(0, tm // rows, mix, 0)

    o_ref[...] = h_ref[...] + jnp.dot(gated_ref[...], _mxu_weight(w_ref, wb_ref),
                                      preferred_element_type=F32)


def _gmlp_out(z, ws, bs_t, h, wl, *, tm, tn, rows, chunk, emit=False):
    w_out, l = wl
    M = z.shape[0]
    N = w_out.shape[2]
    wb_specs, wb_shapes = _weight_copy_out(emit, M, tm, D_GATE, N, D_GATE, tn, lambda i, j: (0, j))
    outs = pl.pallas_call(
        functools.partial(_gmlp_out_kernel, rows=rows, chunk=chunk, emit=emit),
        grid=(M // tm, N // tn),
        in_specs=[
            pl.BlockSpec((tm, D_GATE), lambda i, j: (i, 0)),
            pl.BlockSpec((tm, D_GATE), lambda i, j: (i, 1)),
            pl.BlockSpec((N_GROUPS_A, rows, rows), lambda i, j: (0, 0, 0)),
            pl.BlockSpec((rows, N_GROUPS_A), lambda i, j: (0, 0)),
            pl.BlockSpec((tm, tn), lambda i, j: (i, j)),
            pl.BlockSpec((None, D_GATE, tn), lambda i, j: (l, 0, j)),
        ],
        out_specs=[pl.BlockSpec((tm, tn), lambda i, j: (i, j))] + wb_specs,
        out_shape=[jax.ShapeDtypeStruct((M, N), F32)] + wb_shapes,
        scratch_shapes=[pltpu.VMEM((tm, D_GATE), BF16),
                        pltpu.VMEM((N_GROUPS_A, rows, rows), BF16)],
        compiler_params=_params(2),
        name="gmlp_out",
    )(z, z, ws, bs_t, h, w_out)
    return (outs[0], outs[1][None]) if emit else outs[0]


def _gmlp_gate_kernel(x_ref, g_ref, wu_ref, wv_ref, ws_ref, bs_ref, o_ref, xn_ref):
    @pl.when(pl.program_id(1) == 0)
    def _():
        xn_ref[...] = _rms_rows(x_ref[...], g_ref[...]).astype(BF16)

    def gelu(y):
        return 0.5 * y * (1.0 + lax.erf(y * (2.0 ** -0.5)))

    xn = xn_ref[...]
    u = gelu(jnp.dot(xn, wu_ref[...], preferred_element_type=F32))
    v = gelu(jnp.dot(xn, wv_ref[...], preferred_element_type=F32)).astype(BF16)
    r = lax.broadcasted_iota(jnp.int32, (CHUNK, CHUNK), 0)
    c = lax.broadcasted_iota(jnp.int32, (CHUNK, CHUNK), 1)
    for gg in range(ws_ref.shape[0]):
        wt = jnp.where(r >= c, ws_ref[gg], 0.0).astype(BF16)
        b = bs_ref[gg]
        b = jnp.concatenate([b] * (GROUP_DIM_A // LANES), axis=1)
        cs = slice(gg * GROUP_DIM_A, (gg + 1) * GROUP_DIM_A)
        for ci in range(u.shape[0] // CHUNK):
            rs = slice(ci * CHUNK, (ci + 1) * CHUNK)
            s = jnp.dot(wt, v[rs, cs], preferred_element_type=F32) + b
            o_ref[rs, cs] = (u[rs, cs] * s).astype(o_ref.dtype)


def _gmlp_gate(x, gl, wl, ws, bs, *, tm, tn):
    g, lg = gl
    w, l = wl
    M, K = x.shape
    gpt = tn // GROUP_DIM_A
    nt = D_GATE // tn
    bs_rep = jnp.broadcast_to(bs[:, :, None], bs.shape + (LANES,))
    return pl.pallas_call(
        _gmlp_gate_kernel,
        grid=(M // tm, nt),
        in_specs=[
            pl.BlockSpec((tm, K), lambda i, j: (i, 0)),
            pl.BlockSpec((None, 1, K), lambda i, j: (lg, 0, 0)),
            pl.BlockSpec((None, K, tn), lambda i, j: (l, 0, j)),
            pl.BlockSpec((None, K, tn), lambda i, j: (l, 0, nt + j)),
            pl.BlockSpec((gpt, CHUNK, CHUNK), lambda i, j: (j, 0, 0)),
            pl.BlockSpec((gpt, CHUNK, LANES), lambda i, j: (j, 0, 0)),
        ],
        out_specs=pl.BlockSpec((tm, tn), lambda i, j: (i, j)),
        out_shape=jax.ShapeDtypeStruct((M, D_GATE), BF16),
        scratch_shapes=[pltpu.VMEM((tm, K), BF16)],
        compiler_params=_params(2),
        name="gmlp_gate",
    )(x, g, w, w, ws, bs_rep)


def _ffn_kernel(*refs, final_norm, emit):
    x_ref, g_ref, wu_ref, wd_ref = refs[:4]
    refs = refs[4:]
    if final_norm:
        gf_ref = refs[0]
        refs = refs[1:]
    o_ref = refs[0]
    wub_ref, wdb_ref = refs[1:3] if emit else (None, None)
    xn_ref = refs[-1]
    f = pl.program_id(1)

    @pl.when(f == 0)
    def _():
        x = x_ref[...]
        xn_ref[...] = _rms_rows(x, g_ref[...]).astype(BF16)
        o_ref[...] = x

    a = jnp.dot(xn_ref[...], _mxu_weight(wu_ref, wub_ref), preferred_element_type=F32)
    a = jnp.square(jnp.maximum(a, 0.0)).astype(BF16)
    o_ref[...] += jnp.dot(a, _mxu_weight(wd_ref, wdb_ref), preferred_element_type=F32)

    if final_norm:
        @pl.when(f == pl.num_programs(1) - 1)
        def _():
            o_ref[...] = _rms_rows(o_ref[...], gf_ref[...])


def _ffn(x, gl, wul, wdl, g_final, *, tm, tf, emit=False):
    g, l = gl
    w_up, lu = wul
    w_down, ld = wdl
    M, K = x.shape
    F = w_up.shape[2]
    final_norm = g_final is not None
    in_specs = [
        pl.BlockSpec((tm, K), lambda i, f: (i, 0)),
        pl.BlockSpec((None, 1, K), lambda i, f: (l, 0, 0)),
        pl.BlockSpec((None, K, tf), lambda i, f: (lu, 0, f)),
        pl.BlockSpec((None, tf, K), lambda i, f: (ld, f, 0)),
    ]
    ub_specs, ub_shapes = _weight_copy_out(emit, M, tm, K, F, K, tf, lambda i, f: (0, f))
    db_specs, db_shapes = _weight_copy_out(emit, M, tm, F, K, tf, K, lambda i, f: (f, 0))
    args = [x, g, w_up, w_down]
    if final_norm:
        in_specs.append(pl.BlockSpec((1, K), lambda i, f: (0, 0)))
        args.append(g_final)
    outs = pl.pallas_call(
        functools.partial(_ffn_kernel, final_norm=final_norm, emit=emit),
        grid=(M // tm, F // tf),
        in_specs=in_specs,
        out_specs=[pl.BlockSpec((tm, K), lambda i, f: (i, 0))] + ub_specs + db_specs,
        out_shape=[jax.ShapeDtypeStruct((M, K), F32)] + ub_shapes + db_shapes,
        scratch_shapes=[pltpu.VMEM((tm, K), BF16)],
        compiler_params=_params(2),
        name="ffn",
    )(*args)
    return (outs[0], outs[1][None], outs[2][None]) if emit else outs[0]


def _head_rows(ref, h, n_heads):
    return ref[pl.ds(h, PAGE_SIZE, stride=n_heads), :].astype(BF16)


def _page_k(k_ref):
    return jnp.concatenate([_head_rows(k_ref, h, N_QK) for h in range(N_QK)], axis=1)


def _matmul_res_kernel(x_ref, w_ref, h_ref, o_ref, wb_ref=None):
    o_ref[...] = h_ref[...] + jnp.dot(x_ref[...], _mxu_weight(w_ref, wb_ref), preferred_element_type=F32)


def _matmul_res(x, wl, h, *, tm, tn, emit=False):
    w, l = wl
    M, K = x.shape
    N = w.shape[2]
    wb_specs, wb_shapes = _weight_copy_out(emit, M, tm, K, N, K, tn, lambda i, j: (0, j))
    outs = pl.pallas_call(
        _matmul_res_kernel,
        grid=(M // tm, N // tn),
        in_specs=[
            pl.BlockSpec((tm, K), lambda i, j: (i, 0)),
            pl.BlockSpec((None, K, tn), lambda i, j: (l, 0, j)),
            pl.BlockSpec((tm, tn), lambda i, j: (i, j)),
        ],
        out_specs=[pl.BlockSpec((tm, tn), lambda i, j: (i, j))] + wb_specs,
        out_shape=[jax.ShapeDtypeStruct((M, N), F32)] + wb_shapes,
        compiler_params=_params(2),
        name="matmul_res",
    )(x, w, h)
    return (outs[0], outs[1][None]) if emit else outs[0]


def _lane_chunks(s):
    return [s[:, c * LANES:(c + 1) * LANES] for c in range(s.shape[1] // LANES)]


def _chunk_max(chunks):
    m = chunks[0]
    for c in chunks[1:]:
        m = jnp.maximum(m, c)
    return m


def _softmax_step(s_list, m_ref, l_ref, m_blk=None):
    cols = [_lane_chunks(s) for s in s_list]
    if m_blk is None:
        m_blk = _chunk_max([c for cs in cols for c in cs])
    m_prev = m_ref[...]
    m_new = jnp.maximum(m_prev, jnp.max(m_blk, axis=-1, keepdims=True))
    alpha = jnp.exp2(m_prev - m_new)
    probs = [[jnp.exp2(c - m_new) for c in cs] for cs in cols]
    flat = [p for ps in probs for p in ps]
    l_blk = flat[0]
    for p in flat[1:]:
        l_blk = l_blk + p
    l_ref[...] = alpha * l_ref[...] + jnp.sum(l_blk, axis=-1, keepdims=True)
    m_ref[...] = m_new
    return [jnp.concatenate(ps, axis=1).astype(BF16) if len(ps) > 1 else ps[0].astype(BF16)
            for ps in probs], alpha


def _head_out(o1, o2, lam, subln, lam_init):
    d = o1 - lam * o2
    return _rms_rows(d, subln) * (1.0 - lam_init)


def _attn_prompt_kernel(qi_tab, kj_tab, q0_ref, k0_ref, qn_ref, kn_ref, v_ref, lam_ref, subln_ref, o_ref,
                        m_ref, l_ref, acc_ref, sa_ref, sb_ref, ma_ref, mb_ref, p_ref, a_ref, *, lam_init):
    step = pl.program_id(2)
    n_steps = pl.num_programs(2)
    qi = qi_tab[step]
    kj = kj_tab[step]
    nxt = jnp.where(step + 1 == n_steps, 0, step + 1)
    next_masked = qi_tab[nxt] == kj_tab[nxt]
    tq = qn_ref.shape[1]
    tk = kn_ref.shape[1]
    n_sub = m_ref.shape[0]

    def scores(q_ref, k_ref, bufs, masked):
        s_ref, mx_ref = bufs
        for sub in range(n_sub):
            sl = slice(sub * HEAD_DIM, (sub + 1) * HEAD_DIM)
            q = q_ref[0, :, sl]
            k = k_ref[0, :, sl].astype(BF16)
            s = lax.dot_general(q, k, (((1,), (1,)), ((), ())), preferred_element_type=F32)
            if masked:
                r = lax.broadcasted_iota(jnp.int32, (tq, tk), 0)
                c = lax.broadcasted_iota(jnp.int32, (tq, tk), 1)
                s = jnp.where(c <= r, s, NEG_INF)
            s_ref[sub] = s
            mx_ref[sub] = _chunk_max(_lane_chunks(s))

    def accumulate(bufs):
        s_ref, mx_ref = bufs
        for sub in range(n_sub):
            p, alpha = _softmax_step([s_ref[sub]], m_ref.at[sub], l_ref.at[sub], mx_ref[sub])
            p_ref[sub] = p[0]
            a_ref[sub] = alpha
        for sub in range(n_sub):
            vs = slice((sub // 2) * V_DIM, (sub // 2 + 1) * V_DIM)
            alpha = a_ref[sub]
            acc_ref[sub] = (jnp.concatenate([alpha, alpha], axis=1) * acc_ref[sub]
                            + jnp.dot(p_ref[sub], v_ref[0, :, vs].astype(BF16), preferred_element_type=F32))

    @pl.when((pl.program_id(0) == 0) & (pl.program_id(1) == 0) & (step == 0))
    def _():
        scores(q0_ref, k0_ref, (sa_ref, ma_ref), True)

    @pl.when(kj == 0)
    def _():
        m_ref[...] = jnp.full(m_ref.shape, NEG_INF, F32)
        l_ref[...] = jnp.zeros(l_ref.shape, F32)
        acc_ref[...] = jnp.zeros(acc_ref.shape, F32)

    for parity, (cur_ref, nxt_ref) in enumerate((((sa_ref, ma_ref), (sb_ref, mb_ref)),
                                                 ((sb_ref, mb_ref), (sa_ref, ma_ref)))):
        for masked in (False, True):
            @pl.when((step % 2 == parity) & (next_masked == masked))
            def _(cur_ref=cur_ref, nxt_ref=nxt_ref, masked=masked):
                scores(qn_ref, kn_ref, nxt_ref, masked)
                accumulate(cur_ref)

    @pl.when(kj == qi)
    def _():
        lam = _lambda(lam_ref, lam_init)
        for hv in range(n_sub // 2):
            inv1 = 1.0 / l_ref[2 * hv]
            inv2 = 1.0 / l_ref[2 * hv + 1]
            o1 = acc_ref[2 * hv] * jnp.concatenate([inv1, inv1], axis=1)
            o2 = acc_ref[2 * hv + 1] * jnp.concatenate([inv2, inv2], axis=1)
            o_ref[0, :, hv * V_DIM:(hv + 1) * V_DIM] = _head_out(
                o1, o2, lam, subln_ref[...], lam_init).astype(o_ref.dtype)


def _attn_prompt(q, k, v, lam_vecs, subln, l, lam_init, *, tq, heads_per_step):
    B, T, _ = q.shape
    nq = T // tq
    pairs = [(i, j) for i in range(nq) for j in range(i + 1)]
    qi_tab = jnp.asarray([p[0] for p in pairs], jnp.int32)
    kj_tab = jnp.asarray([p[1] for p in pairs], jnp.int32)
    hw = heads_per_step * V_DIM
    n_sub = 2 * heads_per_step
    n_hg = N_HEADS // heads_per_step
    n_steps = len(pairs)
    assert n_steps % 2 == 0

    def next_block(tab):
        def index_map(b, h, s, qt, kt):
            wrap_s = (s + 1 == n_steps).astype(jnp.int32)
            s_n = (s + 1) * (1 - wrap_s)
            wrap_h = ((h + wrap_s) == n_hg).astype(jnp.int32)
            h_n = (h + wrap_s) * (1 - wrap_h)
            b_n = jnp.minimum(b + wrap_h, B - 1)
            return (b_n, (qt if tab == "q" else kt)[s_n], h_n)
        return index_map

    grid_spec = pltpu.PrefetchScalarGridSpec(
        num_scalar_prefetch=2,
        grid=(B, n_hg, n_steps),
        in_specs=[
            pl.BlockSpec((1, tq, hw), lambda b, h, s, qt, kt: (0, 0, 0)),
            pl.BlockSpec((1, tq, hw), lambda b, h, s, qt, kt: (0, 0, 0)),
            pl.BlockSpec((1, tq, hw), next_block("q")),
            pl.BlockSpec((1, tq, hw), next_block("k")),
            pl.BlockSpec((1, tq, hw), lambda b, h, s, qt, kt: (b, kt[s], h)),
            pl.BlockSpec((None, 4, HEAD_DIM), lambda b, h, s, qt, kt: (l, 0, 0)),
            pl.BlockSpec((None, 1, V_DIM), lambda b, h, s, qt, kt: (l, 0, 0)),
        ],
        out_specs=pl.BlockSpec((1, tq, hw), lambda b, h, s, qt, kt: (b, qt[s], h)),
        scratch_shapes=[pltpu.VMEM((n_sub, tq, LANES), F32), pltpu.VMEM((n_sub, tq, LANES), F32),
                        pltpu.VMEM((n_sub, tq, V_DIM), F32),
                        pltpu.VMEM((n_sub, tq, tq), F32), pltpu.VMEM((n_sub, tq, tq), F32),
                        pltpu.VMEM((n_sub, tq, LANES), F32), pltpu.VMEM((n_sub, tq, LANES), F32),
                        pltpu.VMEM((n_sub, tq, tq), BF16), pltpu.VMEM((n_sub, tq, LANES), F32)],
    )
    return pl.pallas_call(
        functools.partial(_attn_prompt_kernel, lam_init=lam_init),
        grid_spec=grid_spec,
        out_shape=jax.ShapeDtypeStruct((B, T, N_HEADS * V_DIM), BF16),
        compiler_params=_params(3),
        name="attn_prompt",
    )(qi_tab, kj_tab, q, k, q, k, v, lam_vecs, subln)


def _attn_sample_step(load_tiles, q_ref, kn_ref, vn_ref, lam_ref, subln_ref, o_ref,
                      qbd_ref, m_ref, l_ref, acc_ref, lam_init):
    p_idx = pl.program_id(1)
    tq = q_ref.shape[1]
    rows = N_QK * tq
    d_all = N_QK * HEAD_DIM

    @pl.when(p_idx == 0)
    def _():
        m_ref[...] = jnp.full(m_ref.shape, NEG_INF, F32)
        l_ref[...] = jnp.zeros(l_ref.shape, F32)
        acc_ref[...] = jnp.zeros(acc_ref.shape, F32)
        qt = jnp.concatenate([q_ref[0]] * N_QK, axis=0)
        r = lax.broadcasted_iota(jnp.int32, (rows, d_all), 0)
        c = lax.broadcasted_iota(jnp.int32, (rows, d_all), 1)
        qbd_ref[...] = jnp.where(r // tq == c // HEAD_DIM, qt, 0.0).astype(BF16)

    def update(k_list, v_list, mask):
        qbd = qbd_ref[...]
        s_list = []
        for kp in k_list:
            s = lax.dot_general(qbd, kp, (((1,), (1,)), ((), ())), preferred_element_type=F32)
            if mask is not None:
                s = jnp.where(mask, s, NEG_INF)
            s_list.append(s)
        p_list, alpha = _softmax_step(s_list, m_ref, l_ref)
        for hv in range(N_HEADS):
            rs = slice(hv * 2 * tq, (hv + 1) * 2 * tq)
            pv = None
            for p, v_heads in zip(p_list, v_list):
                t = jnp.dot(p[rs, :], v_heads[hv], preferred_element_type=F32)
                pv = t if pv is None else pv + t
            a = alpha[rs, :]
            acc_ref[rs, :] = jnp.concatenate([a, a], axis=1) * acc_ref[rs, :] + pv

    update(*load_tiles(), None)

    @pl.when(p_idx == pl.num_programs(1) - 1)
    def _():
        pad = jnp.zeros((PAGE_SIZE - tq, d_all), F32)
        kn = jnp.concatenate([kn_ref[0], pad], axis=0).astype(BF16)
        vn = jnp.concatenate([vn_ref[0], pad], axis=0).astype(BF16)
        r = lax.broadcasted_iota(jnp.int32, (rows, PAGE_SIZE), 0)
        c = lax.broadcasted_iota(jnp.int32, (rows, PAGE_SIZE), 1)
        update([kn], [[vn[:, hv * V_DIM:(hv + 1) * V_DIM] for hv in range(N_HEADS)]], c <= r % tq)
        lam = _lambda(lam_ref, lam_init)
        inv = 1.0 / l_ref[...]
        o = acc_ref[...] * jnp.concatenate([inv, inv], axis=1)
        for hv in range(N_HEADS):
            o1 = o[hv * 2 * tq:hv * 2 * tq + tq, :]
            o2 = o[hv * 2 * tq + tq:(hv + 1) * 2 * tq, :]
            o_ref[0, :, hv * V_DIM:(hv + 1) * V_DIM] = _head_out(
                o1, o2, lam, subln_ref[...], lam_init).astype(o_ref.dtype)


def _attn_sample_kernel(pt_ref, q_ref, *refs, pages, lam_init):
    k_refs = refs[:pages]
    v_refs = refs[pages:3 * pages]

    def load_tiles():
        return ([_page_k(k) for k in k_refs],
                [[jnp.concatenate([_head_rows(r, hv, N_HEADS) for r in v_refs[2 * i:2 * i + 2]], axis=1)
                  for hv in range(N_HEADS)] for i in range(pages)])

    _attn_sample_step(load_tiles, q_ref, *refs[3 * pages:], lam_init)


def _attn_sample(q, cache_k, cache_v, page_table, k_new, v_new, lam_vecs, subln, l, lam_init, *,
                 pages_per_step):
    Bd, Tq, D = q.shape
    n_pages = page_table.shape[1]
    pps = pages_per_step

    def page_map(i, half):
        return lambda b, p, pt: (pt[b * n_pages + p * pps + i], 0, half)

    k_specs = [pl.BlockSpec((None, PAGE_SIZE * N_QK, HEAD_DIM), page_map(i, 0)) for i in range(pps)]
    v_specs = [pl.BlockSpec((None, PAGE_SIZE * N_HEADS, LANES), page_map(i, half))
               for i in range(pps) for half in range(V_DIM // LANES)]
    row_spec = pl.BlockSpec((1, Tq, D), lambda b, p, pt: (b, 0, 0))
    grid_spec = pltpu.PrefetchScalarGridSpec(
        num_scalar_prefetch=1,
        grid=(Bd, n_pages // pps),
        in_specs=[row_spec] + k_specs + v_specs + [
            row_spec, row_spec,
            pl.BlockSpec((None, 4, HEAD_DIM), lambda b, p, pt: (l, 0, 0)),
            pl.BlockSpec((None, 1, V_DIM), lambda b, p, pt: (l, 0, 0)),
        ],
        out_specs=row_spec,
        scratch_shapes=[pltpu.VMEM((N_QK * Tq, D), BF16),
                        pltpu.VMEM((N_QK * Tq, LANES), F32), pltpu.VMEM((N_QK * Tq, LANES), F32),
                        pltpu.VMEM((N_QK * Tq, V_DIM), F32)],
    )
    return pl.pallas_call(
        functools.partial(_attn_sample_kernel, pages=pps, lam_init=lam_init),
        grid_spec=grid_spec,
        out_shape=jax.ShapeDtypeStruct((Bd, Tq, D), BF16),
        compiler_params=_params(2),
        name="attn_sample",
    )(page_table.reshape(-1), q, *([cache_k] * pps), *([cache_v] * (2 * pps)), k_new, v_new, lam_vecs, subln)


def _rope_tables(pos):
    half = HEAD_DIM // 2
    inv = ROPE_THETA ** (-jnp.arange(half, dtype=F32) / half)
    ang = pos.astype(F32)[:, None] * inv[None, :]
    cos = jnp.cos(ang)
    sin = jnp.sin(ang)
    return jnp.concatenate([cos, cos], axis=-1), jnp.concatenate([-sin, sin], axis=-1)


class _Group:
    def __init__(self, x, pos, cfg):
        self.B, self.T, _ = x.shape
        self.h = x.reshape(self.B * self.T, D_MODEL)
        self.cfg = cfg
        cos, sin = _rope_tables(pos)
        if self.T < cfg["tm"]:
            cos = jnp.tile(cos, (cfg["tm"] // self.T, 1))
            sin = jnp.tile(sin, (cfg["tm"] // self.T, 1))
        self.rope = (cos, sin)
        self.v_rows = []
        self.k_sh = self.v_sh = self.k_mxu = self.v_mxu = None


def _layer(G, l, p, W, Wb=None, paged=None):
    cfg, B, T, h = G.cfg, G.B, G.T, G.h
    M = B * T
    tm, rows, chunk = cfg["tm"], cfg["rows"], cfg["chunk"]
    cos, sin = G.rope
    emit = Wb is not None

    def keep(name, outs):
        if not emit:
            return outs
        for n, wb in zip(name.split(","), outs[1:]):
            Wb[n].append((wb, 0))
        return outs[0]

    if l < N_A_LAYERS and cfg["fused_gate"]:
        assert not emit and chunk == CHUNK
        gated = _gmlp_gate(h, p["norm_a"][l], W["w_in_a"][l], p["w_s_a"][l], p["b_s_a"][l],
                           tm=cfg["tm_in"], tn=cfg["tn_in"])
        h = _matmul_res(gated, W["w_out_a"][l], h, tm=cfg["tm_gmlp"], tn=cfg["tn_gmlp"])
    elif l < N_A_LAYERS:
        z = keep("w_in_a", _norm_matmul(h, p["norm_a"][l], W["w_in_a"][l], tm=cfg["tm_in"], tn=cfg["tn_in"],
                                        out_dtype=cfg["z_dtype"], epilogue="gelu", emit=emit))
        ws = p["w_s_a"][l][:, :chunk, :chunk]
        bs_t = p["b_s_a"][l][:, :chunk].T
        if rows > chunk:
            ws = jnp.tile(ws, (1, rows // chunk, rows // chunk))
            bs_t = jnp.tile(bs_t, (rows // chunk, 1))
        h = keep("w_out_a", _gmlp_out(z, ws, bs_t, h, W["w_out_a"][l], tm=cfg["tm_gmlp"], tn=cfg["tn_gmlp"],
                                      rows=rows, chunk=chunk, emit=emit))
        G.v_rows.append(z[:, D_GATE:])
    else:
        if l == N_A_LAYERS:
            dup = paged is None
            k_sh = keep("w_k", _norm_matmul(h, p["norm_kv"][0], W["w_k"][0], tm=tm, tn=cfg["tn"], out_dtype=F32,
                                            epilogue="rope", rope=(cos, sin), emit=emit, dup=dup))
            v_sh = keep("w_v", _norm_matmul(h, p["norm_kv"][0], W["w_v"][0], tm=tm, tn=cfg["tn"], out_dtype=F32,
                                            emit=emit, dup=dup))
            if dup:
                (k_sh, G.k_mxu), (v_sh, G.v_mxu) = k_sh, v_sh
            G.k_sh, G.v_sh = k_sh, v_sh
        j = l - N_A_LAYERS
        lam_init = 0.8 - 0.6 * math.exp(-0.3 * l)
        q = keep("w_q", _norm_matmul(h, p["norm_b"][j], W["w_q"][j], tm=tm, tn=cfg["tn"],
                                     out_dtype=cfg["q_dtype"], epilogue="rope", rope=(cos, sin),
                                     scale=HEAD_DIM ** -0.5 * LOG2E, emit=emit))
        if paged is None:
            o = _attn_prompt(q.reshape(B, T, -1), G.k_mxu.reshape(B, T, -1), G.v_mxu.reshape(B, T, -1),
                             p["lam_vecs"], p["subln_b"], j, lam_init, tq=cfg["tq"],
                             heads_per_step=cfg["hps"])
        else:
            o = _attn_sample(q.reshape(B, T, -1), *paged, G.k_sh.reshape(B, T, -1), G.v_sh.reshape(B, T, -1),
                             p["lam_vecs"], p["subln_b"], j, lam_init, pages_per_step=cfg["pps"])
        h = keep("w_o_b", _matmul_res(o.reshape(M, -1), W["w_o_b"][j], h, tm=tm, tn=cfg["tn"], emit=emit))
    G.h = keep("w_up,w_down", _ffn(h, p["norm_ffn"][l], W["w_up"][l], W["w_down"][l],
                                   p["norm_f"] if l == DEPTH - 1 else None,
                                   tm=cfg["tm_ffn"], tf=cfg["tf"], emit=emit))


def kernel(x_prompt, x_sample, cache_k, cache_v, page_table, norm_a, w_in_a, w_s_a, b_s_a, w_out_a, norm_kv, w_k, w_v, norm_b, w_q, lambda_q1, lambda_k1, lambda_q2, lambda_k2, subln_b, w_o_b, norm_ffn, w_up, w_down, norm_f):
    def layers(a):
        return [(a, l) for l in range(a.shape[0])]

    def gains(g):
        return layers(g.reshape(g.shape[0], 1, g.shape[1]))

    p = dict(norm_a=gains(norm_a), w_s_a=w_s_a, b_s_a=b_s_a, norm_kv=gains(norm_kv[None]),
             norm_b=gains(norm_b),
             lam_vecs=jnp.stack([lambda_q1, lambda_k1, lambda_q2, lambda_k2], axis=1),
             subln_b=subln_b.reshape(subln_b.shape[0], 1, -1), norm_ffn=gains(norm_ffn),
             norm_f=norm_f.reshape(1, -1))
    W = dict(w_in_a=layers(w_in_a), w_out_a=layers(w_out_a), w_k=layers(w_k[None]), w_v=layers(w_v[None]),
             w_q=layers(w_q), w_o_b=layers(w_o_b), w_up=layers(w_up), w_down=layers(w_down))

    B, T, _ = x_prompt.shape
    Bd, Td, _ = x_sample.shape

    Ms = Bd * Td
    cfg_s = dict(tm=Ms, tn=2048, tm_in=Ms, tn_in=2048, tm_gmlp=Ms, tn_gmlp=1024, tm_ffn=Ms, tf=1024,
                 pps=8, rows=Ms, chunk=Td, z_dtype=F32, q_dtype=F32, fused_gate=False)
    cfg_p = dict(tm=512, tn=2048, tm_in=1024, tn_in=1024, tm_gmlp=1024, tn_gmlp=1024, tm_ffn=512, tf=2048,
                 tq=512, hps=4, rows=CHUNK, chunk=CHUNK, q_dtype=BF16, fused_gate=True)
    S = _Group(x_sample, PAST_LEN + jnp.arange(Td, dtype=jnp.int32), cfg_s)
    P = _Group(x_prompt, jnp.arange(T, dtype=jnp.int32), cfg_p)

    n_pool = cache_k.shape[0]
    paged = (cache_k.reshape(n_pool, PAGE_SIZE * N_QK, HEAD_DIM),
             cache_v.reshape(n_pool, PAGE_SIZE * N_HEADS, V_DIM), page_table)
    Wb = {name: [] for name in W}
    for l in range(DEPTH):
        _layer(S, l, p, W, Wb=Wb, paged=paged)
    for l in range(DEPTH):
        _layer(P, l, p, Wb)

    return (P.h.reshape(B, T, D_MODEL), S.h.reshape(Bd, Td, D_MODEL),
            P.k_sh.reshape(B, T, N_QK, HEAD_DIM), P.v_sh.reshape(B, T, N_HEADS, V_DIM),
            S.k_sh.reshape(Bd, Td, N_QK, HEAD_DIM), S.v_sh.reshape(Bd, Td, N_HEADS, V_DIM),
            jnp.stack(S.v_rows).reshape(N_A_LAYERS, Bd, Td, D_GATE))
```

```python
import functools
import math

import jax
import jax.numpy as jnp
from jax import lax
from jax.experimental import pallas as pl
from jax.experimental.pallas import tpu as pltpu

D_MODEL = 2048
DEPTH = 4
PAST_LEN = 16384
PAGE_SIZE = 128
N_A_LAYERS = DEPTH // 2
CHUNK = 128
D_GATE = 2 * D_MODEL
N_GROUPS_A = 16
GROUP_DIM_A = D_GATE // N_GROUPS_A
HEAD_DIM = 128
N_HEADS = D_MODEL // (2 * HEAD_DIM)
N_QK = 2 * N_HEADS
V_DIM = 2 * HEAD_DIM
D_FF = 4 * D_MODEL
ROPE_THETA = 10000.0
EPS = 1e-5
NEG_INF = -1e30

LANES = 128
VMEM_LIMIT = 60 * 1024 * 1024
LOG2E = math.log2(math.e)

F32 = jnp.float32
BF16 = jnp.bfloat16


def _params(n_axes):
    return pltpu.CompilerParams(dimension_semantics=("arbitrary",) * n_axes,
                                vmem_limit_bytes=VMEM_LIMIT)


def _rms_rows(x, g):
    return x * lax.rsqrt(jnp.mean(x * x, axis=-1, keepdims=True) + EPS) * g


def _lambda(lam_ref, lam_init):
    a = jnp.sum(lam_ref[0:1, :] * lam_ref[1:2, :], axis=-1, keepdims=True)
    b = jnp.sum(lam_ref[2:3, :] * lam_ref[3:4, :], axis=-1, keepdims=True)
    return jnp.exp(a) - jnp.exp(b) + lam_init


def _mxu_weight(w_ref, wb_ref):
    w = w_ref[...].astype(BF16)
    if wb_ref is not None:
        wb_ref[...] = w
    return w


def _norm_matmul_kernel(*refs, epilogue, scale, emit, dup):
    x_ref, g_ref, w_ref = refs[:3]
    refs = refs[3:]
    if epilogue == "rope":
        cos_ref, sin_ref = refs[:2]
        refs = refs[2:]
    out_refs = refs[:2] if dup else refs[:1]
    wb_ref = refs[len(out_refs)] if emit else None
    xn_ref = refs[-1]

    def put(sl, val):
        for o_ref in out_refs:
            o_ref[:, sl] = val.astype(o_ref.dtype)

    @pl.when(pl.program_id(1) == 0)
    def _():
        xn_ref[...] = _rms_rows(x_ref[...], g_ref[...]).astype(BF16)

    y = jnp.dot(xn_ref[...], _mxu_weight(w_ref, wb_ref), preferred_element_type=F32)
    if epilogue == "gelu":
        put(slice(None), 0.5 * y * (1.0 + lax.erf(y * (2.0 ** -0.5))))
    elif epilogue == "rope":
        cos = cos_ref[...]
        sin = sin_ref[...]
        for h in range(y.shape[1] // HEAD_DIM):
            sl = slice(h * HEAD_DIM, (h + 1) * HEAD_DIM)
            yh = y[:, sl]
            oh = yh * cos + pltpu.roll(yh, HEAD_DIM // 2, 1) * sin
            if scale != 1.0:
                oh = oh * scale
            put(sl, oh)
    else:
        put(slice(None), y)


def _weight_copy_out(emit, M, tm, K, N, bk, bn, index_map):
    if not emit:
        return [], []
    assert M == tm
    return [pl.BlockSpec((bk, bn), index_map)], [jax.ShapeDtypeStruct((K, N), BF16)]


def _norm_matmul(x, gl, wl, *, tm, tn, out_dtype, epilogue=None, rope=None, scale=1.0, emit=False,
                 dup=False):
    assert not (emit and dup)
    g, lg = gl
    w, l = wl
    M, K = x.shape
    N = w.shape[2]
    wb_specs, wb_shapes = _weight_copy_out(emit, M, tm, K, N, K, tn, lambda i, j: (0, j))
    in_specs = [
        pl.BlockSpec((tm, K), lambda i, j: (i, 0)),
        pl.BlockSpec((None, 1, K), lambda i, j: (lg, 0, 0)),
        pl.BlockSpec((None, K, tn), lambda i, j: (l, 0, j)),
    ]
    args = [x, g, w]
    if epilogue == "rope":
        cos, sin = rope
        nb = cos.shape[0] // tm
        in_specs += [pl.BlockSpec((tm, HEAD_DIM), lambda i, j: (i % nb, 0))] * 2
        args += [cos, sin]
    outs = pl.pallas_call(
        functools.partial(_norm_matmul_kernel, epilogue=epilogue, scale=scale, emit=emit, dup=dup),
        grid=(M // tm, N // tn),
        in_specs=in_specs,
        out_specs=[pl.BlockSpec((tm, tn), lambda i, j: (i, j))] * (2 if dup else 1) + wb_specs,
        out_shape=[jax.ShapeDtypeStruct((M, N), out_dtype)]
        + ([jax.ShapeDtypeStruct((M, N), BF16)] if dup else []) + wb_shapes,
        scratch_shapes=[pltpu.VMEM((tm, K), BF16)],
        compiler_params=_params(2),
        name="norm_matmul_" + (epilogue or "plain"),
    )(*args)
    if emit:
        return outs[0], outs[1][None]
    return (outs[0], outs[1]) if dup else outs[0]


def _gmlp_out_kernel(u_ref, v_ref, ws_ref, bs_ref, h_ref, w_ref, o_ref, *refs, rows, chunk, emit):
    wb_ref = refs[0] if emit else None
    gated_ref, wt_ref = refs[-2:]
    tm = u_ref.shape[0]

    @pl.when(pl.program_id(1) == 0)
    def _():
        r = lax.broadcasted_iota(jnp.int32, (rows, rows), 0)
        c = lax.broadcasted_iota(jnp.int32, (rows, rows), 1)
        mask = (r // chunk == c // chunk) & (r >= c)
        for g in range(N_GROUPS_A):
            wt_ref[g] = jnp.where(mask, ws_ref[g], 0.0).astype(BF16)

        def mix(ci, carry):
            r0 = pl.multiple_of(ci * rows, rows)
            for g in range(N_GROUPS_A):
                sl = slice(g * GROUP_DIM_A, (g + 1) * GROUP_DIM_A)
                vg = v_ref[pl.ds(r0, rows), sl].astype(BF16)
                s = jnp.dot(wt_ref[g], vg, preferred_element_type=F32) + bs_ref[:, g:g + 1]
                ug = u_ref[pl.ds(r0, rows), sl].astype(F32)
                gated_ref[pl.ds(r0, rows), sl] = (ug * s).astype(BF16)
            return carry

        lax.fori_loop(0, tm // rows, mix, 0)

    o_ref[...] = h_ref[...] + jnp.dot(gated_ref[...], _mxu_weight(w_ref, wb_ref),
                                      preferred_element_type=F32)


def _gmlp_out(z, ws, bs_t, h, wl, *, tm, tn, rows, chunk, emit=False):
    w_out, l = wl
    M = z.shape[0]
    N = w_out.shape[2]
    wb_specs, wb_shapes = _weight_copy_out(emit, M, tm, D_GATE, N, D_GATE, tn, lambda i, j: (0, j))
    outs = pl.pallas_call(
        functools.partial(_gmlp_out_kernel, rows=rows, chunk=chunk, emit=emit),
        grid=(M // tm, N // tn),
        in_specs=[
            pl.BlockSpec((tm, D_GATE), lambda i, j: (i, 0)),
            pl.BlockSpec((tm, D_GATE), lambda i, j: (i, 1)),
            pl.BlockSpec((N_GROUPS_A, rows, rows), lambda i, j: (0, 0, 0)),
            pl.BlockSpec((rows, N_GROUPS_A), lambda i, j: (0, 0)),
            pl.BlockSpec((tm, tn), lambda i, j: (i, j)),
            pl.BlockSpec((None, D_GATE, tn), lambda i, j: (l, 0, j)),
        ],
        out_specs=[pl.BlockSpec((tm, tn), lambda i, j: (i, j))] + wb_specs,
        out_shape=[jax.ShapeDtypeStruct((M, N), F32)] + wb_shapes,
        scratch_shapes=[pltpu.VMEM((tm, D_GATE), BF16),
                        pltpu.VMEM((N_GROUPS_A, rows, rows), BF16)],
        compiler_params=_params(2),
        name="gmlp_out",
    )(z, z, ws, bs_t, h, w_out)
    return (outs[0], outs[1][None]) if emit else outs[0]


def _gmlp_gate_kernel(x_ref, g_ref, wu_ref, wv_ref, ws_ref, bs_ref, o_ref, xn_ref):
    @pl.when(pl.program_id(1) == 0)
    def _():
        xn_ref[...] = _rms_rows(x_ref[...], g_ref[...]).astype(BF16)

    def gelu(y):
        return 0.5 * y * (1.0 + lax.erf(y * (2.0 ** -0.5)))

    xn = xn_ref[...]
    u = gelu(jnp.dot(xn, wu_ref[...], preferred_element_type=F32))
    v = gelu(jnp.dot(xn, wv_ref[...], preferred_element_type=F32)).astype(BF16)
    r = lax.broadcasted_iota(jnp.int32, (CHUNK, CHUNK), 0)
    c = lax.broadcasted_iota(jnp.int32, (CHUNK, CHUNK), 1)
    for gg in range(ws_ref.shape[0]):
        wt = jnp.where(r >= c, ws_ref[gg], 0.0).astype(BF16)
        b = bs_ref[gg]
        b = jnp.concatenate([b] * (GROUP_DIM_A // LANES), axis=1)
        cs = slice(gg * GROUP_DIM_A, (gg + 1) * GROUP_DIM_A)
        for ci in range(u.shape[0] // CHUNK):
            rs = slice(ci * CHUNK, (ci + 1) * CHUNK)
            s = jnp.dot(wt, v[rs, cs], preferred_element_type=F32) + b
            o_ref[rs, cs] = (u[rs, cs] * s).astype(o_ref.dtype)


def _gmlp_gate(x, gl, wl, ws, bs, *, tm, tn):
    g, lg = gl
    w, l = wl
    M, K = x.shape
    gpt = tn // GROUP_DIM_A
    nt = D_GATE // tn
    bs_rep = jnp.broadcast_to(bs[:, :, None], bs.shape + (LANES,))
    return pl.pallas_call(
        _gmlp_gate_kernel,
        grid=(M // tm, nt),
        in_specs=[
            pl.BlockSpec((tm, K), lambda i, j: (i, 0)),
            pl.BlockSpec((None, 1, K), lambda i, j: (lg, 0, 0)),
            pl.BlockSpec((None, K, tn), lambda i, j: (l, 0, j)),
            pl.BlockSpec((None, K, tn), lambda i, j: (l, 0, nt + j)),
            pl.BlockSpec((gpt, CHUNK, CHUNK), lambda i, j: (j, 0, 0)),
            pl.BlockSpec((gpt, CHUNK, LANES), lambda i, j: (j, 0, 0)),
        ],
        out_specs=pl.BlockSpec((tm, tn), lambda i, j: (i, j)),
        out_shape=jax.ShapeDtypeStruct((M, D_GATE), BF16),
        scratch_shapes=[pltpu.VMEM((tm, K), BF16)],
        compiler_params=_params(2),
        name="gmlp_gate",
    )(x, g, w, w, ws, bs_rep)


def _ffn_kernel(*refs, final_norm, emit):
    x_ref, g_ref, wu_ref, wd_ref = refs[:4]
    refs = refs[4:]
    if final_norm:
        gf_ref = refs[0]
        refs = refs[1:]
    o_ref = refs[0]
    wub_ref, wdb_ref = refs[1:3] if emit else (None, None)
    xn_ref = refs[-1]
    f = pl.program_id(1)

    @pl.when(f == 0)
    def _():
        x = x_ref[...]
        xn_ref[...] = _rms_rows(x, g_ref[...]).astype(BF16)
        o_ref[...] = x

    a = jnp.dot(xn_ref[...], _mxu_weight(wu_ref, wub_ref), preferred_element_type=F32)
    a = jnp.square(jnp.maximum(a, 0.0)).astype(BF16)
    o_ref[...] += jnp.dot(a, _mxu_weight(wd_ref, wdb_ref), preferred_element_type=F32)

    if final_norm:
        @pl.when(f == pl.num_programs(1) - 1)
        def _():
            o_ref[...] = _rms_rows(o_ref[...], gf_ref[...])


def _ffn(x, gl, wul, wdl, g_final, *, tm, tf, emit=False):
    g, l = gl
    w_up, lu = wul
    w_down, ld = wdl
    M, K = x.shape
    F = w_up.shape[2]
    final_norm = g_final is not None
    in_specs = [
        pl.BlockSpec((tm, K), lambda i, f: (i, 0)),
        pl.BlockSpec((None, 1, K), lambda i, f: (l, 0, 0)),
        pl.BlockSpec((None, K, tf), lambda i, f: (lu, 0, f)),
        pl.BlockSpec((None, tf, K), lambda i, f: (ld, f, 0)),
    ]
    ub_specs, ub_shapes = _weight_copy_out(emit, M, tm, K, F, K, tf, lambda i, f: (0, f))
    db_specs, db_shapes = _weight_copy_out(emit, M, tm, F, K, tf, K, lambda i, f: (f, 0))
    args = [x, g, w_up, w_down]
    if final_norm:
        in_specs.append(pl.BlockSpec((1, K), lambda i, f: (0, 0)))
        args.append(g_final)
    outs = pl.pallas_call(
        functools.partial(_ffn_kernel, final_norm=final_norm, emit=emit),
        grid=(M // tm, F // tf),
        in_specs=in_specs,
        out_specs=[pl.BlockSpec((tm, K), lambda i, f: (i, 0))] + ub_specs + db_specs,
        out_shape=[jax.ShapeDtypeStruct((M, K), F32)] + ub_shapes + db_shapes,
        scratch_shapes=[pltpu.VMEM((tm, K), BF16)],
        compiler_params=_params(2),
        name="ffn",
    )(*args)
    return (outs[0], outs[1][None], outs[2][None]) if emit else outs[0]


def _head_rows(ref, h, n_heads):
    return ref[pl.ds(h, PAGE_SIZE, stride=n_heads), :].astype(BF16)


def _page_k(k_ref):
    return jnp.concatenate([_head_rows(k_ref, h, N_QK) for h in range(N_QK)], axis=1)


def _matmul_res_kernel(x_ref, w_ref, h_ref, o_ref, wb_ref=None):
    o_ref[...] = h_ref[...] + jnp.dot(x_ref[...], _mxu_weight(w_ref, wb_ref), preferred_element_type=F32)


def _matmul_res(x, wl, h, *, tm, tn, emit=False):
    w, l = wl
    M, K = x.shape
    N = w.shape[2]
    wb_specs, wb_shapes = _weight_copy_out(emit, M, tm, K, N, K, tn, lambda i, j: (0, j))
    outs = pl.pallas_call(
        _matmul_res_kernel,
        grid=(M // tm, N // tn),
        in_specs=[
            pl.BlockSpec((tm, K), lambda i, j: (i, 0)),
            pl.BlockSpec((None, K, tn), lambda i, j: (l, 0, j)),
            pl.BlockSpec((tm, tn), lambda i, j: (i, j)),
        ],
        out_specs=[pl.BlockSpec((tm, tn), lambda i, j: (i, j))] + wb_specs,
        out_shape=[jax.ShapeDtypeStruct((M, N), F32)] + wb_shapes,
        compiler_params=_params(2),
        name="matmul_res",
    )(x, w, h)
    return (outs[0], outs[1][None]) if emit else outs[0]


def _lane_chunks(s):
    return [s[:, c * LANES:(c + 1) * LANES] for c in range(s.shape[1] // LANES)]


def _chunk_max(chunks):
    m = chunks[0]
    for c in chunks[1:]:
        m = jnp.maximum(m, c)
    return m


def _softmax_step(s_list, m_ref, l_ref, m_blk=None):
    cols = [_lane_chunks(s) for s in s_list]
    if m_blk is None:
        m_blk = _chunk_max([c for cs in cols for c in cs])
    m_prev = m_ref[...]
    m_new = jnp.maximum(m_prev, jnp.max(m_blk, axis=-1, keepdims=True))
    alpha = jnp.exp2(m_prev - m_new)
    probs = [[jnp.exp2(c - m_new) for c in cs] for cs in cols]
    flat = [p for ps in probs for p in ps]
    l_blk = flat[0]
    for p in flat[1:]:
        l_blk = l_blk + p
    l_ref[...] = alpha * l_ref[...] + jnp.sum(l_blk, axis=-1, keepdims=True)
    m_ref[...] = m_new
    return [jnp.concatenate(ps, axis=1).astype(BF16) if len(ps) > 1 else ps[0].astype(BF16)
            for ps in probs], alpha


def _head_out(o1, o2, lam, subln, lam_init):
    d = o1 - lam * o2
    return _rms_rows(d, subln) * (1.0 - lam_init)


def _attn_prompt_kernel(qi_tab, kj_tab, q0_ref, k0_ref, qn_ref, kn_ref, v_ref, lam_ref, subln_ref, o_ref,
                        m_ref, l_ref, acc_ref, sa_ref, sb_ref, ma_ref, mb_ref, p_ref, a_ref, *, lam_init):
    step = pl.program_id(2)
    n_steps = pl.num_programs(2)
    qi = qi_tab[step]
    kj = kj_tab[step]
    nxt = jnp.where(step + 1 == n_steps, 0, step + 1)
    next_masked = qi_tab[nxt] == kj_tab[nxt]
    tq = qn_ref.shape[1]
    tk = kn_ref.shape[1]
    n_sub = m_ref.shape[0]

    def scores(q_ref, k_ref, bufs, masked):
        s_ref, mx_ref = bufs
        for sub in range(n_sub):
            sl = slice(sub * HEAD_DIM, (sub + 1) * HEAD_DIM)
            q = q_ref[0, :, sl]
            k = k_ref[0, :, sl].astype(BF16)
            s = lax.dot_general(q, k, (((1,), (1,)), ((), ())), preferred_element_type=F32)
            if masked:
                r = lax.broadcasted_iota(jnp.int32, (tq, tk), 0)
                c = lax.broadcasted_iota(jnp.int32, (tq, tk), 1)
                s = jnp.where(c <= r, s, NEG_INF)
            s_ref[sub] = s
            mx_ref[sub] = _chunk_max(_lane_chunks(s))

    def accumulate(bufs):
        s_ref, mx_ref = bufs
        for sub in range(n_sub):
            p, alpha = _softmax_step([s_ref[sub]], m_ref.at[sub], l_ref.at[sub], mx_ref[sub])
            p_ref[sub] = p[0]
            a_ref[sub] = alpha
        for sub in range(n_sub):
            vs = slice((sub // 2) * V_DIM, (sub // 2 + 1) * V_DIM)
            alpha = a_ref[sub]
            acc_ref[sub] = (jnp.concatenate([alpha, alpha], axis=1) * acc_ref[sub]
                            + jnp.dot(p_ref[sub], v_ref[0, :, vs].astype(BF16), preferred_element_type=F32))

    @pl.when((pl.program_id(0) == 0) & (pl.program_id(1) == 0) & (step == 0))
    def _():
        scores(q0_ref, k0_ref, (sa_ref, ma_ref), True)

    @pl.when(kj == 0)
    def _():
        m_ref[...] = jnp.full(m_ref.shape, NEG_INF, F32)
        l_ref[...] = jnp.zeros(l_ref.shape, F32)
        acc_ref[...] = jnp.zeros(acc_ref.shape, F32)

    for parity, (cur_ref, nxt_ref) in enumerate((((sa_ref, ma_ref), (sb_ref, mb_ref)),
                                                 ((sb_ref, mb_ref), (sa_ref, ma_ref)))):
        for masked in (False, True):
            @pl.when((step % 2 == parity) & (next_masked == masked))
            def _(cur_ref=cur_ref, nxt_ref=nxt_ref, masked=masked):
                scores(qn_ref, kn_ref, nxt_ref, masked)
                accumulate(cur_ref)

    @pl.when(kj == qi)
    def _():
        lam = _lambda(lam_ref, lam_init)
        for hv in range(n_sub // 2):
            inv1 = 1.0 / l_ref[2 * hv]
            inv2 = 1.0 / l_ref[2 * hv + 1]
            o1 = acc_ref[2 * hv] * jnp.concatenate([inv1, inv1], axis=1)
            o2 = acc_ref[2 * hv + 1] * jnp.concatenate([inv2, inv2], axis=1)
            o_ref[0, :, hv * V_DIM:(hv + 1) * V_DIM] = _head_out(
                o1, o2, lam, subln_ref[...], lam_init).astype(o_ref.dtype)


def _attn_prompt(q, k, v, lam_vecs, subln, l, lam_init, *, tq, heads_per_step):
    B, T, _ = q.shape
    nq = T // tq
    pairs = [(i, j) for i in range(nq) for j in range(i + 1)]
    qi_tab = jnp.asarray([p[0] for p in pairs], jnp.int32)
    kj_tab = jnp.asarray([p[1] for p in pairs], jnp.int32)
    hw = heads_per_step * V_DIM
    n_sub = 2 * heads_per_step
    n_hg = N_HEADS // heads_per_step
    n_steps = len(pairs)
    assert n_steps % 2 == 0

    def next_block(tab):
        def index_map(b, h, s, qt, kt):
            wrap_s = (s + 1 == n_steps).astype(jnp.int32)
            s_n = (s + 1) * (1 - wrap_s)
            wrap_h = ((h + wrap_s) == n_hg).astype(jnp.int32)
            h_n = (h + wrap_s) * (1 - wrap_h)
            b_n = jnp.minimum(b + wrap_h, B - 1)
            return (b_n, (qt if tab == "q" else kt)[s_n], h_n)
        return index_map

    grid_spec = pltpu.PrefetchScalarGridSpec(
        num_scalar_prefetch=2,
        grid=(B, n_hg, n_steps),
        in_specs=[
            pl.BlockSpec((1, tq, hw), lambda b, h, s, qt, kt: (0, 0, 0)),
            pl.BlockSpec((1, tq, hw), lambda b, h, s, qt, kt: (0, 0, 0)),
            pl.BlockSpec((1, tq, hw), next_block("q")),
            pl.BlockSpec((1, tq, hw), next_block("k")),
            pl.BlockSpec((1, tq, hw), lambda b, h, s, qt, kt: (b, kt[s], h)),
            pl.BlockSpec((None, 4, HEAD_DIM), lambda b, h, s, qt, kt: (l, 0, 0)),
            pl.BlockSpec((None, 1, V_DIM), lambda b, h, s, qt, kt: (l, 0, 0)),
        ],
        out_specs=pl.BlockSpec((1, tq, hw), lambda b, h, s, qt, kt: (b, qt[s], h)),
        scratch_shapes=[pltpu.VMEM((n_sub, tq, LANES), F32), pltpu.VMEM((n_sub, tq, LANES), F32),
                        pltpu.VMEM((n_sub, tq, V_DIM), F32),
                        pltpu.VMEM((n_sub, tq, tq), F32), pltpu.VMEM((n_sub, tq, tq), F32),
                        pltpu.VMEM((n_sub, tq, LANES), F32), pltpu.VMEM((n_sub, tq, LANES), F32),
                        pltpu.VMEM((n_sub, tq, tq), BF16), pltpu.VMEM((n_sub, tq, LANES), F32)],
    )
    return pl.pallas_call(
        functools.partial(_attn_prompt_kernel, lam_init=lam_init),
        grid_spec=grid_spec,
        out_shape=jax.ShapeDtypeStruct((B, T, N_HEADS * V_DIM), BF16),
        compiler_params=_params(3),
        name="attn_prompt",
    )(qi_tab, kj_tab, q, k, q, k, v, lam_vecs, subln)


def _attn_sample_step(load_tiles, q_ref, kn_ref, vn_ref, lam_ref, subln_ref, o_ref,
                      qbd_ref, m_ref, l_ref, acc_ref, lam_init):
    p_idx = pl.program_id(1)
    tq = q_ref.shape[1]
    rows = N_QK * tq
    d_all = N_QK * HEAD_DIM

    @pl.when(p_idx == 0)
    def _():
        m_ref[...] = jnp.full(m_ref.shape, NEG_INF, F32)
        l_ref[...] = jnp.zeros(l_ref.shape, F32)
        acc_ref[...] = jnp.zeros(acc_ref.shape, F32)
        qt = jnp.concatenate([q_ref[0]] * N_QK, axis=0)
        r = lax.broadcasted_iota(jnp.int32, (rows, d_all), 0)
        c = lax.broadcasted_iota(jnp.int32, (rows, d_all), 1)
        qbd_ref[...] = jnp.where(r // tq == c // HEAD_DIM, qt, 0.0).astype(BF16)

    def update(k_list, v_list, mask):
        qbd = qbd_ref[...]
        s_list = []
        for kp in k_list:
            s = lax.dot_general(qbd, kp, (((1,), (1,)), ((), ())), preferred_element_type=F32)
            if mask is not None:
                s = jnp.where(mask, s, NEG_INF)
            s_list.append(s)
        p_list, alpha = _softmax_step(s_list, m_ref, l_ref)
        for hv in range(N_HEADS):
            rs = slice(hv * 2 * tq, (hv + 1) * 2 * tq)
            pv = None
            for p, v_heads in zip(p_list, v_list):
                t = jnp.dot(p[rs, :], v_heads[hv], preferred_element_type=F32)
                pv = t if pv is None else pv + t
            a = alpha[rs, :]
            acc_ref[rs, :] = jnp.concatenate([a, a], axis=1) * acc_ref[rs, :] + pv

    update(*load_tiles(), None)

    @pl.when(p_idx == pl.num_programs(1) - 1)
    def _():
        pad = jnp.zeros((PAGE_SIZE - tq, d_all), F32)
        kn = jnp.concatenate([kn_ref[0], pad], axis=0).astype(BF16)
        vn = jnp.concatenate([vn_ref[0], pad], axis=0).astype(BF16)
        r = lax.broadcasted_iota(jnp.int32, (rows, PAGE_SIZE), 0)
        c = lax.broadcasted_iota(jnp.int32, (rows, PAGE_SIZE), 1)
        update([kn], [[vn[:, hv * V_DIM:(hv + 1) * V_DIM] for hv in range(N_HEADS)]], c <= r % tq)
        lam = _lambda(lam_ref, lam_init)
        inv = 1.0 / l_ref[...]
        o = acc_ref[...] * jnp.concatenate([inv, inv], axis=1)
        for hv in range(N_HEADS):
            o1 = o[hv * 2 * tq:hv * 2 * tq + tq, :]
            o2 = o[hv * 2 * tq + tq:(hv + 1) * 2 * tq, :]
            o_ref[0, :, hv * V_DIM:(hv + 1) * V_DIM] = _head_out(
                o1, o2, lam, subln_ref[...], lam_init).astype(o_ref.dtype)


def _attn_sample_kernel(pt_ref, q_ref, *refs, pages, lam_init):
    k_refs = refs[:pages]
    v_refs = refs[pages:3 * pages]

    def load_tiles():
        return ([_page_k(k) for k in k_refs],
                [[jnp.concatenate([_head_rows(r, hv, N_HEADS) for r in v_refs[2 * i:2 * i + 2]], axis=1)
                  for hv in range(N_HEADS)] for i in range(pages)])

    _attn_sample_step(load_tiles, q_ref, *refs[3 * pages:], lam_init)


def _attn_sample(q, cache_k, cache_v, page_table, k_new, v_new, lam_vecs, subln, l, lam_init, *,
                 pages_per_step):
    Bd, Tq, D = q.shape
    n_pages = page_table.shape[1]
    pps = pages_per_step

    def page_map(i, half):
        return lambda b, p, pt: (pt[b * n_pages + p * pps + i], 0, half)

    k_specs = [pl.BlockSpec((None, PAGE_SIZE * N_QK, HEAD_DIM), page_map(i, 0)) for i in range(pps)]
    v_specs = [pl.BlockSpec((None, PAGE_SIZE * N_HEADS, LANES), page_map(i, half))
               for i in range(pps) for half in range(V_DIM // LANES)]
    row_spec = pl.BlockSpec((1, Tq, D), lambda b, p, pt: (b, 0, 0))
    grid_spec = pltpu.PrefetchScalarGridSpec(
        num_scalar_prefetch=1,
        grid=(Bd, n_pages // pps),
        in_specs=[row_spec] + k_specs + v_specs + [
            row_spec, row_spec,
            pl.BlockSpec((None, 4, HEAD_DIM), lambda b, p, pt: (l, 0, 0)),
            pl.BlockSpec((None, 1, V_DIM), lambda b, p, pt: (l, 0, 0)),
        ],
        out_specs=row_spec,
        scratch_shapes=[pltpu.VMEM((N_QK * Tq, D), BF16),
                        pltpu.VMEM((N_QK * Tq, LANES), F32), pltpu.VMEM((N_QK * Tq, LANES), F32),
                        pltpu.VMEM((N_QK * Tq, V_DIM), F32)],
    )
    return pl.pallas_call(
        functools.partial(_attn_sample_kernel, pages=pps, lam_init=lam_init),
        grid_spec=grid_spec,
        out_shape=jax.ShapeDtypeStruct((Bd, Tq, D), BF16),
        compiler_params=_params(2),
        name="attn_sample",
    )(page_table.reshape(-1), q, *([cache_k] * pps), *([cache_v] * (2 * pps)), k_new, v_new, lam_vecs, subln)


def _rope_tables(pos):
    half = HEAD_DIM // 2
    inv = ROPE_THETA ** (-jnp.arange(half, dtype=F32) / half)
    ang = pos.astype(F32)[:, None] * inv[None, :]
    cos = jnp.cos(ang)
    sin = jnp.sin(ang)
    return jnp.concatenate([cos, cos], axis=-1), jnp.concatenate([-sin, sin], axis=-1)


class _Group:
    def __init__(self, x, pos, cfg):
        self.B, self.T, _ = x.shape
        self.h = x.reshape(self.B * self.T, D_MODEL)
        self.cfg = cfg
        cos, sin = _rope_tables(pos)
        if self.T < cfg["tm"]:
            cos = jnp.tile(cos, (cfg["tm"] // self.T, 1))
            sin = jnp.tile(sin, (cfg["tm"] // self.T, 1))
        self.rope = (cos, sin)
        self.v_rows = []
        self.k_sh = self.v_sh = self.k_mxu = self.v_mxu = None


def _layer(G, l, p, W, Wb=None, paged=None):
    cfg, B, T, h = G.cfg, G.B, G.T, G.h
    M = B * T
    tm, rows, chunk = cfg["tm"], cfg["rows"], cfg["chunk"]
    cos, sin = G.rope
    emit = Wb is not None

    def keep(name, outs):
        if not emit:
            return outs
        for n, wb in zip(name.split(","), outs[1:]):
            Wb[n].append((wb, 0))
        return outs[0]

    if l < N_A_LAYERS and cfg["fused_gate"]:
        assert not emit and chunk == CHUNK
        gated = _gmlp_gate(h, p["norm_a"][l], W["w_in_a"][l], p["w_s_a"][l], p["b_s_a"][l],
                           tm=cfg["tm_in"], tn=cfg["tn_in"])
        h = _matmul_res(gated, W["w_out_a"][l], h, tm=cfg["tm_gmlp"], tn=cfg["tn_gmlp"])
    elif l < N_A_LAYERS:
        z = keep("w_in_a", _norm_matmul(h, p["norm_a"][l], W["w_in_a"][l], tm=cfg["tm_in"], tn=cfg["tn_in"],
                                        out_dtype=cfg["z_dtype"], epilogue="gelu", emit=emit))
        ws = p["w_s_a"][l][:, :chunk, :chunk]
        bs_t = p["b_s_a"][l][:, :chunk].T
        if rows > chunk:
            ws = jnp.tile(ws, (1, rows // chunk, rows // chunk))
            bs_t = jnp.tile(bs_t, (rows // chunk, 1))
        h = keep("w_out_a", _gmlp_out(z, ws, bs_t, h, W["w_out_a"][l], tm=cfg["tm_gmlp"], tn=cfg["tn_gmlp"],
                                      rows=rows, chunk=chunk, emit=emit))
        G.v_rows.append(z[:, D_GATE:])
    else:
        if l == N_A_LAYERS:
            dup = paged is None
            k_sh = keep("w_k", _norm_matmul(h, p["norm_kv"][0], W["w_k"][0], tm=tm, tn=cfg["tn"], out_dtype=F32,
                                            epilogue="rope", rope=(cos, sin), emit=emit, dup=dup))
            v_sh = keep("w_v", _norm_matmul(h, p["norm_kv"][0], W["w_v"][0], tm=tm, tn=cfg["tn"], out_dtype=F32,
                                            emit=emit, dup=dup))
            if dup:
                (k_sh, G.k_mxu), (v_sh, G.v_mxu) = k_sh, v_sh
            G.k_sh, G.v_sh = k_sh, v_sh
        j = l - N_A_LAYERS
        lam_init = 0.8 - 0.6 * math.exp(-0.3 * l)
        q = keep("w_q", _norm_matmul(h, p["norm_b"][j], W["w_q"][j], tm=tm, tn=cfg["tn"],
                                     out_dtype=cfg["q_dtype"], epilogue="rope", rope=(cos, sin),
                                     scale=HEAD_DIM ** -0.5 * LOG2E, emit=emit))
        if paged is None:
            o = _attn_prompt(q.reshape(B, T, -1), G.k_mxu.reshape(B, T, -1), G.v_mxu.reshape(B, T, -1),
                             p["lam_vecs"], p["subln_b"], j, lam_init, tq=cfg["tq"],
                             heads_per_step=cfg["hps"])
        else:
            o = _attn_sample(q.reshape(B, T, -1), *paged, G.k_sh.reshape(B, T, -1), G.v_sh.reshape(B, T, -1),
                             p["lam_vecs"], p["subln_b"], j, lam_init, pages_per_step=cfg["pps"])
        h = keep("w_o_b", _matmul_res(o.reshape(M, -1), W["w_o_b"][j], h, tm=tm, tn=cfg["tn"], emit=emit))
    G.h = keep("w_up,w_down", _ffn(h, p["norm_ffn"][l], W["w_up"][l], W["w_down"][l],
                                   p["norm_f"] if l == DEPTH - 1 else None,
                                   tm=cfg["tm_ffn"], tf=cfg["tf"], emit=emit))


def kernel(x_prompt, x_sample, cache_k, cache_v, page_table, norm_a, w_in_a, w_s_a, b_s_a, w_out_a, norm_kv, w_k, w_v, norm_b, w_q, lambda_q1, lambda_k1, lambda_q2, lambda_k2, subln_b, w_o_b, norm_ffn, w_up, w_down, norm_f):
    def layers(a):
        return [(a, l) for l in range(a.shape[0])]

    def gains(g):
        return layers(g.reshape(g.shape[0], 1, g.shape[1]))

    p = dict(norm_a=gains(norm_a), w_s_a=w_s_a, b_s_a=b_s_a, norm_kv=gains(norm_kv[None]),
             norm_b=gains(norm_b),
             lam_vecs=jnp.stack([lambda_q1, lambda_k1, lambda_q2, lambda_k2], axis=1),
             subln_b=subln_b.reshape(subln_b.shape[0], 1, -1), norm_ffn=gains(norm_ffn),
             norm_f=norm_f.reshape(1, -1))
    W = dict(w_in_a=layers(w_in_a), w_out_a=layers(w_out_a), w_k=layers(w_k[None]), w_v=layers(w_v[None]),
             w_q=layers(w_q), w_o_b=layers(w_o_b), w_up=layers(w_up), w_down=layers(w_down))

    B, T, _ = x_prompt.shape
    Bd, Td, _ = x_sample.shape

    Ms = Bd * Td
    cfg_s = dict(tm=Ms, tn=1024, tm_in=Ms, tn_in=1024, tm_gmlp=Ms, tn_gmlp=512, tm_ffn=Ms, tf=512,
                 pps=8, rows=Ms, chunk=Td, z_dtype=F32, q_dtype=F32, fused_gate=False)
    cfg_p = dict(tm=512, tn=2048, tm_in=1024, tn_in=1024, tm_gmlp=1024, tn_gmlp=1024, tm_ffn=512, tf=2048,
                 tq=512, hps=4, rows=CHUNK, chunk=CHUNK, q_dtype=BF16, fused_gate=True)
    S = _Group(x_sample, PAST_LEN + jnp.arange(Td, dtype=jnp.int32), cfg_s)
    P = _Group(x_prompt, jnp.arange(T, dtype=jnp.int32), cfg_p)

    n_pool = cache_k.shape[0]
    paged = (cache_k.reshape(n_pool, PAGE_SIZE * N_QK, HEAD_DIM),
             cache_v.reshape(n_pool, PAGE_SIZE * N_HEADS, V_DIM), page_table)
    Wb = {name: [] for name in W}
    for l in range(DEPTH):
        _layer(S, l, p, W, Wb=Wb, paged=paged)
    for l in range(DEPTH):
        _layer(P, l, p, Wb)

    return (P.h.reshape(B, T, D_MODEL), S.h.reshape(Bd, Td, D_MODEL),
            P.k_sh.reshape(B, T, N_QK, HEAD_DIM), P.v_sh.reshape(B, T, N_HEADS, V_DIM),
            S.k_sh.reshape(Bd, Td, N_QK, HEAD_DIM), S.v_sh.reshape(Bd, Td, N_HEADS, V_DIM),
            jnp.stack(S.v_rows).reshape(N_A_LAYERS, Bd, Td, D_GATE))
```

```python
import functools
import math

import jax
import jax.numpy as jnp
from jax import lax
from jax.experimental import pallas as pl
from jax.experimental.pallas import tpu as pltpu

D_MODEL = 2048
DEPTH = 4
PAST_LEN = 16384
PAGE_SIZE = 128
N_A_LAYERS = DEPTH // 2
CHUNK = 128
D_GATE = 2 * D_MODEL
N_GROUPS_A = 16
GROUP_DIM_A = D_GATE // N_GROUPS_A
HEAD_DIM = 128
N_HEADS = D_MODEL // (2 * HEAD_DIM)
N_QK = 2 * N_HEADS
V_DIM = 2 * HEAD_DIM
D_FF = 4 * D_MODEL
ROPE_THETA = 10000.0
EPS = 1e-5
NEG_INF = -1e30

LANES = 128
VMEM_LIMIT = 60 * 1024 * 1024
LOG2E = math.log2(math.e)

F32 = jnp.float32
BF16 = jnp.bfloat16


def _params(n_axes):
    return pltpu.CompilerParams(dimension_semantics=("arbitrary",) * n_axes,
                                vmem_limit_bytes=VMEM_LIMIT)


def _rms_rows(x, g):
    return x * lax.rsqrt(jnp.mean(x * x, axis=-1, keepdims=True) + EPS) * g


def _lambda(lam_ref, lam_init):
    a = jnp.sum(lam_ref[0:1, :] * lam_ref[1:2, :], axis=-1, keepdims=True)
    b = jnp.sum(lam_ref[2:3, :] * lam_ref[3:4, :], axis=-1, keepdims=True)
    return jnp.exp(a) - jnp.exp(b) + lam_init


def _mxu_weight(w_ref, wb_ref):
    w = w_ref[...].astype(BF16)
    if wb_ref is not None:
        wb_ref[...] = w
    return w


def _norm_matmul_kernel(*refs, epilogue, scale, emit, dup):
    x_ref, g_ref, w_ref = refs[:3]
    refs = refs[3:]
    if epilogue == "rope":
        cos_ref, sin_ref = refs[:2]
        refs = refs[2:]
    out_refs = refs[:2] if dup else refs[:1]
    wb_ref = refs[len(out_refs)] if emit else None
    xn_ref = refs[-1]

    def put(sl, val):
        for o_ref in out_refs:
            o_ref[:, sl] = val.astype(o_ref.dtype)

    @pl.when(pl.program_id(1) == 0)
    def _():
        xn_ref[...] = _rms_rows(x_ref[...], g_ref[...]).astype(BF16)

    y = jnp.dot(xn_ref[...], _mxu_weight(w_ref, wb_ref), preferred_element_type=F32)
    if epilogue == "gelu":
        put(slice(None), 0.5 * y * (1.0 + lax.erf(y * (2.0 ** -0.5))))
    elif epilogue == "rope":
        cos = cos_ref[...]
        sin = sin_ref[...]
        for h in range(y.shape[1] // HEAD_DIM):
            sl = slice(h * HEAD_DIM, (h + 1) * HEAD_DIM)
            yh = y[:, sl]
            oh = yh * cos + pltpu.roll(yh, HEAD_DIM // 2, 1) * sin
            if scale != 1.0:
                oh = oh * scale
            put(sl, oh)
    else:
        put(slice(None), y)


def _weight_copy_out(emit, M, tm, K, N, bk, bn, index_map):
    if not emit:
        return [], []
    assert M == tm
    return [pl.BlockSpec((bk, bn), index_map)], [jax.ShapeDtypeStruct((K, N), BF16)]


def _norm_matmul(x, gl, wl, *, tm, tn, out_dtype, epilogue=None, rope=None, scale=1.0, emit=False,
                 dup=False):
    assert not (emit and dup)
    g, lg = gl
    w, l = wl
    M, K = x.shape
    N = w.shape[2]
    wb_specs, wb_shapes = _weight_copy_out(emit, M, tm, K, N, K, tn, lambda i, j: (0, j))
    in_specs = [
        pl.BlockSpec((tm, K), lambda i, j: (i, 0)),
        pl.BlockSpec((None, 1, K), lambda i, j: (lg, 0, 0)),
        pl.BlockSpec((None, K, tn), lambda i, j: (l, 0, j)),
    ]
    args = [x, g, w]
    if epilogue == "rope":
        cos, sin = rope
        nb = cos.shape[0] // tm
        in_specs += [pl.BlockSpec((tm, HEAD_DIM), lambda i, j: (i % nb, 0))] * 2
        args += [cos, sin]
    outs = pl.pallas_call(
        functools.partial(_norm_matmul_kernel, epilogue=epilogue, scale=scale, emit=emit, dup=dup),
        grid=(M // tm, N // tn),
        in_specs=in_specs,
        out_specs=[pl.BlockSpec((tm, tn), lambda i, j: (i, j))] * (2 if dup else 1) + wb_specs,
        out_shape=[jax.ShapeDtypeStruct((M, N), out_dtype)]
        + ([jax.ShapeDtypeStruct((M, N), BF16)] if dup else []) + wb_shapes,
        scratch_shapes=[pltpu.VMEM((tm, K), BF16)],
        compiler_params=_params(2),
        name="norm_matmul_" + (epilogue or "plain"),
    )(*args)
    if emit:
        return outs[0], outs[1][None]
    return (outs[0], outs[1]) if dup else outs[0]


def _gmlp_out_kernel(u_ref, v_ref, ws_ref, bs_ref, h_ref, w_ref, o_ref, *refs, rows, chunk, emit):
    wb_ref = refs[0] if emit else None
    gated_ref, wt_ref = refs[-2:]
    tm = u_ref.shape[0]

    @pl.when(pl.program_id(1) == 0)
    def _():
        r = lax.broadcasted_iota(jnp.int32, (rows, rows), 0)
        c = lax.broadcasted_iota(jnp.int32, (rows, rows), 1)
        mask = (r // chunk == c // chunk) & (r >= c)
        for g in range(N_GROUPS_A):
            wt_ref[g] = jnp.where(mask, ws_ref[g], 0.0).astype(BF16)

        def mix(ci, carry):
            r0 = pl.multiple_of(ci * rows, rows)
            for g in range(N_GROUPS_A):
                sl = slice(g * GROUP_DIM_A, (g + 1) * GROUP_DIM_A)
                vg = v_ref[pl.ds(r0, rows), sl].astype(BF16)
                s = jnp.dot(wt_ref[g], vg, preferred_element_type=F32) + bs_ref[:, g:g + 1]
                ug = u_ref[pl.ds(r0, rows), sl].astype(F32)
                gated_ref[pl.ds(r0, rows), sl] = (ug * s).astype(BF16)
            return carry

        lax.fori_loop(0, tm // rows, mix, 0)

    o_ref[...] = h_ref[...] + jnp.dot(gated_ref[...], _mxu_weight(w_ref, wb_ref),
                                      preferred_element_type=F32)


def _gmlp_out(z, ws, bs_t, h, wl, *, tm, tn, rows, chunk, emit=False):
    w_out, l = wl
    M = z.shape[0]
    N = w_out.shape[2]
    wb_specs, wb_shapes = _weight_copy_out(emit, M, tm, D_GATE, N, D_GATE, tn, lambda i, j: (0, j))
    outs = pl.pallas_call(
        functools.partial(_gmlp_out_kernel, rows=rows, chunk=chunk, emit=emit),
        grid=(M // tm, N // tn),
        in_specs=[
            pl.BlockSpec((tm, D_GATE), lambda i, j: (i, 0)),
            pl.BlockSpec((tm, D_GATE), lambda i, j: (i, 1)),
            pl.BlockSpec((N_GROUPS_A, rows, rows), lambda i, j: (0, 0, 0)),
            pl.BlockSpec((rows, N_GROUPS_A), lambda i, j: (0, 0)),
            pl.BlockSpec((tm, tn), lambda i, j: (i, j)),
            pl.BlockSpec((None, D_GATE, tn), lambda i, j: (l, 0, j)),
        ],
        out_specs=[pl.BlockSpec((tm, tn), lambda i, j: (i, j))] + wb_specs,
        out_shape=[jax.ShapeDtypeStruct((M, N), F32)] + wb_shapes,
        scratch_shapes=[pltpu.VMEM((tm, D_GATE), BF16),
                        pltpu.VMEM((N_GROUPS_A, rows, rows), BF16)],
        compiler_params=_params(2),
        name="gmlp_out",
    )(z, z, ws, bs_t, h, w_out)
    return (outs[0], outs[1][None]) if emit else outs[0]


def _gmlp_gate_kernel(x_ref, g_ref, wu_ref, wv_ref, ws_ref, bs_ref, o_ref, xn_ref):
    @pl.when(pl.program_id(1) == 0)
    def _():
        xn_ref[...] = _rms_rows(x_ref[...], g_ref[...]).astype(BF16)

    def gelu(y):
        return 0.5 * y * (1.0 + lax.erf(y * (2.0 ** -0.5)))

    xn = xn_ref[...]
    u = gelu(jnp.dot(xn, wu_ref[...], preferred_element_type=F32))
    v = gelu(jnp.dot(xn, wv_ref[...], preferred_element_type=F32)).astype(BF16)
    r = lax.broadcasted_iota(jnp.int32, (CHUNK, CHUNK), 0)
    c = lax.broadcasted_iota(jnp.int32, (CHUNK, CHUNK), 1)
    for gg in range(ws_ref.shape[0]):
        wt = jnp.where(r >= c, ws_ref[gg], 0.0).astype(BF16)
        b = bs_ref[gg]
        b = jnp.concatenate([b] * (GROUP_DIM_A // LANES), axis=1)
        cs = slice(gg * GROUP_DIM_A, (gg + 1) * GROUP_DIM_A)
        for ci in range(u.shape[0] // CHUNK):
            rs = slice(ci * CHUNK, (ci + 1) * CHUNK)
            s = jnp.dot(wt, v[rs, cs], preferred_element_type=F32) + b
            o_ref[rs, cs] = (u[rs, cs] * s).astype(o_ref.dtype)


def _gmlp_gate(x, gl, wl, ws, bs, *, tm, tn):
    g, lg = gl
    w, l = wl
    M, K = x.shape
    gpt = tn // GROUP_DIM_A
    nt = D_GATE // tn
    bs_rep = jnp.broadcast_to(bs[:, :, None], bs.shape + (LANES,))
    return pl.pallas_call(
        _gmlp_gate_kernel,
        grid=(M // tm, nt),
        in_specs=[
            pl.BlockSpec((tm, K), lambda i, j: (i, 0)),
            pl.BlockSpec((None, 1, K), lambda i, j: (lg, 0, 0)),
            pl.BlockSpec((None, K, tn), lambda i, j: (l, 0, j)),
            pl.BlockSpec((None, K, tn), lambda i, j: (l, 0, nt + j)),
            pl.BlockSpec((gpt, CHUNK, CHUNK), lambda i, j: (j, 0, 0)),
            pl.BlockSpec((gpt, CHUNK, LANES), lambda i, j: (j, 0, 0)),
        ],
        out_specs=pl.BlockSpec((tm, tn), lambda i, j: (i, j)),
        out_shape=jax.ShapeDtypeStruct((M, D_GATE), BF16),
        scratch_shapes=[pltpu.VMEM((tm, K), BF16)],
        compiler_params=_params(2),
        name="gmlp_gate",
    )(x, g, w, w, ws, bs_rep)


def _ffn_kernel(*refs, final_norm, emit):
    x_ref, g_ref, wu_ref, wd_ref = refs[:4]
    refs = refs[4:]
    if final_norm:
        gf_ref = refs[0]
        refs = refs[1:]
    o_ref = refs[0]
    wub_ref, wdb_ref = refs[1:3] if emit else (None, None)
    xn_ref = refs[-1]
    f = pl.program_id(1)

    @pl.when(f == 0)
    def _():
        x = x_ref[...]
        xn_ref[...] = _rms_rows(x, g_ref[...]).astype(BF16)
        o_ref[...] = x

    a = jnp.dot(xn_ref[...], _mxu_weight(wu_ref, wub_ref), preferred_element_type=F32)
    a = jnp.square(jnp.maximum(a, 0.0)).astype(BF16)
    o_ref[...] += jnp.dot(a, _mxu_weight(wd_ref, wdb_ref), preferred_element_type=F32)

    if final_norm:
        @pl.when(f == pl.num_programs(1) - 1)
        def _():
            o_ref[...] = _rms_rows(o_ref[...], gf_ref[...])


def _ffn(x, gl, wul, wdl, g_final, *, tm, tf, emit=False):
    g, l = gl
    w_up, lu = wul
    w_down, ld = wdl
    M, K = x.shape
    F = w_up.shape[2]
    final_norm = g_final is not None
    in_specs = [
        pl.BlockSpec((tm, K), lambda i, f: (i, 0)),
        pl.BlockSpec((None, 1, K), lambda i, f: (l, 0, 0)),
        pl.BlockSpec((None, K, tf), lambda i, f: (lu, 0, f)),
        pl.BlockSpec((None, tf, K), lambda i, f: (ld, f, 0)),
    ]
    ub_specs, ub_shapes = _weight_copy_out(emit, M, tm, K, F, K, tf, lambda i, f: (0, f))
    db_specs, db_shapes = _weight_copy_out(emit, M, tm, F, K, tf, K, lambda i, f: (f, 0))
    args = [x, g, w_up, w_down]
    if final_norm:
        in_specs.append(pl.BlockSpec((1, K), lambda i, f: (0, 0)))
        args.append(g_final)
    outs = pl.pallas_call(
        functools.partial(_ffn_kernel, final_norm=final_norm, emit=emit),
        grid=(M // tm, F // tf),
        in_specs=in_specs,
        out_specs=[pl.BlockSpec((tm, K), lambda i, f: (i, 0))] + ub_specs + db_specs,
        out_shape=[jax.ShapeDtypeStruct((M, K), F32)] + ub_shapes + db_shapes,
        scratch_shapes=[pltpu.VMEM((tm, K), BF16)],
        compiler_params=_params(2),
        name="ffn",
    )(*args)
    return (outs[0], outs[1][None], outs[2][None]) if emit else outs[0]


def _head_rows(ref, h, n_heads):
    return ref[pl.ds(h, PAGE_SIZE, stride=n_heads), :].astype(BF16)


def _page_k(k_ref):
    return jnp.concatenate([_head_rows(k_ref, h, N_QK) for h in range(N_QK)], axis=1)


def _matmul_res_kernel(x_ref, w_ref, h_ref, o_ref, wb_ref=None):
    o_ref[...] = h_ref[...] + jnp.dot(x_ref[...], _mxu_weight(w_ref, wb_ref), preferred_element_type=F32)


def _matmul_res(x, wl, h, *, tm, tn, emit=False):
    w, l = wl
    M, K = x.shape
    N = w.shape[2]
    wb_specs, wb_shapes = _weight_copy_out(emit, M, tm, K, N, K, tn, lambda i, j: (0, j))
    outs = pl.pallas_call(
        _matmul_res_kernel,
        grid=(M // tm, N // tn),
        in_specs=[
            pl.BlockSpec((tm, K), lambda i, j: (i, 0)),
            pl.BlockSpec((None, K, tn), lambda i, j: (l, 0, j)),
            pl.BlockSpec((tm, tn), lambda i, j: (i, j)),
        ],
        out_specs=[pl.BlockSpec((tm, tn), lambda i, j: (i, j))] + wb_specs,
        out_shape=[jax.ShapeDtypeStruct((M, N), F32)] + wb_shapes,
        compiler_params=_params(2),
        name="matmul_res",
    )(x, w, h)
    return (outs[0], outs[1][None]) if emit else outs[0]


def _lane_chunks(s):
    return [s[:, c * LANES:(c + 1) * LANES] for c in range(s.shape[1] // LANES)]


def _chunk_max(chunks):
    m = chunks[0]
    for c in chunks[1:]:
        m = jnp.maximum(m, c)
    return m


def _softmax_step(s_list, m_ref, l_ref, m_blk=None):
    cols = [_lane_chunks(s) for s in s_list]
    if m_blk is None:
        m_blk = _chunk_max([c for cs in cols for c in cs])
    m_prev = m_ref[...]
    m_new = jnp.maximum(m_prev, jnp.max(m_blk, axis=-1, keepdims=True))
    alpha = jnp.exp2(m_prev - m_new)
    probs = [[jnp.exp2(c - m_new) for c in cs] for cs in cols]
    flat = [p for ps in probs for p in ps]
    l_blk = flat[0]
    for p in flat[1:]:
        l_blk = l_blk + p
    l_ref[...] = alpha * l_ref[...] + jnp.sum(l_blk, axis=-1, keepdims=True)
    m_ref[...] = m_new
    return [jnp.concatenate(ps, axis=1).astype(BF16) if len(ps) > 1 else ps[0].astype(BF16)
            for ps in probs], alpha


def _head_out(o1, o2, lam, subln, lam_init):
    d = o1 - lam * o2
    return _rms_rows(d, subln) * (1.0 - lam_init)


def _attn_prompt_kernel(qi_tab, kj_tab, q0_ref, k0_ref, qn_ref, kn_ref, v_ref, lam_ref, subln_ref, o_ref,
                        m_ref, l_ref, acc_ref, sa_ref, sb_ref, ma_ref, mb_ref, p_ref, a_ref, *, lam_init):
    step = pl.program_id(2)
    n_steps = pl.num_programs(2)
    qi = qi_tab[step]
    kj = kj_tab[step]
    nxt = jnp.where(step + 1 == n_steps, 0, step + 1)
    next_masked = qi_tab[nxt] == kj_tab[nxt]
    tq = qn_ref.shape[1]
    tk = kn_ref.shape[1]
    n_sub = m_ref.shape[0]

    def scores(q_ref, k_ref, bufs, masked):
        s_ref, mx_ref = bufs
        for sub in range(n_sub):
            sl = slice(sub * HEAD_DIM, (sub + 1) * HEAD_DIM)
            q = q_ref[0, :, sl]
            k = k_ref[0, :, sl].astype(BF16)
            s = lax.dot_general(q, k, (((1,), (1,)), ((), ())), preferred_element_type=F32)
            if masked:
                r = lax.broadcasted_iota(jnp.int32, (tq, tk), 0)
                c = lax.broadcasted_iota(jnp.int32, (tq, tk), 1)
                s = jnp.where(c <= r, s, NEG_INF)
            s_ref[sub] = s
            mx_ref[sub] = _chunk_max(_lane_chunks(s))

    def accumulate(bufs):
        s_ref, mx_ref = bufs
        for sub in range(n_sub):
            p, alpha = _softmax_step([s_ref[sub]], m_ref.at[sub], l_ref.at[sub], mx_ref[sub])
            p_ref[sub] = p[0]
            a_ref[sub] = alpha
        for sub in range(n_sub):
            vs = slice((sub // 2) * V_DIM, (sub // 2 + 1) * V_DIM)
            alpha = a_ref[sub]
            acc_ref[sub] = (jnp.concatenate([alpha, alpha], axis=1) * acc_ref[sub]
                            + jnp.dot(p_ref[sub], v_ref[0, :, vs].astype(BF16), preferred_element_type=F32))

    @pl.when((pl.program_id(0) == 0) & (pl.program_id(1) == 0) & (step == 0))
    def _():
        scores(q0_ref, k0_ref, (sa_ref, ma_ref), True)

    @pl.when(kj == 0)
    def _():
        m_ref[...] = jnp.full(m_ref.shape, NEG_INF, F32)
        l_ref[...] = jnp.zeros(l_ref.shape, F32)
        acc_ref[...] = jnp.zeros(acc_ref.shape, F32)

    for parity, (cur_ref, nxt_ref) in enumerate((((sa_ref, ma_ref), (sb_ref, mb_ref)),
                                                 ((sb_ref, mb_ref), (sa_ref, ma_ref)))):
        for masked in (False, True):
            @pl.when((step % 2 == parity) & (next_masked == masked))
            def _(cur_ref=cur_ref, nxt_ref=nxt_ref, masked=masked):
                scores(qn_ref, kn_ref, nxt_ref, masked)
                accumulate(cur_ref)

    @pl.when(kj == qi)
    def _():
        lam = _lambda(lam_ref, lam_init)
        for hv in range(n_sub // 2):
            inv1 = 1.0 / l_ref[2 * hv]
            inv2 = 1.0 / l_ref[2 * hv + 1]
            o1 = acc_ref[2 * hv] * jnp.concatenate([inv1, inv1], axis=1)
            o2 = acc_ref[2 * hv + 1] * jnp.concatenate([inv2, inv2], axis=1)
            o_ref[0, :, hv * V_DIM:(hv + 1) * V_DIM] = _head_out(
                o1, o2, lam, subln_ref[...], lam_init).astype(o_ref.dtype)


def _attn_prompt(q, k, v, lam_vecs, subln, l, lam_init, *, tq, heads_per_step):
    B, T, _ = q.shape
    nq = T // tq
    pairs = [(i, j) for i in range(nq) for j in range(i + 1)]
    qi_tab = jnp.asarray([p[0] for p in pairs], jnp.int32)
    kj_tab = jnp.asarray([p[1] for p in pairs], jnp.int32)
    hw = heads_per_step * V_DIM
    n_sub = 2 * heads_per_step
    n_hg = N_HEADS // heads_per_step
    n_steps = len(pairs)
    assert n_steps % 2 == 0

    def next_block(tab):
        def index_map(b, h, s, qt, kt):
            wrap_s = (s + 1 == n_steps).astype(jnp.int32)
            s_n = (s + 1) * (1 - wrap_s)
            wrap_h = ((h + wrap_s) == n_hg).astype(jnp.int32)
            h_n = (h + wrap_s) * (1 - wrap_h)
            b_n = jnp.minimum(b + wrap_h, B - 1)
            return (b_n, (qt if tab == "q" else kt)[s_n], h_n)
        return index_map

    grid_spec = pltpu.PrefetchScalarGridSpec(
        num_scalar_prefetch=2,
        grid=(B, n_hg, n_steps),
        in_specs=[
            pl.BlockSpec((1, tq, hw), lambda b, h, s, qt, kt: (0, 0, 0)),
            pl.BlockSpec((1, tq, hw), lambda b, h, s, qt, kt: (0, 0, 0)),
            pl.BlockSpec((1, tq, hw), next_block("q")),
            pl.BlockSpec((1, tq, hw), next_block("k")),
            pl.BlockSpec((1, tq, hw), lambda b, h, s, qt, kt: (b, kt[s], h)),
            pl.BlockSpec((None, 4, HEAD_DIM), lambda b, h, s, qt, kt: (l, 0, 0)),
            pl.BlockSpec((None, 1, V_DIM), lambda b, h, s, qt, kt: (l, 0, 0)),
        ],
        out_specs=pl.BlockSpec((1, tq, hw), lambda b, h, s, qt, kt: (b, qt[s], h)),
        scratch_shapes=[pltpu.VMEM((n_sub, tq, LANES), F32), pltpu.VMEM((n_sub, tq, LANES), F32),
                        pltpu.VMEM((n_sub, tq, V_DIM), F32),
                        pltpu.VMEM((n_sub, tq, tq), F32), pltpu.VMEM((n_sub, tq, tq), F32),
                        pltpu.VMEM((n_sub, tq, LANES), F32), pltpu.VMEM((n_sub, tq, LANES), F32),
                        pltpu.VMEM((n_sub, tq, tq), BF16), pltpu.VMEM((n_sub, tq, LANES), F32)],
    )
    return pl.pallas_call(
        functools.partial(_attn_prompt_kernel, lam_init=lam_init),
        grid_spec=grid_spec,
        out_shape=jax.ShapeDtypeStruct((B, T, N_HEADS * V_DIM), BF16),
        compiler_params=_params(3),
        name="attn_prompt",
    )(qi_tab, kj_tab, q, k, q, k, v, lam_vecs, subln)


def _attn_sample_step(load_tiles, q_ref, kn_ref, vn_ref, lam_ref, subln_ref, o_ref,
                      qbd_ref, m_ref, l_ref, acc_ref, lam_init):
    p_idx = pl.program_id(1)
    tq = q_ref.shape[1]
    rows = N_QK * tq
    d_all = N_QK * HEAD_DIM

    @pl.when(p_idx == 0)
    def _():
        m_ref[...] = jnp.full(m_ref.shape, NEG_INF, F32)
        l_ref[...] = jnp.zeros(l_ref.shape, F32)
        acc_ref[...] = jnp.zeros(acc_ref.shape, F32)
        qt = jnp.concatenate([q_ref[0]] * N_QK, axis=0)
        r = lax.broadcasted_iota(jnp.int32, (rows, d_all), 0)
        c = lax.broadcasted_iota(jnp.int32, (rows, d_all), 1)
        qbd_ref[...] = jnp.where(r // tq == c // HEAD_DIM, qt, 0.0).astype(BF16)

    def update(k_list, v_list, mask):
        qbd = qbd_ref[...]
        s_list = []
        for kp in k_list:
            s = lax.dot_general(qbd, kp, (((1,), (1,)), ((), ())), preferred_element_type=F32)
            if mask is not None:
                s = jnp.where(mask, s, NEG_INF)
            s_list.append(s)
        p_list, alpha = _softmax_step(s_list, m_ref, l_ref)
        for hv in range(N_HEADS):
            rs = slice(hv * 2 * tq, (hv + 1) * 2 * tq)
            pv = None
            for p, v_heads in zip(p_list, v_list):
                t = jnp.dot(p[rs, :], v_heads[hv], preferred_element_type=F32)
                pv = t if pv is None else pv + t
            a = alpha[rs, :]
            acc_ref[rs, :] = jnp.concatenate([a, a], axis=1) * acc_ref[rs, :] + pv

    update(*load_tiles(), None)

    @pl.when(p_idx == pl.num_programs(1) - 1)
    def _():
        pad = jnp.zeros((PAGE_SIZE - tq, d_all), F32)
        kn = jnp.concatenate([kn_ref[0], pad], axis=0).astype(BF16)
        vn = jnp.concatenate([vn_ref[0], pad], axis=0).astype(BF16)
        r = lax.broadcasted_iota(jnp.int32, (rows, PAGE_SIZE), 0)
        c = lax.broadcasted_iota(jnp.int32, (rows, PAGE_SIZE), 1)
        update([kn], [[vn[:, hv * V_DIM:(hv + 1) * V_DIM] for hv in range(N_HEADS)]], c <= r % tq)
        lam = _lambda(lam_ref, lam_init)
        inv = 1.0 / l_ref[...]
        o = acc_ref[...] * jnp.concatenate([inv, inv], axis=1)
        for hv in range(N_HEADS):
            o1 = o[hv * 2 * tq:hv * 2 * tq + tq, :]
            o2 = o[hv * 2 * tq + tq:(hv + 1) * 2 * tq, :]
            o_ref[0, :, hv * V_DIM:(hv + 1) * V_DIM] = _head_out(
                o1, o2, lam, subln_ref[...], lam_init).astype(o_ref.dtype)


def _attn_sample_kernel(pt_ref, q_ref, *refs, pages, lam_init):
    k_refs = refs[:pages]
    v_refs = refs[pages:3 * pages]

    def load_tiles():
        ks = jnp.concatenate([_page_k(k) for k in k_refs], axis=0)
        vs = [jnp.concatenate(
            [jnp.concatenate([_head_rows(r, hv, N_HEADS) for r in v_refs[2 * i:2 * i + 2]], axis=1)
             for i in range(pages)], axis=0) for hv in range(N_HEADS)]
        return [ks], [vs]

    _attn_sample_step(load_tiles, q_ref, *refs[3 * pages:], lam_init)


def _attn_sample(q, cache_k, cache_v, page_table, k_new, v_new, lam_vecs, subln, l, lam_init, *,
                 pages_per_step):
    Bd, Tq, D = q.shape
    n_pages = page_table.shape[1]
    pps = pages_per_step

    def page_map(i, half):
        return lambda b, p, pt: (pt[b * n_pages + p * pps + i], 0, half)

    k_specs = [pl.BlockSpec((None, PAGE_SIZE * N_QK, HEAD_DIM), page_map(i, 0)) for i in range(pps)]
    v_specs = [pl.BlockSpec((None, PAGE_SIZE * N_HEADS, LANES), page_map(i, half))
               for i in range(pps) for half in range(V_DIM // LANES)]
    row_spec = pl.BlockSpec((1, Tq, D), lambda b, p, pt: (b, 0, 0))
    grid_spec = pltpu.PrefetchScalarGridSpec(
        num_scalar_prefetch=1,
        grid=(Bd, n_pages // pps),
        in_specs=[row_spec] + k_specs + v_specs + [
            row_spec, row_spec,
            pl.BlockSpec((None, 4, HEAD_DIM), lambda b, p, pt: (l, 0, 0)),
            pl.BlockSpec((None, 1, V_DIM), lambda b, p, pt: (l, 0, 0)),
        ],
        out_specs=row_spec,
        scratch_shapes=[pltpu.VMEM((N_QK * Tq, D), BF16),
                        pltpu.VMEM((N_QK * Tq, LANES), F32), pltpu.VMEM((N_QK * Tq, LANES), F32),
                        pltpu.VMEM((N_QK * Tq, V_DIM), F32)],
    )
    return pl.pallas_call(
        functools.partial(_attn_sample_kernel, pages=pps, lam_init=lam_init),
        grid_spec=grid_spec,
        out_shape=jax.ShapeDtypeStruct((Bd, Tq, D), BF16),
        compiler_params=_params(2),
        name="attn_sample",
    )(page_table.reshape(-1), q, *([cache_k] * pps), *([cache_v] * (2 * pps)), k_new, v_new, lam_vecs, subln)


def _rope_tables(pos):
    half = HEAD_DIM // 2
    inv = ROPE_THETA ** (-jnp.arange(half, dtype=F32) / half)
    ang = pos.astype(F32)[:, None] * inv[None, :]
    cos = jnp.cos(ang)
    sin = jnp.sin(ang)
    return jnp.concatenate([cos, cos], axis=-1), jnp.concatenate([-sin, sin], axis=-1)


class _Group:
    def __init__(self, x, pos, cfg):
        self.B, self.T, _ = x.shape
        self.h = x.reshape(self.B * self.T, D_MODEL)
        self.cfg = cfg
        cos, sin = _rope_tables(pos)
        if self.T < cfg["tm"]:
            cos = jnp.tile(cos, (cfg["tm"] // self.T, 1))
            sin = jnp.tile(sin, (cfg["tm"] // self.T, 1))
        self.rope = (cos, sin)
        self.v_rows = []
        self.k_sh = self.v_sh = self.k_mxu = self.v_mxu = None


def _layer(G, l, p, W, Wb=None, paged=None):
    cfg, B, T, h = G.cfg, G.B, G.T, G.h
    M = B * T
    tm, rows, chunk = cfg["tm"], cfg["rows"], cfg["chunk"]
    cos, sin = G.rope
    emit = Wb is not None

    def keep(name, outs):
        if not emit:
            return outs
        for n, wb in zip(name.split(","), outs[1:]):
            Wb[n].append((wb, 0))
        return outs[0]

    if l < N_A_LAYERS and cfg["fused_gate"]:
        assert not emit and chunk == CHUNK
        gated = _gmlp_gate(h, p["norm_a"][l], W["w_in_a"][l], p["w_s_a"][l], p["b_s_a"][l],
                           tm=cfg["tm_in"], tn=cfg["tn_in"])
        h = _matmul_res(gated, W["w_out_a"][l], h, tm=cfg["tm_gmlp"], tn=cfg["tn_gmlp"])
    elif l < N_A_LAYERS:
        z = keep("w_in_a", _norm_matmul(h, p["norm_a"][l], W["w_in_a"][l], tm=cfg["tm_in"], tn=cfg["tn_in"],
                                        out_dtype=cfg["z_dtype"], epilogue="gelu", emit=emit))
        ws = p["w_s_a"][l][:, :chunk, :chunk]
        bs_t = p["b_s_a"][l][:, :chunk].T
        if rows > chunk:
            ws = jnp.tile(ws, (1, rows // chunk, rows // chunk))
            bs_t = jnp.tile(bs_t, (rows // chunk, 1))
        h = keep("w_out_a", _gmlp_out(z, ws, bs_t, h, W["w_out_a"][l], tm=cfg["tm_gmlp"], tn=cfg["tn_gmlp"],
                                      rows=rows, chunk=chunk, emit=emit))
        G.v_rows.append(z[:, D_GATE:])
    else:
        if l == N_A_LAYERS:
            dup = paged is None
            k_sh = keep("w_k", _norm_matmul(h, p["norm_kv"][0], W["w_k"][0], tm=tm, tn=cfg["tn"], out_dtype=F32,
                                            epilogue="rope", rope=(cos, sin), emit=emit, dup=dup))
            v_sh = keep("w_v", _norm_matmul(h, p["norm_kv"][0], W["w_v"][0], tm=tm, tn=cfg["tn"], out_dtype=F32,
                                            emit=emit, dup=dup))
            if dup:
                (k_sh, G.k_mxu), (v_sh, G.v_mxu) = k_sh, v_sh
            G.k_sh, G.v_sh = k_sh, v_sh
        j = l - N_A_LAYERS
        lam_init = 0.8 - 0.6 * math.exp(-0.3 * l)
        q = keep("w_q", _norm_matmul(h, p["norm_b"][j], W["w_q"][j], tm=tm, tn=cfg["tn"],
                                     out_dtype=cfg["q_dtype"], epilogue="rope", rope=(cos, sin),
                                     scale=HEAD_DIM ** -0.5 * LOG2E, emit=emit))
        if paged is None:
            o = _attn_prompt(q.reshape(B, T, -1), G.k_mxu.reshape(B, T, -1), G.v_mxu.reshape(B, T, -1),
                             p["lam_vecs"], p["subln_b"], j, lam_init, tq=cfg["tq"],
                             heads_per_step=cfg["hps"])
        else:
            o = _attn_sample(q.reshape(B, T, -1), *paged, G.k_sh.reshape(B, T, -1), G.v_sh.reshape(B, T, -1),
                             p["lam_vecs"], p["subln_b"], j, lam_init, pages_per_step=cfg["pps"])
        h = keep("w_o_b", _matmul_res(o.reshape(M, -1), W["w_o_b"][j], h, tm=tm, tn=cfg["tn"], emit=emit))
    G.h = keep("w_up,w_down", _ffn(h, p["norm_ffn"][l], W["w_up"][l], W["w_down"][l],
                                   p["norm_f"] if l == DEPTH - 1 else None,
                                   tm=cfg["tm_ffn"], tf=cfg["tf"], emit=emit))


def kernel(x_prompt, x_sample, cache_k, cache_v, page_table, norm_a, w_in_a, w_s_a, b_s_a, w_out_a, norm_kv, w_k, w_v, norm_b, w_q, lambda_q1, lambda_k1, lambda_q2, lambda_k2, subln_b, w_o_b, norm_ffn, w_up, w_down, norm_f):
    def layers(a):
        return [(a, l) for l in range(a.shape[0])]

    def gains(g):
        return layers(g.reshape(g.shape[0], 1, g.shape[1]))

    p = dict(norm_a=gains(norm_a), w_s_a=w_s_a, b_s_a=b_s_a, norm_kv=gains(norm_kv[None]),
             norm_b=gains(norm_b),
             lam_vecs=jnp.stack([lambda_q1, lambda_k1, lambda_q2, lambda_k2], axis=1),
             subln_b=subln_b.reshape(subln_b.shape[0], 1, -1), norm_ffn=gains(norm_ffn),
             norm_f=norm_f.reshape(1, -1))
    W = dict(w_in_a=layers(w_in_a), w_out_a=layers(w_out_a), w_k=layers(w_k[None]), w_v=layers(w_v[None]),
             w_q=layers(w_q), w_o_b=layers(w_o_b), w_up=layers(w_up), w_down=layers(w_down))

    B, T, _ = x_prompt.shape
    Bd, Td, _ = x_sample.shape

    Ms = Bd * Td
    cfg_s = dict(tm=Ms, tn=1024, tm_in=Ms, tn_in=1024, tm_gmlp=Ms, tn_gmlp=512, tm_ffn=Ms, tf=512,
                 pps=8, rows=Ms, chunk=Td, z_dtype=F32, q_dtype=F32, fused_gate=False)
    cfg_p = dict(tm=512, tn=2048, tm_in=1024, tn_in=1024, tm_gmlp=1024, tn_gmlp=1024, tm_ffn=512, tf=2048,
                 tq=512, hps=4, rows=CHUNK, chunk=CHUNK, q_dtype=BF16, fused_gate=True)
    S = _Group(x_sample, PAST_LEN + jnp.arange(Td, dtype=jnp.int32), cfg_s)
    P = _Group(x_prompt, jnp.arange(T, dtype=jnp.int32), cfg_p)

    n_pool = cache_k.shape[0]
    paged = (cache_k.reshape(n_pool, PAGE_SIZE * N_QK, HEAD_DIM),
             cache_v.reshape(n_pool, PAGE_SIZE * N_HEADS, V_DIM), page_table)
    Wb = {name: [] for name in W}
    for l in range(DEPTH):
        _layer(S, l, p, W, Wb=Wb, paged=paged)
    for l in range(DEPTH):
        _layer(P, l, p, Wb)

    return (P.h.reshape(B, T, D_MODEL), S.h.reshape(Bd, Td, D_MODEL),
            P.k_sh.reshape(B, T, N_QK, HEAD_DIM), P.v_sh.reshape(B, T, N_HEADS, V_DIM),
            S.k_sh.reshape(Bd, Td, N_QK, HEAD_DIM), S.v_sh.reshape(Bd, Td, N_HEADS, V_DIM),
            jnp.stack(S.v_rows).reshape(N_A_LAYERS, Bd, Td, D_GATE))
```

```python
import functools
import math

import jax
import jax.numpy as jnp
from jax import lax
from jax.experimental import pallas as pl
from jax.experimental.pallas import tpu as pltpu

D_MODEL = 2048
DEPTH = 4
PAST_LEN = 16384
PAGE_SIZE = 128
N_A_LAYERS = DEPTH // 2
CHUNK = 128
D_GATE = 2 * D_MODEL
N_GROUPS_A = 16
GROUP_DIM_A = D_GATE // N_GROUPS_A
HEAD_DIM = 128
N_HEADS = D_MODEL // (2 * HEAD_DIM)
N_QK = 2 * N_HEADS
V_DIM = 2 * HEAD_DIM
D_FF = 4 * D_MODEL
ROPE_THETA = 10000.0
EPS = 1e-5
NEG_INF = -1e30

LANES = 128
VMEM_LIMIT = 60 * 1024 * 1024
LOG2E = math.log2(math.e)

F32 = jnp.float32
BF16 = jnp.bfloat16


def _params(n_axes):
    return pltpu.CompilerParams(dimension_semantics=("arbitrary",) * n_axes,
                                vmem_limit_bytes=VMEM_LIMIT)


def _rms_rows(x, g):
    return x * lax.rsqrt(jnp.mean(x * x, axis=-1, keepdims=True) + EPS) * g


def _lambda(lam_ref, lam_init):
    a = jnp.sum(lam_ref[0:1, :] * lam_ref[1:2, :], axis=-1, keepdims=True)
    b = jnp.sum(lam_ref[2:3, :] * lam_ref[3:4, :], axis=-1, keepdims=True)
    return jnp.exp(a) - jnp.exp(b) + lam_init


def _mxu_weight(w_ref, wb_ref):
    w = w_ref[...].astype(BF16)
    if wb_ref is not None:
        wb_ref[...] = w
    return w


def _norm_matmul_kernel(*refs, epilogue, scale, emit, dup):
    x_ref, g_ref, w_ref = refs[:3]
    refs = refs[3:]
    if epilogue == "rope":
        cos_ref, sin_ref = refs[:2]
        refs = refs[2:]
    out_refs = refs[:2] if dup else refs[:1]
    wb_ref = refs[len(out_refs)] if emit else None
    xn_ref = refs[-1]

    def put(sl, val):
        for o_ref in out_refs:
            o_ref[:, sl] = val.astype(o_ref.dtype)

    @pl.when(pl.program_id(1) == 0)
    def _():
        xn_ref[...] = _rms_rows(x_ref[...], g_ref[...]).astype(BF16)

    y = jnp.dot(xn_ref[...], _mxu_weight(w_ref, wb_ref), preferred_element_type=F32)
    if epilogue == "gelu":
        put(slice(None), 0.5 * y * (1.0 + lax.erf(y * (2.0 ** -0.5))))
    elif epilogue == "rope":
        cos = cos_ref[...]
        sin = sin_ref[...]
        for h in range(y.shape[1] // HEAD_DIM):
            sl = slice(h * HEAD_DIM, (h + 1) * HEAD_DIM)
            yh = y[:, sl]
            oh = yh * cos + pltpu.roll(yh, HEAD_DIM // 2, 1) * sin
            if scale != 1.0:
                oh = oh * scale
            put(sl, oh)
    else:
        put(slice(None), y)


def _weight_copy_out(emit, M, tm, K, N, bk, bn, index_map):
    if not emit:
        return [], []
    assert M == tm
    return [pl.BlockSpec((bk, bn), index_map)], [jax.ShapeDtypeStruct((K, N), BF16)]


def _norm_matmul(x, gl, wl, *, tm, tn, out_dtype, epilogue=None, rope=None, scale=1.0, emit=False,
                 dup=False):
    assert not (emit and dup)
    g, lg = gl
    w, l = wl
    M, K = x.shape
    N = w.shape[2]
    wb_specs, wb_shapes = _weight_copy_out(emit, M, tm, K, N, K, tn, lambda i, j: (0, j))
    in_specs = [
        pl.BlockSpec((tm, K), lambda i, j: (i, 0)),
        pl.BlockSpec((None, 1, K), lambda i, j: (lg, 0, 0)),
        pl.BlockSpec((None, K, tn), lambda i, j: (l, 0, j)),
    ]
    args = [x, g, w]
    if epilogue == "rope":
        cos, sin = rope
        nb = cos.shape[0] // tm
        in_specs += [pl.BlockSpec((tm, HEAD_DIM), lambda i, j: (i % nb, 0))] * 2
        args += [cos, sin]
    outs = pl.pallas_call(
        functools.partial(_norm_matmul_kernel, epilogue=epilogue, scale=scale, emit=emit, dup=dup),
        grid=(M // tm, N // tn),
        in_specs=in_specs,
        out_specs=[pl.BlockSpec((tm, tn), lambda i, j: (i, j))] * (2 if dup else 1) + wb_specs,
        out_shape=[jax.ShapeDtypeStruct((M, N), out_dtype)]
        + ([jax.ShapeDtypeStruct((M, N), BF16)] if dup else []) + wb_shapes,
        scratch_shapes=[pltpu.VMEM((tm, K), BF16)],
        compiler_params=_params(2),
        name="norm_matmul_" + (epilogue or "plain"),
    )(*args)
    if emit:
        return outs[0], outs[1][None]
    return (outs[0], outs[1]) if dup else outs[0]


def _gmlp_out_kernel(u_ref, v_ref, ws_ref, bs_ref, h_ref, w_ref, o_ref, *refs, rows, chunk, emit):
    wb_ref = refs[0] if emit else None
    gated_ref, wt_ref = refs[-2:]
    tm = u_ref.shape[0]

    @pl.when(pl.program_id(1) == 0)
    def _():
        r = lax.broadcasted_iota(jnp.int32, (rows, rows), 0)
        c = lax.broadcasted_iota(jnp.int32, (rows, rows), 1)
        mask = (r // chunk == c // chunk) & (r >= c)
        for g in range(N_GROUPS_A):
            wt_ref[g] = jnp.where(mask, ws_ref[g], 0.0).astype(BF16)

        def mix(ci, carry):
            r0 = pl.multiple_of(ci * rows, rows)
            for g in range(N_GROUPS_A):
                sl = slice(g * GROUP_DIM_A, (g + 1) * GROUP_DIM_A)
                vg = v_ref[pl.ds(r0, rows), sl].astype(BF16)
                s = jnp.dot(wt_ref[g], vg, preferred_element_type=F32) + bs_ref[:, g:g + 1]
                ug = u_ref[pl.ds(r0, rows), sl].astype(F32)
                gated_ref[pl.ds(r0, rows), sl] = (ug * s).astype(BF16)
            return carry

        lax.fori_loop(0, tm // rows, mix, 0)

    o_ref[...] = h_ref[...] + jnp.dot(gated_ref[...], _mxu_weight(w_ref, wb_ref),
                                      preferred_element_type=F32)


def _gmlp_out(z, ws, bs_t, h, wl, *, tm, tn, rows, chunk, emit=False):
    w_out, l = wl
    M = z.shape[0]
    N = w_out.shape[2]
    wb_specs, wb_shapes = _weight_copy_out(emit, M, tm, D_GATE, N, D_GATE, tn, lambda i, j: (0, j))
    outs = pl.pallas_call(
        functools.partial(_gmlp_out_kernel, rows=rows, chunk=chunk, emit=emit),
        grid=(M // tm, N // tn),
        in_specs=[
            pl.BlockSpec((tm, D_GATE), lambda i, j: (i, 0)),
            pl.BlockSpec((tm, D_GATE), lambda i, j: (i, 1)),
            pl.BlockSpec((N_GROUPS_A, rows, rows), lambda i, j: (0, 0, 0)),
            pl.BlockSpec((rows, N_GROUPS_A), lambda i, j: (0, 0)),
            pl.BlockSpec((tm, tn), lambda i, j: (i, j)),
            pl.BlockSpec((None, D_GATE, tn), lambda i, j: (l, 0, j)),
        ],
        out_specs=[pl.BlockSpec((tm, tn), lambda i, j: (i, j))] + wb_specs,
        out_shape=[jax.ShapeDtypeStruct((M, N), F32)] + wb_shapes,
        scratch_shapes=[pltpu.VMEM((tm, D_GATE), BF16),
                        pltpu.VMEM((N_GROUPS_A, rows, rows), BF16)],
        compiler_params=_params(2),
        name="gmlp_out",
    )(z, z, ws, bs_t, h, w_out)
    return (outs[0], outs[1][None]) if emit else outs[0]


def _gmlp_gate_kernel(x_ref, g_ref, wu_ref, wv_ref, ws_ref, bs_ref, o_ref, xn_ref):
    @pl.when(pl.program_id(1) == 0)
    def _():
        xn_ref[...] = _rms_rows(x_ref[...], g_ref[...]).astype(BF16)

    def gelu(y):
        return 0.5 * y * (1.0 + lax.erf(y * (2.0 ** -0.5)))

    xn = xn_ref[...]
    u = gelu(jnp.dot(xn, wu_ref[...], preferred_element_type=F32))
    v = gelu(jnp.dot(xn, wv_ref[...], preferred_element_type=F32)).astype(BF16)
    r = lax.broadcasted_iota(jnp.int32, (CHUNK, CHUNK), 0)
    c = lax.broadcasted_iota(jnp.int32, (CHUNK, CHUNK), 1)
    for gg in range(ws_ref.shape[0]):
        wt = jnp.where(r >= c, ws_ref[gg], 0.0).astype(BF16)
        b = bs_ref[gg]
        b = jnp.concatenate([b] * (GROUP_DIM_A // LANES), axis=1)
        cs = slice(gg * GROUP_DIM_A, (gg + 1) * GROUP_DIM_A)
        for ci in range(u.shape[0] // CHUNK):
            rs = slice(ci * CHUNK, (ci + 1) * CHUNK)
            s = jnp.dot(wt, v[rs, cs], preferred_element_type=F32) + b
            o_ref[rs, cs] = (u[rs, cs] * s).astype(o_ref.dtype)


def _gmlp_gate(x, gl, wl, ws, bs, *, tm, tn):
    g, lg = gl
    w, l = wl
    M, K = x.shape
    gpt = tn // GROUP_DIM_A
    nt = D_GATE // tn
    bs_rep = jnp.broadcast_to(bs[:, :, None], bs.shape + (LANES,))
    return pl.pallas_call(
        _gmlp_gate_kernel,
        grid=(M // tm, nt),
        in_specs=[
            pl.BlockSpec((tm, K), lambda i, j: (i, 0)),
            pl.BlockSpec((None, 1, K), lambda i, j: (lg, 0, 0)),
            pl.BlockSpec((None, K, tn), lambda i, j: (l, 0, j)),
            pl.BlockSpec((None, K, tn), lambda i, j: (l, 0, nt + j)),
            pl.BlockSpec((gpt, CHUNK, CHUNK), lambda i, j: (j, 0, 0)),
            pl.BlockSpec((gpt, CHUNK, LANES), lambda i, j: (j, 0, 0)),
        ],
        out_specs=pl.BlockSpec((tm, tn), lambda i, j: (i, j)),
        out_shape=jax.ShapeDtypeStruct((M, D_GATE), BF16),
        scratch_shapes=[pltpu.VMEM((tm, K), BF16)],
        compiler_params=_params(2),
        name="gmlp_gate",
    )(x, g, w, w, ws, bs_rep)


def _ffn_kernel(*refs, final_norm, emit):
    x_ref, g_ref, wu_ref, wd_ref = refs[:4]
    refs = refs[4:]
    if final_norm:
        gf_ref = refs[0]
        refs = refs[1:]
    o_ref = refs[0]
    wub_ref, wdb_ref = refs[1:3] if emit else (None, None)
    xn_ref = refs[-1]
    f = pl.program_id(1)

    @pl.when(f == 0)
    def _():
        x = x_ref[...]
        xn_ref[...] = _rms_rows(x, g_ref[...]).astype(BF16)
        o_ref[...] = x

    a = jnp.dot(xn_ref[...], _mxu_weight(wu_ref, wub_ref), preferred_element_type=F32)
    a = jnp.square(jnp.maximum(a, 0.0)).astype(BF16)
    o_ref[...] += jnp.dot(a, _mxu_weight(wd_ref, wdb_ref), preferred_element_type=F32)

    if final_norm:
        @pl.when(f == pl.num_programs(1) - 1)
        def _():
            o_ref[...] = _rms_rows(o_ref[...], gf_ref[...])


def _ffn(x, gl, wul, wdl, g_final, *, tm, tf, emit=False):
    g, l = gl
    w_up, lu = wul
    w_down, ld = wdl
    M, K = x.shape
    F = w_up.shape[2]
    final_norm = g_final is not None
    in_specs = [
        pl.BlockSpec((tm, K), lambda i, f: (i, 0)),
        pl.BlockSpec((None, 1, K), lambda i, f: (l, 0, 0)),
        pl.BlockSpec((None, K, tf), lambda i, f: (lu, 0, f)),
        pl.BlockSpec((None, tf, K), lambda i, f: (ld, f, 0)),
    ]
    ub_specs, ub_shapes = _weight_copy_out(emit, M, tm, K, F, K, tf, lambda i, f: (0, f))
    db_specs, db_shapes = _weight_copy_out(emit, M, tm, F, K, tf, K, lambda i, f: (f, 0))
    args = [x, g, w_up, w_down]
    if final_norm:
        in_specs.append(pl.BlockSpec((1, K), lambda i, f: (0, 0)))
        args.append(g_final)
    outs = pl.pallas_call(
        functools.partial(_ffn_kernel, final_norm=final_norm, emit=emit),
        grid=(M // tm, F // tf),
        in_specs=in_specs,
        out_specs=[pl.BlockSpec((tm, K), lambda i, f: (i, 0))] + ub_specs + db_specs,
        out_shape=[jax.ShapeDtypeStruct((M, K), F32)] + ub_shapes + db_shapes,
        scratch_shapes=[pltpu.VMEM((tm, K), BF16)],
        compiler_params=_params(2),
        name="ffn",
    )(*args)
    return (outs[0], outs[1][None], outs[2][None]) if emit else outs[0]


def _head_rows(ref, h, n_heads):
    return ref[pl.ds(h, PAGE_SIZE, stride=n_heads), :].astype(BF16)


def _page_k(k_ref):
    return jnp.concatenate([_head_rows(k_ref, h, N_QK) for h in range(N_QK)], axis=1)


def _matmul_res_kernel(x_ref, w_ref, h_ref, o_ref, wb_ref=None):
    o_ref[...] = h_ref[...] + jnp.dot(x_ref[...], _mxu_weight(w_ref, wb_ref), preferred_element_type=F32)


def _matmul_res(x, wl, h, *, tm, tn, emit=False):
    w, l = wl
    M, K = x.shape
    N = w.shape[2]
    wb_specs, wb_shapes = _weight_copy_out(emit, M, tm, K, N, K, tn, lambda i, j: (0, j))
    outs = pl.pallas_call(
        _matmul_res_kernel,
        grid=(M // tm, N // tn),
        in_specs=[
            pl.BlockSpec((tm, K), lambda i, j: (i, 0)),
            pl.BlockSpec((None, K, tn), lambda i, j: (l, 0, j)),
            pl.BlockSpec((tm, tn), lambda i, j: (i, j)),
        ],
        out_specs=[pl.BlockSpec((tm, tn), lambda i, j: (i, j))] + wb_specs,
        out_shape=[jax.ShapeDtypeStruct((M, N), F32)] + wb_shapes,
        compiler_params=_params(2),
        name="matmul_res",
    )(x, w, h)
    return (outs[0], outs[1][None]) if emit else outs[0]


def _lane_chunks(s):
    return [s[:, c * LANES:(c + 1) * LANES] for c in range(s.shape[1] // LANES)]


def _chunk_max(chunks):
    m = chunks[0]
    for c in chunks[1:]:
        m = jnp.maximum(m, c)
    return m


def _softmax_step(s_list, m_ref, l_ref, m_blk=None):
    cols = [_lane_chunks(s) for s in s_list]
    if m_blk is None:
        m_blk = _chunk_max([c for cs in cols for c in cs])
    m_prev = m_ref[...]
    m_new = jnp.maximum(m_prev, jnp.max(m_blk, axis=-1, keepdims=True))
    alpha = jnp.exp2(m_prev - m_new)
    probs = [[jnp.exp2(c - m_new) for c in cs] for cs in cols]
    flat = [p for ps in probs for p in ps]
    l_blk = flat[0]
    for p in flat[1:]:
        l_blk = l_blk + p
    l_ref[...] = alpha * l_ref[...] + jnp.sum(l_blk, axis=-1, keepdims=True)
    m_ref[...] = m_new
    return [jnp.concatenate(ps, axis=1).astype(BF16) if len(ps) > 1 else ps[0].astype(BF16)
            for ps in probs], alpha


def _head_out(o1, o2, lam, subln, lam_init):
    d = o1 - lam * o2
    return _rms_rows(d, subln) * (1.0 - lam_init)


def _attn_prompt_kernel(qi_tab, kj_tab, q0_ref, k0_ref, qn_ref, kn_ref, v_ref, lam_ref, subln_ref, o_ref,
                        m_ref, l_ref, acc_ref, sa_ref, sb_ref, ma_ref, mb_ref, p_ref, a_ref, *, lam_init):
    step = pl.program_id(2)
    n_steps = pl.num_programs(2)
    qi = qi_tab[step]
    kj = kj_tab[step]
    nxt = jnp.where(step + 1 == n_steps, 0, step + 1)
    next_masked = qi_tab[nxt] == kj_tab[nxt]
    tq = qn_ref.shape[1]
    tk = kn_ref.shape[1]
    n_sub = m_ref.shape[0]

    def scores(q_ref, k_ref, bufs, masked):
        s_ref, mx_ref = bufs
        for sub in range(n_sub):
            sl = slice(sub * HEAD_DIM, (sub + 1) * HEAD_DIM)
            q = q_ref[0, :, sl]
            k = k_ref[0, :, sl].astype(BF16)
            s = lax.dot_general(q, k, (((1,), (1,)), ((), ())), preferred_element_type=F32)
            if masked:
                r = lax.broadcasted_iota(jnp.int32, (tq, tk), 0)
                c = lax.broadcasted_iota(jnp.int32, (tq, tk), 1)
                s = jnp.where(c <= r, s, NEG_INF)
            s_ref[sub] = s
            mx_ref[sub] = _chunk_max(_lane_chunks(s))

    def accumulate(bufs):
        s_ref, mx_ref = bufs
        for sub in range(n_sub):
            p, alpha = _softmax_step([s_ref[sub]], m_ref.at[sub], l_ref.at[sub], mx_ref[sub])
            p_ref[sub] = p[0]
            a_ref[sub] = alpha
        for hv in range(n_sub // 2):
            v = v_ref[0, :, hv * V_DIM:(hv + 1) * V_DIM].astype(BF16)
            pv = jnp.dot(p_ref[2 * hv:2 * hv + 2].reshape(2 * tq, tk), v, preferred_element_type=F32)
            for i in range(2):
                alpha = a_ref[2 * hv + i]
                acc_ref[2 * hv + i] = (jnp.concatenate([alpha, alpha], axis=1) * acc_ref[2 * hv + i]
                                       + pv[i * tq:(i + 1) * tq])

    @pl.when((pl.program_id(0) == 0) & (pl.program_id(1) == 0) & (step == 0))
    def _():
        scores(q0_ref, k0_ref, (sa_ref, ma_ref), True)

    @pl.when(kj == 0)
    def _():
        m_ref[...] = jnp.full(m_ref.shape, NEG_INF, F32)
        l_ref[...] = jnp.zeros(l_ref.shape, F32)
        acc_ref[...] = jnp.zeros(acc_ref.shape, F32)

    for parity, (cur_ref, nxt_ref) in enumerate((((sa_ref, ma_ref), (sb_ref, mb_ref)),
                                                 ((sb_ref, mb_ref), (sa_ref, ma_ref)))):
        for masked in (False, True):
            @pl.when((step % 2 == parity) & (next_masked == masked))
            def _(cur_ref=cur_ref, nxt_ref=nxt_ref, masked=masked):
                scores(qn_ref, kn_ref, nxt_ref, masked)
                accumulate(cur_ref)

    @pl.when(kj == qi)
    def _():
        lam = _lambda(lam_ref, lam_init)
        for hv in range(n_sub // 2):
            inv1 = 1.0 / l_ref[2 * hv]
            inv2 = 1.0 / l_ref[2 * hv + 1]
            o1 = acc_ref[2 * hv] * jnp.concatenate([inv1, inv1], axis=1)
            o2 = acc_ref[2 * hv + 1] * jnp.concatenate([inv2, inv2], axis=1)
            o_ref[0, :, hv * V_DIM:(hv + 1) * V_DIM] = _head_out(
                o1, o2, lam, subln_ref[...], lam_init).astype(o_ref.dtype)


def _attn_prompt(q, k, v, lam_vecs, subln, l, lam_init, *, tq, heads_per_step):
    B, T, _ = q.shape
    nq = T // tq
    pairs = [(i, j) for i in range(nq) for j in range(i + 1)]
    qi_tab = jnp.asarray([p[0] for p in pairs], jnp.int32)
    kj_tab = jnp.asarray([p[1] for p in pairs], jnp.int32)
    hw = heads_per_step * V_DIM
    n_sub = 2 * heads_per_step
    n_hg = N_HEADS // heads_per_step
    n_steps = len(pairs)
    assert n_steps % 2 == 0

    def next_block(tab):
        def index_map(b, h, s, qt, kt):
            wrap_s = (s + 1 == n_steps).astype(jnp.int32)
            s_n = (s + 1) * (1 - wrap_s)
            wrap_h = ((h + wrap_s) == n_hg).astype(jnp.int32)
            h_n = (h + wrap_s) * (1 - wrap_h)
            b_n = jnp.minimum(b + wrap_h, B - 1)
            return (b_n, (qt if tab == "q" else kt)[s_n], h_n)
        return index_map

    grid_spec = pltpu.PrefetchScalarGridSpec(
        num_scalar_prefetch=2,
        grid=(B, n_hg, n_steps),
        in_specs=[
            pl.BlockSpec((1, tq, hw), lambda b, h, s, qt, kt: (0, 0, 0)),
            pl.BlockSpec((1, tq, hw), lambda b, h, s, qt, kt: (0, 0, 0)),
            pl.BlockSpec((1, tq, hw), next_block("q")),
            pl.BlockSpec((1, tq, hw), next_block("k")),
            pl.BlockSpec((1, tq, hw), lambda b, h, s, qt, kt: (b, kt[s], h)),
            pl.BlockSpec((None, 4, HEAD_DIM), lambda b, h, s, qt, kt: (l, 0, 0)),
            pl.BlockSpec((None, 1, V_DIM), lambda b, h, s, qt, kt: (l, 0, 0)),
        ],
        out_specs=pl.BlockSpec((1, tq, hw), lambda b, h, s, qt, kt: (b, qt[s], h)),
        scratch_shapes=[pltpu.VMEM((n_sub, tq, LANES), F32), pltpu.VMEM((n_sub, tq, LANES), F32),
                        pltpu.VMEM((n_sub, tq, V_DIM), F32),
                        pltpu.VMEM((n_sub, tq, tq), F32), pltpu.VMEM((n_sub, tq, tq), F32),
                        pltpu.VMEM((n_sub, tq, LANES), F32), pltpu.VMEM((n_sub, tq, LANES), F32),
                        pltpu.VMEM((n_sub, tq, tq), BF16), pltpu.VMEM((n_sub, tq, LANES), F32)],
    )
    return pl.pallas_call(
        functools.partial(_attn_prompt_kernel, lam_init=lam_init),
        grid_spec=grid_spec,
        out_shape=jax.ShapeDtypeStruct((B, T, N_HEADS * V_DIM), BF16),
        compiler_params=_params(3),
        name="attn_prompt",
    )(qi_tab, kj_tab, q, k, q, k, v, lam_vecs, subln)


def _attn_sample_step(load_tiles, q_ref, kn_ref, vn_ref, lam_ref, subln_ref, o_ref,
                      qbd_ref, m_ref, l_ref, acc_ref, lam_init):
    p_idx = pl.program_id(1)
    tq = q_ref.shape[1]
    rows = N_QK * tq
    d_all = N_QK * HEAD_DIM

    @pl.when(p_idx == 0)
    def _():
        m_ref[...] = jnp.full(m_ref.shape, NEG_INF, F32)
        l_ref[...] = jnp.zeros(l_ref.shape, F32)
        acc_ref[...] = jnp.zeros(acc_ref.shape, F32)
        qt = jnp.concatenate([q_ref[0]] * N_QK, axis=0)
        r = lax.broadcasted_iota(jnp.int32, (rows, d_all), 0)
        c = lax.broadcasted_iota(jnp.int32, (rows, d_all), 1)
        qbd_ref[...] = jnp.where(r // tq == c // HEAD_DIM, qt, 0.0).astype(BF16)

    def update(k_list, v_list, mask):
        qbd = qbd_ref[...]
        s_list = []
        for kp in k_list:
            s = lax.dot_general(qbd, kp, (((1,), (1,)), ((), ())), preferred_element_type=F32)
            if mask is not None:
                s = jnp.where(mask, s, NEG_INF)
            s_list.append(s)
        p_list, alpha = _softmax_step(s_list, m_ref, l_ref)
        for hv in range(N_HEADS):
            rs = slice(hv * 2 * tq, (hv + 1) * 2 * tq)
            pv = None
            for p, v_heads in zip(p_list, v_list):
                t = jnp.dot(p[rs, :], v_heads[hv], preferred_element_type=F32)
                pv = t if pv is None else pv + t
            a = alpha[rs, :]
            acc_ref[rs, :] = jnp.concatenate([a, a], axis=1) * acc_ref[rs, :] + pv

    update(*load_tiles(), None)

    @pl.when(p_idx == pl.num_programs(1) - 1)
    def _():
        pad = jnp.zeros((PAGE_SIZE - tq, d_all), F32)
        kn = jnp.concatenate([kn_ref[0], pad], axis=0).astype(BF16)
        vn = jnp.concatenate([vn_ref[0], pad], axis=0).astype(BF16)
        r = lax.broadcasted_iota(jnp.int32, (rows, PAGE_SIZE), 0)
        c = lax.broadcasted_iota(jnp.int32, (rows, PAGE_SIZE), 1)
        update([kn], [[vn[:, hv * V_DIM:(hv + 1) * V_DIM] for hv in range(N_HEADS)]], c <= r % tq)
        lam = _lambda(lam_ref, lam_init)
        inv = 1.0 / l_ref[...]
        o = acc_ref[...] * jnp.concatenate([inv, inv], axis=1)
        for hv in range(N_HEADS):
            o1 = o[hv * 2 * tq:hv * 2 * tq + tq, :]
            o2 = o[hv * 2 * tq + tq:(hv + 1) * 2 * tq, :]
            o_ref[0, :, hv * V_DIM:(hv + 1) * V_DIM] = _head_out(
                o1, o2, lam, subln_ref[...], lam_init).astype(o_ref.dtype)


def _attn_sample_kernel(pt_ref, q_ref, *refs, pages, lam_init):
    k_refs = refs[:pages]
    v_refs = refs[pages:3 * pages]

    def load_tiles():
        ks = jnp.concatenate([_page_k(k) for k in k_refs], axis=0)
        vs = [jnp.concatenate(
            [jnp.concatenate([_head_rows(r, hv, N_HEADS) for r in v_refs[2 * i:2 * i + 2]], axis=1)
             for i in range(pages)], axis=0) for hv in range(N_HEADS)]
        return [ks], [vs]

    _attn_sample_step(load_tiles, q_ref, *refs[3 * pages:], lam_init)


def _attn_sample(q, cache_k, cache_v, page_table, k_new, v_new, lam_vecs, subln, l, lam_init, *,
                 pages_per_step):
    Bd, Tq, D = q.shape
    n_pages = page_table.shape[1]
    pps = pages_per_step

    def page_map(i, half):
        return lambda b, p, pt: (pt[b * n_pages + p * pps + i], 0, half)

    k_specs = [pl.BlockSpec((None, PAGE_SIZE * N_QK, HEAD_DIM), page_map(i, 0)) for i in range(pps)]
    v_specs = [pl.BlockSpec((None, PAGE_SIZE * N_HEADS, LANES), page_map(i, half))
               for i in range(pps) for half in range(V_DIM // LANES)]
    row_spec = pl.BlockSpec((1, Tq, D), lambda b, p, pt: (b, 0, 0))
    grid_spec = pltpu.PrefetchScalarGridSpec(
        num_scalar_prefetch=1,
        grid=(Bd, n_pages // pps),
        in_specs=[row_spec] + k_specs + v_specs + [
            row_spec, row_spec,
            pl.BlockSpec((None, 4, HEAD_DIM), lambda b, p, pt: (l, 0, 0)),
            pl.BlockSpec((None, 1, V_DIM), lambda b, p, pt: (l, 0, 0)),
        ],
        out_specs=row_spec,
        scratch_shapes=[pltpu.VMEM((N_QK * Tq, D), BF16),
                        pltpu.VMEM((N_QK * Tq, LANES), F32), pltpu.VMEM((N_QK * Tq, LANES), F32),
                        pltpu.VMEM((N_QK * Tq, V_DIM), F32)],
    )
    return pl.pallas_call(
        functools.partial(_attn_sample_kernel, pages=pps, lam_init=lam_init),
        grid_spec=grid_spec,
        out_shape=jax.ShapeDtypeStruct((Bd, Tq, D), BF16),
        compiler_params=_params(2),
        name="attn_sample",
    )(page_table.reshape(-1), q, *([cache_k] * pps), *([cache_v] * (2 * pps)), k_new, v_new, lam_vecs, subln)


def _rope_tables(pos):
    half = HEAD_DIM // 2
    inv = ROPE_THETA ** (-jnp.arange(half, dtype=F32) / half)
    ang = pos.astype(F32)[:, None] * inv[None, :]
    cos = jnp.cos(ang)
    sin = jnp.sin(ang)
    return jnp.concatenate([cos, cos], axis=-1), jnp.concatenate([-sin, sin], axis=-1)


class _Group:
    def __init__(self, x, pos, cfg):
        self.B, self.T, _ = x.shape
        self.h = x.reshape(self.B * self.T, D_MODEL)
        self.cfg = cfg
        cos, sin = _rope_tables(pos)
        if self.T < cfg["tm"]:
            cos = jnp.tile(cos, (cfg["tm"] // self.T, 1))
            sin = jnp.tile(sin, (cfg["tm"] // self.T, 1))
        self.rope = (cos, sin)
        self.v_rows = []
        self.k_sh = self.v_sh = self.k_mxu = self.v_mxu = None


def _layer(G, l, p, W, Wb=None, paged=None):
    cfg, B, T, h = G.cfg, G.B, G.T, G.h
    M = B * T
    tm, rows, chunk = cfg["tm"], cfg["rows"], cfg["chunk"]
    cos, sin = G.rope
    emit = Wb is not None

    def keep(name, outs):
        if not emit:
            return outs
        for n, wb in zip(name.split(","), outs[1:]):
            Wb[n].append((wb, 0))
        return outs[0]

    if l < N_A_LAYERS and cfg["fused_gate"]:
        assert not emit and chunk == CHUNK
        gated = _gmlp_gate(h, p["norm_a"][l], W["w_in_a"][l], p["w_s_a"][l], p["b_s_a"][l],
                           tm=cfg["tm_in"], tn=cfg["tn_in"])
        h = _matmul_res(gated, W["w_out_a"][l], h, tm=cfg["tm_gmlp"], tn=cfg["tn_gmlp"])
    elif l < N_A_LAYERS:
        z = keep("w_in_a", _norm_matmul(h, p["norm_a"][l], W["w_in_a"][l], tm=cfg["tm_in"], tn=cfg["tn_in"],
                                        out_dtype=cfg["z_dtype"], epilogue="gelu", emit=emit))
        ws = p["w_s_a"][l][:, :chunk, :chunk]
        bs_t = p["b_s_a"][l][:, :chunk].T
        if rows > chunk:
            ws = jnp.tile(ws, (1, rows // chunk, rows // chunk))
            bs_t = jnp.tile(bs_t, (rows // chunk, 1))
        h = keep("w_out_a", _gmlp_out(z, ws, bs_t, h, W["w_out_a"][l], tm=cfg["tm_gmlp"], tn=cfg["tn_gmlp"],
                                      rows=rows, chunk=chunk, emit=emit))
        G.v_rows.append(z[:, D_GATE:])
    else:
        if l == N_A_LAYERS:
            dup = paged is None
            k_sh = keep("w_k", _norm_matmul(h, p["norm_kv"][0], W["w_k"][0], tm=tm, tn=cfg["tn"], out_dtype=F32,
                                            epilogue="rope", rope=(cos, sin), emit=emit, dup=dup))
            v_sh = keep("w_v", _norm_matmul(h, p["norm_kv"][0], W["w_v"][0], tm=tm, tn=cfg["tn"], out_dtype=F32,
                                            emit=emit, dup=dup))
            if dup:
                (k_sh, G.k_mxu), (v_sh, G.v_mxu) = k_sh, v_sh
            G.k_sh, G.v_sh = k_sh, v_sh
        j = l - N_A_LAYERS
        lam_init = 0.8 - 0.6 * math.exp(-0.3 * l)
        q = keep("w_q", _norm_matmul(h, p["norm_b"][j], W["w_q"][j], tm=tm, tn=cfg["tn"],
                                     out_dtype=cfg["q_dtype"], epilogue="rope", rope=(cos, sin),
                                     scale=HEAD_DIM ** -0.5 * LOG2E, emit=emit))
        if paged is None:
            o = _attn_prompt(q.reshape(B, T, -1), G.k_mxu.reshape(B, T, -1), G.v_mxu.reshape(B, T, -1),
                             p["lam_vecs"], p["subln_b"], j, lam_init, tq=cfg["tq"],
                             heads_per_step=cfg["hps"])
        else:
            o = _attn_sample(q.reshape(B, T, -1), *paged, G.k_sh.reshape(B, T, -1), G.v_sh.reshape(B, T, -1),
                             p["lam_vecs"], p["subln_b"], j, lam_init, pages_per_step=cfg["pps"])
        h = keep("w_o_b", _matmul_res(o.reshape(M, -1), W["w_o_b"][j], h, tm=tm, tn=cfg["tn"], emit=emit))
    G.h = keep("w_up,w_down", _ffn(h, p["norm_ffn"][l], W["w_up"][l], W["w_down"][l],
                                   p["norm_f"] if l == DEPTH - 1 else None,
                                   tm=cfg["tm_ffn"], tf=cfg["tf"], emit=emit))


def kernel(x_prompt, x_sample, cache_k, cache_v, page_table, norm_a, w_in_a, w_s_a, b_s_a, w_out_a, norm_kv, w_k, w_v, norm_b, w_q, lambda_q1, lambda_k1, lambda_q2, lambda_k2, subln_b, w_o_b, norm_ffn, w_up, w_down, norm_f):
    def layers(a):
        return [(a, l) for l in range(a.shape[0])]

    def gains(g):
        return layers(g.reshape(g.shape[0], 1, g.shape[1]))

    p = dict(norm_a=gains(norm_a), w_s_a=w_s_a, b_s_a=b_s_a, norm_kv=gains(norm_kv[None]),
             norm_b=gains(norm_b),
             lam_vecs=jnp.stack([lambda_q1, lambda_k1, lambda_q2, lambda_k2], axis=1),
             subln_b=subln_b.reshape(subln_b.shape[0], 1, -1), norm_ffn=gains(norm_ffn),
             norm_f=norm_f.reshape(1, -1))
    W = dict(w_in_a=layers(w_in_a), w_out_a=layers(w_out_a), w_k=layers(w_k[None]), w_v=layers(w_v[None]),
             w_q=layers(w_q), w_o_b=layers(w_o_b), w_up=layers(w_up), w_down=layers(w_down))

    B, T, _ = x_prompt.shape
    Bd, Td, _ = x_sample.shape

    Ms = Bd * Td
    cfg_s = dict(tm=Ms, tn=1024, tm_in=Ms, tn_in=1024, tm_gmlp=Ms, tn_gmlp=512, tm_ffn=Ms, tf=512,
                 pps=8, rows=Ms, chunk=Td, z_dtype=F32, q_dtype=F32, fused_gate=False)
    cfg_p = dict(tm=512, tn=2048, tm_in=1024, tn_in=1024, tm_gmlp=1024, tn_gmlp=1024, tm_ffn=512, tf=2048,
                 tq=512, hps=4, rows=CHUNK, chunk=CHUNK, q_dtype=BF16, fused_gate=True)
    S = _Group(x_sample, PAST_LEN + jnp.arange(Td, dtype=jnp.int32), cfg_s)
    P = _Group(x_prompt, jnp.arange(T, dtype=jnp.int32), cfg_p)

    n_pool = cache_k.shape[0]
    paged = (cache_k.reshape(n_pool, PAGE_SIZE * N_QK, HEAD_DIM),
             cache_v.reshape(n_pool, PAGE_SIZE * N_HEADS, V_DIM), page_table)
    Wb = {name: [] for name in W}
    for l in range(DEPTH):
        _layer(S, l, p, W, Wb=Wb, paged=paged)
    for l in range(DEPTH):
        _layer(P, l, p, Wb)

    return (P.h.reshape(B, T, D_MODEL), S.h.reshape(Bd, Td, D_MODEL),
            P.k_sh.reshape(B, T, N_QK, HEAD_DIM), P.v_sh.reshape(B, T, N_HEADS, V_DIM),
            S.k_sh.reshape(Bd, Td, N_QK, HEAD_DIM), S.v_sh.reshape(Bd, Td, N_HEADS, V_DIM),
            jnp.stack(S.v_rows).reshape(N_A_LAYERS, Bd, Td, D_GATE))
```
